```python
import math
import jax
import jax.numpy as jnp
from jax import lax
import numpy as np

D_MODEL = 1024
BATCH = 2
SEQ = 16384
DEPTH = 2

GRID_W = 64
CTX_LEN = 256
EPS = 1e-6

RWKV_HEADS = 8
RWKV_HEAD_DIM = 64
RWKV_WIDTH = RWKV_HEADS * RWKV_HEAD_DIM
RWKV_DECAY_RANK = 64
RWKV_ICLR_RANK = 64
RWKV_GATE_RANK = 128
SHIFT_TAPS = 3

CONV_CHANNELS = 512
CONV_TAPS = 31

S5_WIDTH = 512
S5_GROUP = 16
S5_GROUPS = S5_WIDTH // S5_GROUP
S5_STATE = 64
S5_MAX_RE = -1e-4

RET_HEADS = 4
RET_HEAD_DIM = 128
RET_WIDTH = RET_HEADS * RET_HEAD_DIM
RET_CHUNK = 128
ROPE_BASE = 10000.0

N_BRANCH = 4
BRANCH_WIDTH = 512

PEER_HEADS = 8
PEER_KEYS = 128
PEER_EXPERTS = PEER_KEYS * PEER_KEYS
PEER_TOPK = 16
PEER_QUERY = 256
PEER_HALF = PEER_QUERY // 2
PEER_BLOCK = 128

RWKV_IN = 3 * RWKV_WIDTH + 2 * RWKV_DECAY_RANK + 2 * RWKV_ICLR_RANK + RWKV_GATE_RANK
CONV_IN = 2 * CONV_CHANNELS
S5_IN = S5_WIDTH
RET_IN = 5 * RET_WIDTH
A_END = RWKV_IN
B_END = A_END + CONV_IN
C_END = B_END + S5_IN
IN_WIDTH = C_END + RET_IN
RWKV_SPLITS = (RWKV_WIDTH, 2 * RWKV_WIDTH, 3 * RWKV_WIDTH,
               3 * RWKV_WIDTH + RWKV_DECAY_RANK, 3 * RWKV_WIDTH + 2 * RWKV_DECAY_RANK,
               3 * RWKV_WIDTH + 2 * RWKV_DECAY_RANK + RWKV_ICLR_RANK,
               3 * RWKV_WIDTH + 2 * RWKV_DECAY_RANK + 2 * RWKV_ICLR_RANK)

kernel_name = "hybrid_rwkv7_conformer_s5_retention_peer_dit"


def rms_norm(z, g):
    zf = z.astype(jnp.float32)
    y = zf * lax.rsqrt(jnp.mean(zf * zf, axis=-1, keepdims=True) + EPS)
    return (y * g).astype(z.dtype)


def _normalise(z):
    zf = z.astype(jnp.float32)
    zc = zf - jnp.mean(zf, axis=-1, keepdims=True)
    return zc * lax.rsqrt(jnp.mean(zc * zc, axis=-1, keepdims=True) + EPS)


def modulate(z, shift, scale):
    return z * (1.0 + scale) + shift


def depthwise_conv(z, w):
    taps = w.shape[0]
    pad = (taps - 1) // 2
    return lax.conv_general_dilated(
        z, w[:, None, :].astype(z.dtype), window_strides=(1,), padding=[(pad, pad)],
        dimension_numbers=("NWC", "WIO", "NWC"), feature_group_count=z.shape[-1])


def axial_rope(n_tokens):
    rows = n_tokens // GRID_W
    row = jnp.repeat(jnp.arange(rows, dtype=jnp.float32), GRID_W)
    col = jnp.tile(jnp.arange(GRID_W, dtype=jnp.float32), rows)
    n_freq = RET_HEAD_DIM // 4
    inv = ROPE_BASE ** (-jnp.arange(n_freq, dtype=jnp.float32) / n_freq)
    ang = jnp.concatenate([row[:, None] * inv, col[:, None] * inv], axis=-1)
    return jnp.cos(ang), jnp.sin(ang)


def apply_rope(z, cos, sin):
    z1, z2 = jnp.split(z, 2, axis=-1)
    cs, sn = cos[None, :, None, :], sin[None, :, None, :]
    return jnp.concatenate([z1 * cs - z2 * sn, z1 * sn + z2 * cs], axis=-1)


def rwkv7_prep(p, shift_w, w0, w2, a0, a2, g2, k_k, k_a):
    b, t, _ = p.shape
    heads = lambda z: z.astype(jnp.float32).reshape(b, t, RWKV_HEADS, RWKV_HEAD_DIM)
    p = depthwise_conv(p, shift_w)
    r, k, v, wl_f, wl_b, al_f, al_b, gl = jnp.split(p, RWKV_SPLITS, axis=-1)
    kk = heads(k * k_k)
    kk = kk * lax.rsqrt(jnp.sum(kk * kk, axis=-1, keepdims=True) + EPS)
    g = jax.nn.sigmoid(gl) @ g2
    per_dir = []
    for d, (wl, al) in enumerate(((wl_f, al_f), (wl_b, al_b))):
        w_log = -jax.nn.softplus(-(w0[d] + jnp.tanh(wl) @ w2[d])) - 0.5
        decay = jnp.exp(-jnp.exp(w_log.astype(jnp.float32)))
        a = jax.nn.sigmoid(a0[d] + al @ a2[d])
        k_d = k * (1.0 + (a - 1.0) * k_a)
        per_dir.append((heads(k_d), heads(decay), heads(a)))
    return heads(r), heads(v), kk, g, per_dir


def rwkv7_scan(r, k, v, w, kk, a, s0, reverse, emit):
    def step(s, inp):
        r_t, k_t, v_t, w_t, kk_t, a_t = inp
        s_kk = jnp.einsum("bhvk,bhk->bhv", s, kk_t)
        s = (s * w_t[:, :, None, :] - s_kk[..., None] * (a_t * kk_t)[:, :, None, :]
             + v_t[..., None] * k_t[:, :, None, :])
        if emit:
            return s, jnp.einsum("bhvk,bhk->bhv", s, r_t)
        return s, None
    xs = tuple(jnp.moveaxis(z, 1, 0) for z in (r, k, v, w, kk, a))
    s_final, ys = lax.scan(step, s0, xs, reverse=reverse)
    return (jnp.moveaxis(ys, 0, 1) if emit else None), s_final


def rwkv7_readout(y, r, v, per_dir, g, r_k, gn_g, gn_b):
    b, t = y.shape[:2]
    k_sum = per_dir[0][0] + per_dir[1][0]
    bonus = jnp.sum(r * k_sum * r_k, axis=-1, keepdims=True) * v
    o = _normalise(y).reshape(b, t, RWKV_WIDTH) * gn_g + gn_b + bonus.reshape(b, t, RWKV_WIDTH)
    return o * g


def rwkv7_mixer(p_lat, p_ctx, shift_w, w0, w2, a0, a2, g2, k_k, k_a, r_k, gn_g, gn_b, emit_ctx):
    prep = lambda p: rwkv7_prep(p, shift_w, w0, w2, a0, a2, g2, k_k, k_a)
    r_l, v_l, kk_l, g_l, dirs_l = prep(p_lat)
    r_c, v_c, kk_c, g_c, dirs_c = prep(p_ctx)
    s_zero = jnp.zeros((p_ctx.shape[0], RWKV_HEADS, RWKV_HEAD_DIM, RWKV_HEAD_DIM), jnp.float32)
    y_l, y_c = 0.0, 0.0
    for d in range(2):
        rev = d == 1
        k_c, w_c, a_c = dirs_c[d]
        yc, s_ctx = rwkv7_scan(r_c, k_c, v_c, w_c, kk_c, a_c, s_zero, rev, emit_ctx)
        k_l, w_l, a_l = dirs_l[d]
        yl, _ = rwkv7_scan(r_l, k_l, v_l, w_l, kk_l, a_l, s_ctx, rev, True)
        y_l = y_l + yl
        if emit_ctx:
            y_c = y_c + yc
    out_l = rwkv7_readout(y_l, r_l, v_l, dirs_l, g_l, r_k, gn_g, gn_b).astype(p_lat.dtype)
    if not emit_ctx:
        return out_l, None
    out_c = rwkv7_readout(y_c, r_c, v_c, dirs_c, g_c, r_k, gn_g, gn_b).astype(p_ctx.dtype)
    return out_l, out_c


def conformer_conv(p, dw, db, ln_g, ln_b):
    val, gate = jnp.split(p, 2, axis=-1)
    z = val * jax.nn.sigmoid(gate)
    z = depthwise_conv(z, dw) + db
    z = _normalise(z) * ln_g + ln_b
    return jax.nn.silu(z).astype(p.dtype)


def s5_discretise(lam_re, lam_im, log_dt, b_re, b_im):
    lam_re = jnp.minimum(lam_re.astype(jnp.float32), S5_MAX_RE)
    lam_im = lam_im.astype(jnp.float32)
    dt = jnp.exp(log_dt.astype(jnp.float32))[:, None]
    mag = jnp.exp(lam_re * dt)
    ang = lam_im * dt
    ab_re, ab_im = mag * jnp.cos(ang), mag * jnp.sin(ang)
    den = lam_re * lam_re + lam_im * lam_im
    nr, ni = ab_re - 1.0, ab_im
    f_re = (nr * lam_re + ni * lam_im) / den
    f_im = (ni * lam_re - nr * lam_im) / den
    b_re, b_im = b_re.astype(jnp.float32), b_im.astype(jnp.float32)
    bb_re = f_re[..., None] * b_re - f_im[..., None] * b_im
    bb_im = f_re[..., None] * b_im + f_im[..., None] * b_re
    return ab_re, ab_im, bb_re, bb_im


def _s5_combine(e1, e2):
    a1r, a1i, b1r, b1i = e1
    a2r, a2i, b2r, b2i = e2
    return (a1r * a2r - a1i * a2i, a1r * a2i + a1i * a2r,
            a2r * b1r - a2i * b1i + b2r, a2r * b1i + a2i * b1r + b2i)


def s5_states(u, ab_re, ab_im, bb_re, bb_im, h0_re, h0_im, reverse):
    bu_re = jnp.einsum("gph,tbgh->tbgp", bb_re, u)
    bu_im = jnp.einsum("gph,tbgh->tbgp", bb_im, u)
    first = -1 if reverse else 0
    bu_re = bu_re.at[first].add(ab_re * h0_re - ab_im * h0_im)
    bu_im = bu_im.at[first].add(ab_re * h0_im + ab_im * h0_re)
    t = u.shape[0]
    a_re = jnp.broadcast_to(ab_re, (t, 1) + ab_re.shape)
    a_im = jnp.broadcast_to(ab_im, (t, 1) + ab_im.shape)
    _, _, x_re, x_im = lax.associative_scan(_s5_combine, (a_re, a_im, bu_re, bu_im), reverse=reverse, axis=0)
    return x_re, x_im


def s5_readout(x_re, x_im, c_re, c_im):
    y = (jnp.einsum("ghp,tbgp->btgh", c_re.astype(jnp.float32), x_re)
         - jnp.einsum("ghp,tbgp->btgh", c_im.astype(jnp.float32), x_im))
    b, t = y.shape[:2]
    return y.reshape(b, t, S5_WIDTH)


def s5_mixer(u_lat, u_ctx, lam_re, lam_im, log_dt, b_re, b_im, c_re, c_im, d_skip, glu_w, glu_b, emit_ctx):
    def time_groups(u):
        b, t, _ = u.shape
        return jnp.moveaxis(u.astype(jnp.float32), 1, 0).reshape(t, b, S5_GROUPS, S5_GROUP)
    ul, uc = time_groups(u_lat), time_groups(u_ctx)
    h_zero = jnp.zeros((u_ctx.shape[0], S5_GROUPS, S5_STATE), jnp.float32)
    y_lat = d_skip * u_lat.astype(jnp.float32)
    y_ctx = d_skip * u_ctx.astype(jnp.float32) if emit_ctx else None
    for d in range(2):
        rev = d == 1
        ab_re, ab_im, bb_re, bb_im = s5_discretise(lam_re[d], lam_im[d], log_dt[d], b_re[d], b_im[d])
        xc_re, xc_im = s5_states(uc, ab_re, ab_im, bb_re, bb_im, h_zero, h_zero, rev)
        last = 0 if rev else -1
        xl_re, xl_im = s5_states(ul, ab_re, ab_im, bb_re, bb_im, xc_re[last], xc_im[last], rev)
        y_lat = y_lat + s5_readout(xl_re, xl_im, c_re[d], c_im[d])
        if emit_ctx:
            y_ctx = y_ctx + s5_readout(xc_re, xc_im, c_re[d], c_im[d])

    def glu(y):
        z = jax.nn.gelu(y)
        return z * jax.nn.sigmoid(z @ glu_w + glu_b)
    out_l = glu(y_lat).astype(u_lat.dtype)
    if not emit_ctx:
        return out_l, None
    return out_l, glu(y_ctx).astype(u_ctx.dtype)


def retention_chunkwise(q, k, v, r0, log_gamma, strict, emit):
    b, t, h, _ = q.shape
    n_chunks = t // RET_CHUNK
    chunks = lambda z: z.reshape(b, n_chunks, RET_CHUNK, h, z.shape[-1]).transpose(1, 0, 3, 2, 4)
    pos = jnp.arange(RET_CHUNK, dtype=jnp.float32)
    diff = pos[:, None] - pos[None, :]
    keep = diff > 0 if strict else diff >= 0
    decay_in = jnp.where(keep, jnp.exp(jnp.where(keep, diff, 0.0) * log_gamma[:, None, None]), 0.0)
    decay_q = jnp.exp((pos + 1.0) * log_gamma[:, None])[None, :, :, None]
    decay_k = jnp.exp((RET_CHUNK - 1.0 - pos) * log_gamma[:, None])[None, :, :, None]
    decay_chunk = jnp.exp(RET_CHUNK * log_gamma)[None, :, None, None]

    def step(r, inp):
        qc, kc, vc = inp
        r_next = decay_chunk * r + jnp.einsum("bhck,bhcv->bhkv", kc * decay_k, vc)
        if not emit:
            return r_next, None
        scores = jnp.einsum("bhnk,bhmk->bhnm", qc, kc) * decay_in
        o = jnp.einsum("bhnm,bhmv->bhnv", scores, vc) + jnp.einsum("bhnk,bhkv->bhnv", qc * decay_q, r)
        return r_next, o
    r_final, o = lax.scan(step, r0, (chunks(q), chunks(k), chunks(v)))
    if emit:
        o = o.transpose(1, 0, 3, 2, 4).reshape(b, t, h, v.shape[-1])
    return o, r_final


def retention_mixer(p_lat, p_ctx, gn_g, gn_b, rope_cos, rope_sin, emit_ctx):
    log_gamma = jnp.log1p(-jnp.exp2(-5.0 - jnp.arange(RET_HEADS, dtype=jnp.float32)))
    k_scale = RET_HEAD_DIM ** -0.5

    def split_heads(p):
        b, t, _ = p.shape
        q, kf, kb, v, g = jnp.split(p.astype(jnp.float32), 5, axis=-1)
        hd = lambda z: z.reshape(b, t, RET_HEADS, RET_HEAD_DIM)
        return hd(q), hd(kf) * k_scale, hd(kb) * k_scale, hd(v), g
    ql, kfl, kbl, vl, gl = split_heads(p_lat)
    ql, kfl, kbl = (apply_rope(z, rope_cos, rope_sin) for z in (ql, kfl, kbl))
    qc, kfc, kbc, vc, gc = split_heads(p_ctx)
    r_zero = jnp.zeros((p_ctx.shape[0], RET_HEADS, RET_HEAD_DIM, RET_HEAD_DIM), jnp.float32)
    flip = lambda z: jnp.flip(z, axis=1)
    oc_f, rc_f = retention_chunkwise(qc, kfc, vc, r_zero, log_gamma, False, emit_ctx)
    oc_b, rc_b = retention_chunkwise(flip(qc), flip(kbc), flip(vc), r_zero, log_gamma, True, emit_ctx)
    ol_f, _ = retention_chunkwise(ql, kfl, vl, rc_f, log_gamma, False, True)
    ol_b, _ = retention_chunkwise(flip(ql), flip(kbl), flip(vl), rc_b, log_gamma, True, True)

    def readout(o, g):
        b, t = o.shape[:2]
        return (_normalise(o).reshape(b, t, RET_WIDTH) * gn_g + gn_b) * jax.nn.silu(g)
    out_l = readout(ol_f + flip(ol_b), gl).astype(p_lat.dtype)
    if not emit_ctx:
        return out_l, None
    return out_l, readout(oc_f + flip(oc_b), gc).astype(p_ctx.dtype)


def merge_branches(n, ys, w_branch, w_merge, b_merge, w_out):
    m = 0.0
    for i, y in enumerate(ys):
        gate = jax.nn.sigmoid(n @ w_merge[i] + b_merge[i])
        m = m + gate * (y.astype(n.dtype) @ w_branch[i])
    return m @ w_out


def token_mixers(n_lat, n_ctx, w_in, rwkv_shift, rwkv_w0, rwkv_w2, rwkv_a0, rwkv_a2, rwkv_g2, rwkv_kk,
                 rwkv_ka, rwkv_rk, rwkv_gn_g, rwkv_gn_b, conv_dw, conv_db, conv_ln_g, conv_ln_b,
                 s5_lam_re, s5_lam_im, s5_log_dt, s5_b_re, s5_b_im, s5_c_re, s5_c_im, s5_d, s5_glu_w,
                 s5_glu_b, ret_gn_g, ret_gn_b, w_branch, w_merge, b_merge, w_out, rope_cos, rope_sin,
                 emit_ctx):
    proj = lambda z, lo, hi: z @ w_in[:, lo:hi]
    ya_l, ya_c = rwkv7_mixer(proj(n_lat, 0, A_END), proj(n_ctx, 0, A_END), rwkv_shift, rwkv_w0, rwkv_w2,
                             rwkv_a0, rwkv_a2, rwkv_g2, rwkv_kk, rwkv_ka, rwkv_rk, rwkv_gn_g, rwkv_gn_b, emit_ctx)
    yb_l = conformer_conv(proj(n_lat, A_END, B_END), conv_dw, conv_db, conv_ln_g, conv_ln_b)
    yc_l, yc_c = s5_mixer(proj(n_lat, B_END, C_END), proj(n_ctx, B_END, C_END), s5_lam_re, s5_lam_im,
                          s5_log_dt, s5_b_re, s5_b_im, s5_c_re, s5_c_im, s5_d, s5_glu_w, s5_glu_b, emit_ctx)
    yd_l, yd_c = retention_mixer(proj(n_lat, C_END, IN_WIDTH), proj(n_ctx, C_END, IN_WIDTH),
                                 ret_gn_g, ret_gn_b, rope_cos, rope_sin, emit_ctx)
    m_lat = merge_branches(n_lat, (ya_l, yb_l, yc_l, yd_l), w_branch, w_merge, b_merge, w_out)
    if not emit_ctx:
        return m_lat, None
    yb_c = conformer_conv(proj(n_ctx, A_END, B_END), conv_dw, conv_db, conv_ln_g, conv_ln_b)
    m_ctx = merge_branches(n_ctx, (ya_c, yb_c, yc_c, yd_c), w_branch, w_merge, b_merge, w_out)
    return m_lat, m_ctx


def peer_ffn(h, w_q, sub_keys, exp_u, exp_v):
    b, t, d = h.shape
    blocks = h.reshape(-1, PEER_BLOCK, d)

    def one_block(z):
        n = z.shape[0]
        q = (z @ w_q).reshape(n, PEER_HEADS, 2, PEER_HALF)
        s = jnp.einsum("nhpk,hpek->nhpe", q, sub_keys).astype(jnp.float32)
        s_top, i_top = lax.top_k(s, PEER_TOPK)
        cand = (s_top[:, :, 0, :, None] + s_top[:, :, 1, None, :]).reshape(n, PEER_HEADS, PEER_TOPK * PEER_TOPK)
        cand_id = (i_top[:, :, 0, :, None] * PEER_KEYS + i_top[:, :, 1, None, :]).reshape(
            n, PEER_HEADS, PEER_TOPK * PEER_TOPK)
        best, pos = lax.top_k(cand, PEER_TOPK)
        ids = jnp.take_along_axis(cand_id, pos, axis=-1)
        gate = jax.nn.softmax(best, axis=-1)
        act = jax.nn.gelu(jnp.einsum("nhkd,nd->nhk", exp_u[ids], z))
        return jnp.einsum("nhk,nhkd->nd", (gate * act).astype(z.dtype), exp_v[ids])
    return lax.map(one_block, blocks).reshape(b, t, d)


def setup_inputs(seed: int = 0) -> dict:
    key = jax.random.key(seed)
    keys = iter(jax.random.split(key, 64))

    def nrm(shape, scale):
        return scale * jax.random.normal(next(keys), shape, jnp.float32)

    def uni(shape, lo, hi):
        return jax.random.uniform(next(keys), shape, jnp.float32, lo, hi)
    L, D = DEPTH, D_MODEL
    shift_base = jnp.array([0.25, 0.5, 0.25], jnp.float32)[None, :, None]
    lam_im0 = math.pi * jnp.arange(S5_STATE, dtype=jnp.float32)
    return {
        "x": nrm((BATCH, SEQ, D), 1.0),
        "c": nrm((BATCH, D), 1.0),
        "ctx": nrm((BATCH, CTX_LEN, D), 1.0),
        "c_ctx": nrm((D,), 1.0),
        "ada_w": nrm((L, D, 6 * D), 0.5 * D ** -0.5),
        "ada_b": nrm((L, 6 * D), 0.02),
        "norm1_g": 1.0 + nrm((L, D), 0.02),
        "norm2_g": 1.0 + nrm((L, D), 0.02),
        "w_in": nrm((L, D, IN_WIDTH), D ** -0.5),
        "rwkv_shift": shift_base + nrm((L, SHIFT_TAPS, RWKV_IN), 0.05),
        "rwkv_w0": uni((L, 2, RWKV_WIDTH), -6.0, 1.0),
        "rwkv_w2": nrm((L, 2, RWKV_DECAY_RANK, RWKV_WIDTH), 0.2 * RWKV_DECAY_RANK ** -0.5),
        "rwkv_a0": nrm((L, 2, RWKV_WIDTH), 0.5),
        "rwkv_a2": nrm((L, 2, RWKV_ICLR_RANK, RWKV_WIDTH), 0.5 * RWKV_ICLR_RANK ** -0.5),
        "rwkv_g2": nrm((L, RWKV_GATE_RANK, RWKV_WIDTH), RWKV_GATE_RANK ** -0.5),
        "rwkv_kk": 1.0 + nrm((L, RWKV_WIDTH), 0.1),
        "rwkv_ka": 1.0 + nrm((L, RWKV_WIDTH), 0.1),
        "rwkv_rk": nrm((L, RWKV_HEADS, RWKV_HEAD_DIM), 0.1),
        "rwkv_gn_g": 1.0 + nrm((L, RWKV_WIDTH), 0.02),
        "rwkv_gn_b": nrm((L, RWKV_WIDTH), 0.02),
        "conv_dw": nrm((L, CONV_TAPS, CONV_CHANNELS), CONV_TAPS ** -0.5),
        "conv_db": nrm((L, CONV_CHANNELS), 0.02),
        "conv_ln_g": 1.0 + nrm((L, CONV_CHANNELS), 0.02),
        "conv_ln_b": nrm((L, CONV_CHANNELS), 0.02),
        "s5_lam_re": -0.5 + nrm((L, 2, S5_GROUPS, S5_STATE), 0.01),
        "s5_lam_im": lam_im0 + nrm((L, 2, S5_GROUPS, S5_STATE), 0.01),
        "s5_log_dt": uni((L, 2, S5_GROUPS), math.log(1e-3), math.log(1e-1)),
        "s5_b_re": nrm((L, 2, S5_GROUPS, S5_STATE, S5_GROUP), (2 * S5_GROUP) ** -0.5),
        "s5_b_im": nrm((L, 2, S5_GROUPS, S5_STATE, S5_GROUP), (2 * S5_GROUP) ** -0.5),
        "s5_c_re": nrm((L, 2, S5_GROUPS, S5_GROUP, S5_STATE), S5_STATE ** -0.5),
        "s5_c_im": nrm((L, 2, S5_GROUPS, S5_GROUP, S5_STATE), S5_STATE ** -0.5),
        "s5_d": nrm((L, S5_WIDTH), 1.0),
        "s5_glu_w": nrm((L, S5_WIDTH, S5_WIDTH), S5_WIDTH ** -0.5),
        "s5_glu_b": nrm((L, S5_WIDTH), 0.02),
        "ret_gn_g": 1.0 + nrm((L, RET_WIDTH), 0.02),
        "ret_gn_b": nrm((L, RET_WIDTH), 0.02),
        "w_branch": nrm((L, N_BRANCH, BRANCH_WIDTH, D), BRANCH_WIDTH ** -0.5),
        "w_merge": nrm((L, N_BRANCH, D, D), D ** -0.5),
        "b_merge": nrm((L, N_BRANCH, D), 0.02),
        "w_out": nrm((L, D, D), D ** -0.5),
        "peer_wq": nrm((L, D, PEER_HEADS * PEER_QUERY), D ** -0.5),
        "peer_keys": nrm((L, PEER_HEADS, 2, PEER_KEYS, PEER_HALF), PEER_HALF ** -0.5),
        "peer_u": nrm((L, PEER_EXPERTS, D), D ** -0.5),
        "peer_v": nrm((L, PEER_EXPERTS, D), 1.0),
        "final_g": 1.0 + nrm((D,), 0.02),
    }


def reference(x, c, ctx, c_ctx, ada_w, ada_b, norm1_g, norm2_g, w_in, rwkv_shift, rwkv_w0, rwkv_w2,
              rwkv_a0, rwkv_a2, rwkv_g2, rwkv_kk, rwkv_ka, rwkv_rk, rwkv_gn_g, rwkv_gn_b,
              conv_dw, conv_db, conv_ln_g, conv_ln_b, s5_lam_re, s5_lam_im, s5_log_dt,
              s5_b_re, s5_b_im, s5_c_re, s5_c_im, s5_d, s5_glu_w, s5_glu_b, ret_gn_g, ret_gn_b,
              w_branch, w_merge, b_merge, w_out, peer_wq, peer_keys, peer_u, peer_v, final_g):
    rope_cos, rope_sin = axial_rope(x.shape[1])
    for l in range(DEPTH):
        emit_ctx = l < DEPTH - 1
        mod = jax.nn.silu(c) @ ada_w[l] + ada_b[l]
        sh1, sc1, gt1, sh2, sc2, gt2 = jnp.split(mod[:, None, :], 6, axis=-1)
        cmod = jax.nn.silu(c_ctx) @ ada_w[l] + ada_b[l]
        csh1, csc1, cgt1, csh2, csc2, cgt2 = jnp.split(cmod, 6, axis=-1)
        n_lat = modulate(rms_norm(x, norm1_g[l]), sh1, sc1)
        n_ctx = modulate(rms_norm(ctx, norm1_g[l]), csh1, csc1)
        m_lat, m_ctx = token_mixers(
            n_lat, n_ctx, w_in[l], rwkv_shift[l], rwkv_w0[l], rwkv_w2[l], rwkv_a0[l], rwkv_a2[l],
            rwkv_g2[l], rwkv_kk[l], rwkv_ka[l], rwkv_rk[l], rwkv_gn_g[l], rwkv_gn_b[l],
            conv_dw[l], conv_db[l], conv_ln_g[l], conv_ln_b[l], s5_lam_re[l], s5_lam_im[l],
            s5_log_dt[l], s5_b_re[l], s5_b_im[l], s5_c_re[l], s5_c_im[l], s5_d[l], s5_glu_w[l],
            s5_glu_b[l], ret_gn_g[l], ret_gn_b[l], w_branch[l], w_merge[l], b_merge[l], w_out[l],
            rope_cos, rope_sin, emit_ctx)
        x = x + gt1 * m_lat
        x = x + gt2 * peer_ffn(modulate(rms_norm(x, norm2_g[l]), sh2, sc2),
                               peer_wq[l], peer_keys[l], peer_u[l], peer_v[l])
        if emit_ctx:
            ctx = ctx + cgt1 * m_ctx
            ctx = ctx + cgt2 * peer_ffn(modulate(rms_norm(ctx, norm2_g[l]), csh2, csc2),
                                        peer_wq[l], peer_keys[l], peer_u[l], peer_v[l])
    return rms_norm(x, final_g)
```

```python
import functools
import math

import jax
import jax.numpy as jnp
from jax import lax
from jax.experimental import pallas as pl
from jax.experimental.pallas import tpu as pltpu

F32 = jnp.float32
BF16 = jnp.bfloat16

D_MODEL = 1024
GRID_W = 64
EPS = 1e-6

RWKV_HEADS = 8
RWKV_HEAD_DIM = 64
RWKV_WIDTH = RWKV_HEADS * RWKV_HEAD_DIM
RWKV_DECAY_RANK = 64
RWKV_ICLR_RANK = 64
RWKV_GATE_RANK = 128
RWKV_CHUNK = 64
LANES = 128


def _bdot(a, b, dims):
    return lax.dot_general(a.astype(BF16), b.astype(BF16), (dims, ((), ())), preferred_element_type=F32)


def _mm(a, b):
    return _bdot(a, b, ((1,), (0,)))


def _mm_nt(a, b):
    return _bdot(a, b, ((1,), (1,)))


def _mm_tn(a, b):
    return _bdot(a, b, ((0,), (0,)))


def _rwkv_chunk_kernel(r_ref, v_ref, kk_ref, k_ref, b_ref, lw_ref, y_ref, s_ref):
    c = RWKV_CHUNK
    d = pl.program_id(1)
    j = pl.program_id(2)

    @pl.when(j == 0)
    def _():
        s_ref[...] = jnp.zeros_like(s_ref)

    sign = jnp.where(d == 0, 1, -1)
    row = lax.broadcasted_iota(jnp.int32, (c, c), 0)
    col = lax.broadcasted_iota(jnp.int32, (c, c), 1)
    cum_mat = jnp.where((row - col) * sign >= 0, 1.0, 0.0).astype(BF16)

    lw = lw_ref[0, 0]
    lw_hi = lw.astype(BF16)
    rem = lw - lw_hi.astype(F32)
    lw_mid = rem.astype(BF16)
    lw_lo = (rem - lw_mid.astype(F32)).astype(BF16)
    lc_incl = _mm(cum_mat, lw_hi) + _mm(cum_mat, lw_mid) + _mm(cum_mat, lw_lo)
    lc_excl = lc_incl - lw
    lc_ref = 0.5 * jnp.sum(lw, axis=0, keepdims=True)

    row2 = lax.broadcasted_iota(jnp.int32, (2 * c, 2 * c), 0)
    col2 = lax.broadcasted_iota(jnp.int32, (2 * c, 2 * c), 1)
    same_head = (row2 // c) == (col2 // c)
    tdiff = jnp.where(same_head, ((row2 % c) - (col2 % c)) * sign, -1)
    strict2 = tdiff > 0
    incl2 = tdiff >= 0
    eye2 = jnp.where(row2 == col2, 1.0, 0.0)
    lane_head = lax.broadcasted_iota(jnp.int32, (c, LANES), 1) // RWKV_HEAD_DIM

    def stack2(z):
        return jnp.concatenate([jnp.where(lane_head == 0, z, 0.0), jnp.where(lane_head == 1, z, 0.0)], axis=0)

    for p in range(RWKV_WIDTH // LANES):
        sl = slice(p * LANES, (p + 1) * LANES)
        r, v, kk = r_ref[0, :, sl], v_ref[0, :, sl], kk_ref[0, :, sl]
        k, b = k_ref[0, 0, :, sl], b_ref[0, 0, :, sl]
        ref = lc_ref[:, sl]
        e_in = jnp.exp(lc_incl[:, sl] - ref)
        e_ex = jnp.exp(lc_excl[:, sl] - ref)
        e_inv = jnp.exp(ref - lc_incl[:, sl])
        e_ref = jnp.exp(ref)
        q_stack = jnp.concatenate([stack2(kk * e_ex), stack2(r * e_in)], axis=0)
        k_stack = jnp.concatenate([stack2(k * e_inv), stack2(b * e_inv)], axis=0)
        x = _mm_nt(q_stack, k_stack)
        m_k = jnp.where(strict2, x[:2 * c, :2 * c], 0.0)
        m_b = jnp.where(strict2, x[:2 * c, 2 * c:], 0.0)
        p_k = jnp.where(incl2, x[2 * c:, :2 * c], 0.0)
        p_b = jnp.where(incl2, x[2 * c:, 2 * c:], 0.0)
        t_inv = eye2 - m_b
        m_pow = m_b
        for _ in range(int(math.log2(c)) - 1):
            m_pow = _mm(m_pow, m_pow)
            t_inv = t_inv + _mm(t_inv, m_pow)
        s0 = s_ref[p]
        v2 = stack2(v)
        q0_stack = jnp.concatenate([stack2(kk * (e_ex * e_ref)), stack2(r * (e_in * e_ref))], axis=0)
        from_state = _mm_nt(q0_stack, s0)
        from_chunk = _mm(jnp.concatenate([m_k, p_k], axis=0), v2)
        u2 = _mm(t_inv, from_state[:2 * c] + from_chunk[:2 * c])
        y2 = from_state[2 * c:] + from_chunk[2 * c:] - _mm(p_b, u2)
        y_ref[0, 0, :, sl] = y2[:c] + y2[c:]
        e_end = e_inv * e_ref
        kb_end = jnp.concatenate([stack2(k * e_end), stack2(-(b * e_end))], axis=0)
        s_ref[p] = s0 * (e_ref * e_ref) + _mm_tn(jnp.concatenate([v2, u2], axis=0), kb_end)


def rwkv_scan(r, v, kk, k_dir, b_dir, lw_dir, n_ctx):
    bsz, length, width = r.shape
    c = RWKV_CHUNK
    nc, nc_ctx = length // c, n_ctx // c

    def chunk_of(d, j):
        back = jnp.where(j < nc_ctx, nc_ctx - 1 - j, nc + nc_ctx - 1 - j)
        return jnp.where(d == 0, j, back)

    shared = pl.BlockSpec((1, c, width), lambda b, d, j: (b, chunk_of(d, j), 0))
    per_dir = pl.BlockSpec((1, 1, c, width), lambda b, d, j: (b, d, chunk_of(d, j), 0))
    return pl.pallas_call(
        _rwkv_chunk_kernel,
        grid=(bsz, 2, nc),
        in_specs=[shared, shared, shared, per_dir, per_dir, per_dir],
        out_specs=per_dir,
        out_shape=jax.ShapeDtypeStruct((bsz, 2, length, width), F32),
        scratch_shapes=[pltpu.VMEM((width // LANES, LANES, LANES), F32)],
        compiler_params=pltpu.CompilerParams(dimension_semantics=("arbitrary", "arbitrary", "arbitrary")),
        name="rwkv_scan",
    )(r, v, kk, k_dir, b_dir, lw_dir)


CONV_CHANNELS = 512
CONV_TAPS = 31
S5_WIDTH = 512
S5_GROUP = 16
S5_GROUPS = S5_WIDTH // S5_GROUP
S5_STATE = 64
S5_MAX_RE = -1e-4
RET_HEADS = 4
RET_HEAD_DIM = 128
RET_WIDTH = RET_HEADS * RET_HEAD_DIM
RET_CHUNK = 128
ROPE_BASE = 10000.0
N_BRANCH = 4
PEER_HEADS = 8
PEER_KEYS = 128
PEER_TOPK = 16
PEER_QUERY = 256
PEER_HALF = PEER_QUERY // 2
PEER_BLOCK = 128
SHIFT_TAPS = 3
RWKV_IN = 3 * RWKV_WIDTH + 2 * RWKV_DECAY_RANK + 2 * RWKV_ICLR_RANK + RWKV_GATE_RANK
CONV_IN = 2 * CONV_CHANNELS
S5_IN = S5_WIDTH
RET_IN = 5 * RET_WIDTH
A_END = RWKV_IN
B_END = A_END + CONV_IN
C_END = B_END + S5_IN
IN_WIDTH = C_END + RET_IN
RWKV_SPLITS = (RWKV_WIDTH, 2 * RWKV_WIDTH, 3 * RWKV_WIDTH,
               3 * RWKV_WIDTH + RWKV_DECAY_RANK, 3 * RWKV_WIDTH + 2 * RWKV_DECAY_RANK,
               3 * RWKV_WIDTH + 2 * RWKV_DECAY_RANK + RWKV_ICLR_RANK,
               3 * RWKV_WIDTH + 2 * RWKV_DECAY_RANK + 2 * RWKV_ICLR_RANK)


def _rms_norm(z, g):
    zf = z.astype(F32)
    y = zf * lax.rsqrt(jnp.mean(zf * zf, axis=-1, keepdims=True) + EPS)
    return (y * g).astype(z.dtype)


def _normalise(z):
    zf = z.astype(F32)
    zc = zf - jnp.mean(zf, axis=-1, keepdims=True)
    return zc * lax.rsqrt(jnp.mean(zc * zc, axis=-1, keepdims=True) + EPS)


def _modulate(z, shift, scale):
    return z * (1.0 + scale) + shift


def _depthwise_conv(z, w):
    taps = w.shape[0]
    pad = (taps - 1) // 2
    return lax.conv_general_dilated(
        z, w[:, None, :].astype(z.dtype), window_strides=(1,), padding=[(pad, pad)],
        dimension_numbers=("NWC", "WIO", "NWC"), feature_group_count=z.shape[-1])


def _axial_rope(n_tokens):
    rows = n_tokens // GRID_W
    row = jnp.repeat(jnp.arange(rows, dtype=F32), GRID_W)
    col = jnp.tile(jnp.arange(GRID_W, dtype=F32), rows)
    n_freq = RET_HEAD_DIM // 4
    inv = ROPE_BASE ** (-jnp.arange(n_freq, dtype=F32) / n_freq)
    ang = jnp.concatenate([row[:, None] * inv, col[:, None] * inv], axis=-1)
    return jnp.cos(ang), jnp.sin(ang)


def _apply_rope(z, cos, sin):
    z1, z2 = jnp.split(z, 2, axis=-1)
    cs, sn = cos[None, :, None, :], sin[None, :, None, :]
    return jnp.concatenate([z1 * cs - z2 * sn, z1 * sn + z2 * cs], axis=-1)


def _rwkv_prep(p, shift_w, w0, w2, a0, a2, g2, k_k, k_a):
    p = _depthwise_conv(p, shift_w)
    r, k, v, wl_f, wl_b, al_f, al_b, gl = jnp.split(p, RWKV_SPLITS, axis=-1)
    b, t, _ = r.shape
    kk = (k * k_k).reshape(b, t, RWKV_HEADS, RWKV_HEAD_DIM)
    kk = (kk * lax.rsqrt(jnp.sum(kk * kk, axis=-1, keepdims=True) + EPS)).reshape(b, t, RWKV_WIDTH)
    g = jax.nn.sigmoid(gl) @ g2
    ks, bs, lws = [], [], []
    for d, (wl, al) in enumerate(((wl_f, al_f), (wl_b, al_b))):
        w_log = -jax.nn.softplus(-(w0[d] + jnp.tanh(wl) @ w2[d])) - 0.5
        a = jax.nn.sigmoid(a0[d] + al @ a2[d])
        ks.append(k * (1.0 + (a - 1.0) * k_a))
        bs.append(a * kk)
        lws.append(-jnp.exp(w_log))
    return r, v, kk, g, jnp.stack(ks, 1), jnp.stack(bs, 1), jnp.stack(lws, 1)


def _rwkv_mixer(p_lat, p_ctx, shift_w, w0, w2, a0, a2, g2, k_k, k_a, r_k, gn_g, gn_b):
    n_ctx = p_ctx.shape[1]
    prep_l = _rwkv_prep(p_lat, shift_w, w0, w2, a0, a2, g2, k_k, k_a)
    prep_c = _rwkv_prep(p_ctx, shift_w, w0, w2, a0, a2, g2, k_k, k_a)
    r, v, kk, g = (jnp.concatenate([c, l], axis=1) for c, l in zip(prep_c[:4], prep_l[:4]))
    k_dir, b_dir, lw_dir = (jnp.concatenate([c, l], axis=2) for c, l in zip(prep_c[4:], prep_l[4:]))
    y = rwkv_scan(r, v, kk, k_dir, b_dir, lw_dir, n_ctx)
    y = y[:, 0] + y[:, 1]
    b, t, _ = y.shape
    heads = lambda z: z.reshape(b, t, RWKV_HEADS, RWKV_HEAD_DIM)
    k_sum = k_dir[:, 0] + k_dir[:, 1]
    bonus = jnp.sum(heads(r) * heads(k_sum) * r_k, axis=-1, keepdims=True) * heads(v)
    o = _normalise(heads(y)).reshape(b, t, RWKV_WIDTH) * gn_g + gn_b + bonus.reshape(b, t, RWKV_WIDTH)
    o = o * g
    return o[:, n_ctx:], o[:, :n_ctx]


def _conformer_conv(p, dw, db, ln_g, ln_b):
    val, gate = jnp.split(p, 2, axis=-1)
    z = val * jax.nn.sigmoid(gate)
    z = _depthwise_conv(z, dw) + db
    z = _normalise(z) * ln_g + ln_b
    return jax.nn.silu(z).astype(p.dtype)


def _s5_discretise(lam_re, lam_im, log_dt, b_re, b_im):
    lam_re = jnp.minimum(lam_re.astype(F32), S5_MAX_RE)
    lam_im = lam_im.astype(F32)
    dt = jnp.exp(log_dt.astype(F32))[:, None]
    mag = jnp.exp(lam_re * dt)
    ang = lam_im * dt
    ab_re, ab_im = mag * jnp.cos(ang), mag * jnp.sin(ang)
    den = lam_re * lam_re + lam_im * lam_im
    nr, ni = ab_re - 1.0, ab_im
    f_re = (nr * lam_re + ni * lam_im) / den
    f_im = (ni * lam_re - nr * lam_im) / den
    b_re, b_im = b_re.astype(F32), b_im.astype(F32)
    bb_re = f_re[..., None] * b_re - f_im[..., None] * b_im
    bb_im = f_re[..., None] * b_im + f_im[..., None] * b_re
    return ab_re, ab_im, bb_re, bb_im


def _s5_combine(e1, e2):
    a1r, a1i, b1r, b1i = e1
    a2r, a2i, b2r, b2i = e2
    return (a1r * a2r - a1i * a2i, a1r * a2i + a1i * a2r,
            a2r * b1r - a2i * b1i + b2r, a2r * b1i + a2i * b1r + b2i)


def _s5_states(u, ab_re, ab_im, bb_re, bb_im, h0_re, h0_im, reverse):
    bu_re = jnp.einsum("gph,tbgh->tbgp", bb_re, u)
    bu_im = jnp.einsum("gph,tbgh->tbgp", bb_im, u)
    first = -1 if reverse else 0
    bu_re = bu_re.at[first].add(ab_re * h0_re - ab_im * h0_im)
    bu_im = bu_im.at[first].add(ab_re * h0_im + ab_im * h0_re)
    t = u.shape[0]
    a_re = jnp.broadcast_to(ab_re, (t, 1) + ab_re.shape)
    a_im = jnp.broadcast_to(ab_im, (t, 1) + ab_im.shape)
    _, _, x_re, x_im = lax.associative_scan(_s5_combine, (a_re, a_im, bu_re, bu_im), reverse=reverse, axis=0)
    return x_re, x_im


def _s5_readout(x_re, x_im, c_re, c_im):
    y = (jnp.einsum("ghp,tbgp->btgh", c_re.astype(F32), x_re)
         - jnp.einsum("ghp,tbgp->btgh", c_im.astype(F32), x_im))
    b, t = y.shape[:2]
    return y.reshape(b, t, S5_WIDTH)


def _s5_mixer(u_lat, u_ctx, lam_re, lam_im, log_dt, b_re, b_im, c_re, c_im, d_skip, glu_w, glu_b):
    def time_groups(u):
        b, t, _ = u.shape
        return jnp.moveaxis(u.astype(F32), 1, 0).reshape(t, b, S5_GROUPS, S5_GROUP)
    ul, uc = time_groups(u_lat), time_groups(u_ctx)
    h_zero = jnp.zeros((u_ctx.shape[0], S5_GROUPS, S5_STATE), F32)
    y_lat = d_skip * u_lat.astype(F32)
    y_ctx = d_skip * u_ctx.astype(F32)
    for d in range(2):
        rev = d == 1
        ab_re, ab_im, bb_re, bb_im = _s5_discretise(lam_re[d], lam_im[d], log_dt[d], b_re[d], b_im[d])
        xc_re, xc_im = _s5_states(uc, ab_re, ab_im, bb_re, bb_im, h_zero, h_zero, rev)
        last = 0 if rev else -1
        xl_re, xl_im = _s5_states(ul, ab_re, ab_im, bb_re, bb_im, xc_re[last], xc_im[last], rev)
        y_lat = y_lat + _s5_readout(xl_re, xl_im, c_re[d], c_im[d])
        y_ctx = y_ctx + _s5_readout(xc_re, xc_im, c_re[d], c_im[d])

    def glu(y):
        z = jax.nn.gelu(y)
        return z * jax.nn.sigmoid(z @ glu_w + glu_b)
    return glu(y_lat), glu(y_ctx)


def _retention_chunkwise(q, k, v, r0, log_gamma, strict, emit):
    b, t, h, _ = q.shape
    n_chunks = t // RET_CHUNK
    chunks = lambda z: z.reshape(b, n_chunks, RET_CHUNK, h, z.shape[-1]).transpose(1, 0, 3, 2, 4)
    pos = jnp.arange(RET_CHUNK, dtype=F32)
    diff = pos[:, None] - pos[None, :]
    keep = diff > 0 if strict else diff >= 0
    decay_in = jnp.where(keep, jnp.exp(jnp.where(keep, diff, 0.0) * log_gamma[:, None, None]), 0.0)
    decay_q = jnp.exp((pos + 1.0) * log_gamma[:, None])[None, :, :, None]
    decay_k = jnp.exp((RET_CHUNK - 1.0 - pos) * log_gamma[:, None])[None, :, :, None]
    decay_chunk = jnp.exp(RET_CHUNK * log_gamma)[None, :, None, None]

    def step(r, inp):
        qc, kc, vc = inp
        r_next = decay_chunk * r + jnp.einsum("bhck,bhcv->bhkv", kc * decay_k, vc)
        if not emit:
            return r_next, None
        scores = jnp.einsum("bhnk,bhmk->bhnm", qc, kc) * decay_in
        o = jnp.einsum("bhnm,bhmv->bhnv", scores, vc) + jnp.einsum("bhnk,bhkv->bhnv", qc * decay_q, r)
        return r_next, o
    r_final, o = lax.scan(step, r0, (chunks(q), chunks(k), chunks(v)))
    if emit:
        o = o.transpose(1, 0, 3, 2, 4).reshape(b, t, h, v.shape[-1])
    return o, r_final


def _retention_mixer(p_lat, p_ctx, gn_g, gn_b, rope_cos, rope_sin):
    log_gamma = jnp.log1p(-jnp.exp2(-5.0 - jnp.arange(RET_HEADS, dtype=F32)))
    k_scale = RET_HEAD_DIM ** -0.5

    def split_heads(p):
        b, t, _ = p.shape
        q, kf, kb, v, g = jnp.split(p.astype(F32), 5, axis=-1)
        hd = lambda z: z.reshape(b, t, RET_HEADS, RET_HEAD_DIM)
        return hd(q), hd(kf) * k_scale, hd(kb) * k_scale, hd(v), g
    ql, kfl, kbl, vl, gl = split_heads(p_lat)
    ql, kfl, kbl = (_apply_rope(z, rope_cos, rope_sin) for z in (ql, kfl, kbl))
    qc, kfc, kbc, vc, gc = split_heads(p_ctx)
    r_zero = jnp.zeros((p_ctx.shape[0], RET_HEADS, RET_HEAD_DIM, RET_HEAD_DIM), F32)
    flip = lambda z: jnp.flip(z, axis=1)
    oc_f, rc_f = _retention_chunkwise(qc, kfc, vc, r_zero, log_gamma, False, True)
    oc_b, rc_b = _retention_chunkwise(flip(qc), flip(kbc), flip(vc), r_zero, log_gamma, True, True)
    ol_f, _ = _retention_chunkwise(ql, kfl, vl, rc_f, log_gamma, False, True)
    ol_b, _ = _retention_chunkwise(flip(ql), flip(kbl), flip(vl), rc_b, log_gamma, True, True)

    def readout(o, g):
        b, t = o.shape[:2]
        return (_normalise(o).reshape(b, t, RET_WIDTH) * gn_g + gn_b) * jax.nn.silu(g)
    return readout(ol_f + flip(ol_b), gl), readout(oc_f + flip(oc_b), gc)


def _merge_branches(n, ys, w_branch, w_merge, b_merge, w_out):
    m = 0.0
    for i, y in enumerate(ys):
        gate = jax.nn.sigmoid(n @ w_merge[i] + b_merge[i])
        m = m + gate * (y.astype(n.dtype) @ w_branch[i])
    return m @ w_out


def _peer_ffn(h, w_q, sub_keys, exp_u, exp_v):
    b, t, d = h.shape
    blocks = h.reshape(-1, PEER_BLOCK, d)

    def one_block(z):
        n = z.shape[0]
        q = (z @ w_q).reshape(n, PEER_HEADS, 2, PEER_HALF)
        s = jnp.einsum("nhpk,hpek->nhpe", q, sub_keys).astype(F32)
        s_top, i_top = lax.top_k(s, PEER_TOPK)
        cand = (s_top[:, :, 0, :, None] + s_top[:, :, 1, None, :]).reshape(n, PEER_HEADS, PEER_TOPK * PEER_TOPK)
        cand_id = (i_top[:, :, 0, :, None] * PEER_KEYS + i_top[:, :, 1, None, :]).reshape(
            n, PEER_HEADS, PEER_TOPK * PEER_TOPK)
        best, pos = lax.top_k(cand, PEER_TOPK)
        ids = jnp.take_along_axis(cand_id, pos, axis=-1)
        gate = jax.nn.softmax(best, axis=-1)
        act = jax.nn.gelu(jnp.einsum("nhkd,nd->nhk", exp_u[ids], z))
        return jnp.einsum("nhk,nhkd->nd", (gate * act).astype(z.dtype), exp_v[ids])
    return lax.map(one_block, blocks).reshape(b, t, d)


def kernel(x, c, ctx, c_ctx, ada_w, ada_b, norm1_g, norm2_g, w_in, rwkv_shift, rwkv_w0, rwkv_w2,
           rwkv_a0, rwkv_a2, rwkv_g2, rwkv_kk, rwkv_ka, rwkv_rk, rwkv_gn_g, rwkv_gn_b,
           conv_dw, conv_db, conv_ln_g, conv_ln_b, s5_lam_re, s5_lam_im, s5_log_dt,
           s5_b_re, s5_b_im, s5_c_re, s5_c_im, s5_d, s5_glu_w, s5_glu_b, ret_gn_g, ret_gn_b,
           w_branch, w_merge, b_merge, w_out, peer_wq, peer_keys, peer_u, peer_v, final_g):
    depth = ada_w.shape[0]
    rope_cos, rope_sin = _axial_rope(x.shape[1])
    for l in range(depth):
        mod = jax.nn.silu(c) @ ada_w[l] + ada_b[l]
        sh1, sc1, gt1, sh2, sc2, gt2 = jnp.split(mod[:, None, :], 6, axis=-1)
        cmod = jax.nn.silu(c_ctx) @ ada_w[l] + ada_b[l]
        csh1, csc1, cgt1, csh2, csc2, cgt2 = jnp.split(cmod, 6, axis=-1)
        n_lat = _modulate(_rms_norm(x, norm1_g[l]), sh1, sc1)
        n_ctx = _modulate(_rms_norm(ctx, norm1_g[l]), csh1, csc1)
        proj = lambda z, lo, hi: z @ w_in[l][:, lo:hi]
        ya_l, ya_c = _rwkv_mixer(proj(n_lat, 0, A_END), proj(n_ctx, 0, A_END), rwkv_shift[l], rwkv_w0[l],
                                 rwkv_w2[l], rwkv_a0[l], rwkv_a2[l], rwkv_g2[l], rwkv_kk[l], rwkv_ka[l],
                                 rwkv_rk[l], rwkv_gn_g[l], rwkv_gn_b[l])
        yb_l = _conformer_conv(proj(n_lat, A_END, B_END), conv_dw[l], conv_db[l], conv_ln_g[l], conv_ln_b[l])
        yb_c = _conformer_conv(proj(n_ctx, A_END, B_END), conv_dw[l], conv_db[l], conv_ln_g[l], conv_ln_b[l])
        yc_l, yc_c = _s5_mixer(proj(n_lat, B_END, C_END), proj(n_ctx, B_END, C_END), s5_lam_re[l], s5_lam_im[l],
                               s5_log_dt[l], s5_b_re[l], s5_b_im[l], s5_c_re[l], s5_c_im[l], s5_d[l],
                               s5_glu_w[l], s5_glu_b[l])
        yd_l, yd_c = _retention_mixer(proj(n_lat, C_END, IN_WIDTH), proj(n_ctx, C_END, IN_WIDTH),
                                      ret_gn_g[l], ret_gn_b[l], rope_cos, rope_sin)
        m_lat = _merge_branches(n_lat, (ya_l, yb_l, yc_l, yd_l), w_branch[l], w_merge[l], b_merge[l], w_out[l])
        m_ctx = _merge_branches(n_ctx, (ya_c, yb_c, yc_c, yd_c), w_branch[l], w_merge[l], b_merge[l], w_out[l])
        x = x + gt1 * m_lat
        x = x + gt2 * _peer_ffn(_modulate(_rms_norm(x, norm2_g[l]), sh2, sc2),
                                peer_wq[l], peer_keys[l], peer_u[l], peer_v[l])
        ctx = ctx + cgt1 * m_ctx
        ctx = ctx + cgt2 * _peer_ffn(_modulate(_rms_norm(ctx, norm2_g[l]), csh2, csc2),
                                     peer_wq[l], peer_keys[l], peer_u[l], peer_v[l])
    return _rms_norm(x, final_g)
```

```python
import functools
import math

import jax
import jax.numpy as jnp
from jax import lax
from jax.experimental import pallas as pl
from jax.experimental.pallas import tpu as pltpu
from jax.experimental.pallas import tpu_sc as plsc

F32 = jnp.float32
BF16 = jnp.bfloat16

D_MODEL = 1024
GRID_W = 64
EPS = 1e-6

RWKV_HEADS = 8
RWKV_HEAD_DIM = 64
RWKV_WIDTH = RWKV_HEADS * RWKV_HEAD_DIM
RWKV_DECAY_RANK = 64
RWKV_ICLR_RANK = 64
RWKV_GATE_RANK = 128
RWKV_CHUNK = 64
LANES = 128


def _bdot(a, b, dims):
    return lax.dot_general(a.astype(BF16), b.astype(BF16), (dims, ((), ())), preferred_element_type=F32)


def _mm(a, b):
    return _bdot(a, b, ((1,), (0,)))


def _mm_nt(a, b):
    return _bdot(a, b, ((1,), (1,)))


def _mm_tn(a, b):
    return _bdot(a, b, ((0,), (0,)))


def _rwkv_chunk_kernel(r_ref, v_ref, kk_ref, k_ref, b_ref, lw_ref, y_ref, s_ref):
    c = RWKV_CHUNK
    d = pl.program_id(1)
    j = pl.program_id(2)

    @pl.when(j == 0)
    def _():
        s_ref[...] = jnp.zeros_like(s_ref)

    sign = jnp.where(d == 0, 1, -1)
    row = lax.broadcasted_iota(jnp.int32, (c, c), 0)
    col = lax.broadcasted_iota(jnp.int32, (c, c), 1)
    cum_mat = jnp.where((row - col) * sign >= 0, 1.0, 0.0).astype(BF16)

    lw = lw_ref[0, 0]
    lw_hi = lw.astype(BF16)
    rem = lw - lw_hi.astype(F32)
    lw_mid = rem.astype(BF16)
    lw_lo = (rem - lw_mid.astype(F32)).astype(BF16)
    lc_incl = _mm(cum_mat, lw_hi) + _mm(cum_mat, lw_mid) + _mm(cum_mat, lw_lo)
    lc_excl = lc_incl - lw
    lc_ref = 0.5 * jnp.sum(lw, axis=0, keepdims=True)

    row2 = lax.broadcasted_iota(jnp.int32, (2 * c, 2 * c), 0)
    col2 = lax.broadcasted_iota(jnp.int32, (2 * c, 2 * c), 1)
    same_head = (row2 // c) == (col2 // c)
    tdiff = jnp.where(same_head, ((row2 % c) - (col2 % c)) * sign, -1)
    strict2 = tdiff > 0
    incl2 = tdiff >= 0
    eye2 = jnp.where(row2 == col2, 1.0, 0.0)
    lane_head = lax.broadcasted_iota(jnp.int32, (c, LANES), 1) // RWKV_HEAD_DIM

    def stack2(z):
        return jnp.concatenate([jnp.where(lane_head == 0, z, 0.0), jnp.where(lane_head == 1, z, 0.0)], axis=0)

    for p in range(RWKV_WIDTH // LANES):
        sl = slice(p * LANES, (p + 1) * LANES)
        r, v, kk = r_ref[0, :, sl], v_ref[0, :, sl], kk_ref[0, :, sl]
        k, b = k_ref[0, 0, :, sl], b_ref[0, 0, :, sl]
        ref = lc_ref[:, sl]
        e_in = jnp.exp(lc_incl[:, sl] - ref)
        e_ex = jnp.exp(lc_excl[:, sl] - ref)
        e_inv = jnp.exp(ref - lc_incl[:, sl])
        e_ref = jnp.exp(ref)
        q_stack = jnp.concatenate([stack2(kk * e_ex), stack2(r * e_in)], axis=0)
        k_stack = jnp.concatenate([stack2(k * e_inv), stack2(b * e_inv)], axis=0)
        x = _mm_nt(q_stack, k_stack)
        m_k = jnp.where(strict2, x[:2 * c, :2 * c], 0.0)
        m_b = jnp.where(strict2, x[:2 * c, 2 * c:], 0.0)
        p_k = jnp.where(incl2, x[2 * c:, :2 * c], 0.0)
        p_b = jnp.where(incl2, x[2 * c:, 2 * c:], 0.0)
        t_inv = eye2 - m_b
        m_pow = m_b
        for _ in range(int(math.log2(c)) - 1):
            m_pow = _mm(m_pow, m_pow)
            t_inv = t_inv + _mm(t_inv, m_pow)
        s0 = s_ref[p]
        v2 = stack2(v)
        q0_stack = jnp.concatenate([stack2(kk * (e_ex * e_ref)), stack2(r * (e_in * e_ref))], axis=0)
        from_state = _mm_nt(q0_stack, s0)
        from_chunk = _mm(jnp.concatenate([m_k, p_k], axis=0), v2)
        u2 = _mm(t_inv, from_state[:2 * c] + from_chunk[:2 * c])
        y2 = from_state[2 * c:] + from_chunk[2 * c:] - _mm(p_b, u2)
        y_ref[0, 0, :, sl] = y2[:c] + y2[c:]
        e_end = e_inv * e_ref
        kb_end = jnp.concatenate([stack2(k * e_end), stack2(-(b * e_end))], axis=0)
        s_ref[p] = s0 * (e_ref * e_ref) + _mm_tn(jnp.concatenate([v2, u2], axis=0), kb_end)


def rwkv_scan(r, v, kk, k_dir, b_dir, lw_dir, n_ctx):
    bsz, length, width = r.shape
    c = RWKV_CHUNK
    nc, nc_ctx = length // c, n_ctx // c

    def chunk_of(d, j):
        back = jnp.where(j < nc_ctx, nc_ctx - 1 - j, nc + nc_ctx - 1 - j)
        return jnp.where(d == 0, j, back)

    shared = pl.BlockSpec((1, c, width), lambda b, d, j: (b, chunk_of(d, j), 0))
    per_dir = pl.BlockSpec((1, 1, c, width), lambda b, d, j: (b, d, chunk_of(d, j), 0))
    return pl.pallas_call(
        _rwkv_chunk_kernel,
        grid=(bsz, 2, nc),
        in_specs=[shared, shared, shared, per_dir, per_dir, per_dir],
        out_specs=per_dir,
        out_shape=jax.ShapeDtypeStruct((bsz, 2, length, width), F32),
        scratch_shapes=[pltpu.VMEM((width // LANES, LANES, LANES), F32)],
        compiler_params=pltpu.CompilerParams(dimension_semantics=("arbitrary", "arbitrary", "arbitrary")),
        name="rwkv_scan",
    )(r, v, kk, k_dir, b_dir, lw_dir)


S5_CHUNK = 128


def _s5_chunk_kernel(u_ref, bre_ref, bim_ref, c_ref, are_ref, aim_ref, y_ref, carry_ref, *, reverse):
    tc = S5_CHUNK
    j = pl.program_id(1)

    @pl.when(j == 0)
    def _():
        carry_ref[...] = jnp.zeros_like(carry_ref)

    u = u_ref[0]
    xr = _mm(u, bre_ref[...])
    xi = _mm(u, bim_ref[...])
    ar, ai = are_ref[...], aim_ref[...]
    cr, ci = carry_ref[0:1, :], carry_ref[1:2, :]
    row = lax.broadcasted_iota(jnp.int32, xr.shape, 0)
    first = tc - 1 if reverse else 0
    xr = xr + jnp.where(row == first, ar * cr - ai * ci, 0.0)
    xi = xi + jnp.where(row == first, ar * ci + ai * cr, 0.0)
    for level in range(int(math.log2(tc))):
        sh = 1 << level
        if reverse:
            sr, si = pltpu.roll(xr, tc - sh, 0), pltpu.roll(xi, tc - sh, 0)
            keep = row < tc - sh
        else:
            sr, si = pltpu.roll(xr, sh, 0), pltpu.roll(xi, sh, 0)
            keep = row >= sh
        sr, si = jnp.where(keep, sr, 0.0), jnp.where(keep, si, 0.0)
        xr, xi = xr + (ar * sr - ai * si), xi + (ar * si + ai * sr)
        ar, ai = ar * ar - ai * ai, 2.0 * (ar * ai)
    last = 0 if reverse else tc - 1
    carry_ref[0:1, :] = xr[last:last + 1, :]
    carry_ref[1:2, :] = xi[last:last + 1, :]
    y_ref[0] = _mm(jnp.concatenate([xr, xi], axis=1), c_ref[...])


def s5_scan(u, b_re, b_im, c_cat, a_re, a_im, n_ctx, reverse):
    bsz, length, width = u.shape
    n_state = b_re.shape[1]
    tc = S5_CHUNK
    nc, nc_ctx = length // tc, n_ctx // tc

    def chunk_of(j):
        if not reverse:
            return j
        return jnp.where(j < nc_ctx, nc_ctx - 1 - j, nc + nc_ctx - 1 - j)

    tok = pl.BlockSpec((1, tc, width), lambda b, j: (b, chunk_of(j), 0))
    full = lambda shape: pl.BlockSpec(shape, lambda b, j: (0,) * len(shape))
    return pl.pallas_call(
        functools.partial(_s5_chunk_kernel, reverse=reverse),
        grid=(bsz, nc),
        in_specs=[tok, full(b_re.shape), full(b_im.shape), full(c_cat.shape), full(a_re.shape), full(a_im.shape)],
        out_specs=tok,
        out_shape=jax.ShapeDtypeStruct((bsz, length, width), F32),
        scratch_shapes=[pltpu.VMEM((8, n_state), F32)],
        compiler_params=pltpu.CompilerParams(dimension_semantics=("arbitrary", "arbitrary"),
                                             vmem_limit_bytes=48 * 1024 * 1024),
        name="s5_scan_rev" if reverse else "s5_scan_fwd",
    )(u, b_re, b_im, c_cat, a_re, a_im)


SC_CORES = 2
SC_SUBCORES = 16
SC_WORKERS = SC_CORES * SC_SUBCORES
GATHER_ROWS = 64
PEER_SLOTS = 128
PEER_TOKENS_PER_STEP = 16


def pack_bf16_pairs(table):
    half = table.shape[1] // 2
    bits = lax.bitcast_convert_type(table.astype(BF16), jnp.uint16).astype(jnp.uint32)
    return bits[:, :half] | (bits[:, half:] << 16)


def sc_gather_rows(table, idx):
    n_rows, width = idx.shape[0], table.shape[1]
    per_worker = n_rows // SC_WORKERS
    n_chunks = per_worker // GATHER_ROWS
    assert per_worker * SC_WORKERS == n_rows and n_chunks * GATHER_ROWS == per_worker
    mesh = plsc.VectorSubcoreMesh(core_axis_name="c", subcore_axis_name="s",
                                  num_cores=SC_CORES, num_subcores=SC_SUBCORES)

    @functools.partial(
        pl.kernel, mesh=mesh,
        out_type=jax.ShapeDtypeStruct((n_rows, width), table.dtype),
        scratch_types=[pltpu.VMEM((GATHER_ROWS,), jnp.int32),
                       pltpu.VMEM((GATHER_ROWS, width), table.dtype),
                       pltpu.SemaphoreType.DMA],
        name="peer_sc_gather",
    )
    def gather(table_hbm, idx_hbm, out_hbm, idx_v, rows_v, sem):
        worker = lax.axis_index("s") * SC_CORES + lax.axis_index("c")
        base = worker * per_worker

        @pl.loop(0, n_chunks)
        def _(i):
            off = pl.multiple_of(base + i * GATHER_ROWS, GATHER_ROWS)
            pltpu.sync_copy(idx_hbm.at[pl.ds(off, GATHER_ROWS)], idx_v)
            pltpu.async_copy(table_hbm.at[idx_v], rows_v, sem).wait()
            pltpu.sync_copy(rows_v, out_hbm.at[pl.ds(off, GATHER_ROWS)])

    return gather(table, idx)


def _unpack_pairs(words):
    lo = pltpu.bitcast(words << 16, F32)
    hi = pltpu.bitcast(words & jnp.uint32(0xFFFF0000), F32)
    return lo, hi


def _gelu_tanh(x):
    return 0.5 * x * (1.0 + jnp.tanh(0.7978845608028654 * (x + 0.044715 * (x * x * x))))


def _peer_expert_kernel(z_ref, gate_ref, ug_ref, vg_ref, o_ref):
    half = ug_ref.shape[2]
    gate_t = gate_ref[...].T
    for n in range(PEER_TOKENS_PER_STEP):
        z_lo, z_hi = z_ref[n:n + 1, :half], z_ref[n:n + 1, half:]
        u_lo, u_hi = _unpack_pairs(ug_ref[n])
        prod = u_lo * z_lo + u_hi * z_hi
        act = jnp.sum(prod, axis=1, keepdims=True)
        w = _gelu_tanh(act) * gate_t[:, n:n + 1]
        v_lo, v_hi = _unpack_pairs(vg_ref[n])
        o_ref[n:n + 1, :half] = jnp.sum(w * v_lo, axis=0, keepdims=True)
        o_ref[n:n + 1, half:] = jnp.sum(w * v_hi, axis=0, keepdims=True)


def peer_experts(z, gate, ug, vg):
    n, d = z.shape
    tn = PEER_TOKENS_PER_STEP
    slots, half = ug.shape[1], ug.shape[2]
    return pl.pallas_call(
        _peer_expert_kernel,
        grid=(n // tn,),
        in_specs=[pl.BlockSpec((tn, d), lambda i: (i, 0)),
                  pl.BlockSpec((tn, slots), lambda i: (i, 0)),
                  pl.BlockSpec((tn, slots, half), lambda i: (i, 0, 0)),
                  pl.BlockSpec((tn, slots, half), lambda i: (i, 0, 0))],
        out_specs=pl.BlockSpec((tn, d), lambda i: (i, 0)),
        out_shape=jax.ShapeDtypeStruct((n, d), F32),
        compiler_params=pltpu.CompilerParams(dimension_semantics=("arbitrary",),
                                             vmem_limit_bytes=48 * 1024 * 1024),
        name="peer_experts",
    )(z, gate, ug, vg)


CONV_CHANNELS = 512
CONV_TAPS = 31
S5_WIDTH = 512
S5_GROUP = 16
S5_GROUPS = S5_WIDTH // S5_GROUP
S5_STATE = 64
S5_MAX_RE = -1e-4
RET_HEADS = 4
RET_HEAD_DIM = 128
RET_WIDTH = RET_HEADS * RET_HEAD_DIM
RET_CHUNK = 128
ROPE_BASE = 10000.0
N_BRANCH = 4
PEER_HEADS = 8
PEER_KEYS = 128
PEER_TOPK = 16
PEER_QUERY = 256
PEER_HALF = PEER_QUERY // 2
PEER_BLOCK = 128
SHIFT_TAPS = 3
RWKV_IN = 3 * RWKV_WIDTH + 2 * RWKV_DECAY_RANK + 2 * RWKV_ICLR_RANK + RWKV_GATE_RANK
CONV_IN = 2 * CONV_CHANNELS
S5_IN = S5_WIDTH
RET_IN = 5 * RET_WIDTH
A_END = RWKV_IN
B_END = A_END + CONV_IN
C_END = B_END + S5_IN
IN_WIDTH = C_END + RET_IN
RWKV_SPLITS = (RWKV_WIDTH, 2 * RWKV_WIDTH, 3 * RWKV_WIDTH,
               3 * RWKV_WIDTH + RWKV_DECAY_RANK, 3 * RWKV_WIDTH + 2 * RWKV_DECAY_RANK,
               3 * RWKV_WIDTH + 2 * RWKV_DECAY_RANK + RWKV_ICLR_RANK,
               3 * RWKV_WIDTH + 2 * RWKV_DECAY_RANK + 2 * RWKV_ICLR_RANK)


def _rms_norm(z, g):
    zf = z.astype(F32)
    y = zf * lax.rsqrt(jnp.mean(zf * zf, axis=-1, keepdims=True) + EPS)
    return (y * g).astype(z.dtype)


def _normalise(z):
    zf = z.astype(F32)
    zc = zf - jnp.mean(zf, axis=-1, keepdims=True)
    return zc * lax.rsqrt(jnp.mean(zc * zc, axis=-1, keepdims=True) + EPS)


def _modulate(z, shift, scale):
    return z * (1.0 + scale) + shift


def _depthwise_conv(z, w):
    taps = w.shape[0]
    pad = (taps - 1) // 2
    return lax.conv_general_dilated(
        z, w[:, None, :].astype(z.dtype), window_strides=(1,), padding=[(pad, pad)],
        dimension_numbers=("NWC", "WIO", "NWC"), feature_group_count=z.shape[-1])


def _axial_rope(n_tokens):
    rows = n_tokens // GRID_W
    row = jnp.repeat(jnp.arange(rows, dtype=F32), GRID_W)
    col = jnp.tile(jnp.arange(GRID_W, dtype=F32), rows)
    n_freq = RET_HEAD_DIM // 4
    inv = ROPE_BASE ** (-jnp.arange(n_freq, dtype=F32) / n_freq)
    ang = jnp.concatenate([row[:, None] * inv, col[:, None] * inv], axis=-1)
    return jnp.cos(ang), jnp.sin(ang)


def _apply_rope(z, cos, sin):
    z1, z2 = jnp.split(z, 2, axis=-1)
    cs, sn = cos[None, :, None, :], sin[None, :, None, :]
    return jnp.concatenate([z1 * cs - z2 * sn, z1 * sn + z2 * cs], axis=-1)


def _rwkv_prep(p, shift_w, w0, w2, a0, a2, g2, k_k, k_a):
    p = _depthwise_conv(p, shift_w)
    r, k, v, wl_f, wl_b, al_f, al_b, gl = jnp.split(p, RWKV_SPLITS, axis=-1)
    b, t, _ = r.shape
    kk = (k * k_k).reshape(b, t, RWKV_HEADS, RWKV_HEAD_DIM)
    kk = (kk * lax.rsqrt(jnp.sum(kk * kk, axis=-1, keepdims=True) + EPS)).reshape(b, t, RWKV_WIDTH)
    g = jax.nn.sigmoid(gl) @ g2
    ks, bs, lws = [], [], []
    for d, (wl, al) in enumerate(((wl_f, al_f), (wl_b, al_b))):
        w_log = -jax.nn.softplus(-(w0[d] + jnp.tanh(wl) @ w2[d])) - 0.5
        a = jax.nn.sigmoid(a0[d] + al @ a2[d])
        ks.append(k * (1.0 + (a - 1.0) * k_a))
        bs.append(a * kk)
        lws.append(-jnp.exp(w_log))
    return r, v, kk, g, jnp.stack(ks, 1), jnp.stack(bs, 1), jnp.stack(lws, 1)


def _rwkv_mixer(p_lat, p_ctx, shift_w, w0, w2, a0, a2, g2, k_k, k_a, r_k, gn_g, gn_b):
    n_ctx = p_ctx.shape[1]
    prep_l = _rwkv_prep(p_lat, shift_w, w0, w2, a0, a2, g2, k_k, k_a)
    prep_c = _rwkv_prep(p_ctx, shift_w, w0, w2, a0, a2, g2, k_k, k_a)
    r, v, kk, g = (jnp.concatenate([c, l], axis=1) for c, l in zip(prep_c[:4], prep_l[:4]))
    k_dir, b_dir, lw_dir = (jnp.concatenate([c, l], axis=2) for c, l in zip(prep_c[4:], prep_l[4:]))
    y = rwkv_scan(r, v, kk, k_dir, b_dir, lw_dir, n_ctx)
    y = y[:, 0] + y[:, 1]
    b, t, _ = y.shape
    heads = lambda z: z.reshape(b, t, RWKV_HEADS, RWKV_HEAD_DIM)
    k_sum = k_dir[:, 0] + k_dir[:, 1]
    bonus = jnp.sum(heads(r) * heads(k_sum) * r_k, axis=-1, keepdims=True) * heads(v)
    o = _normalise(heads(y)).reshape(b, t, RWKV_WIDTH) * gn_g + gn_b + bonus.reshape(b, t, RWKV_WIDTH)
    o = o * g
    return o[:, n_ctx:], o[:, :n_ctx]


def _conformer_conv(p, dw, db, ln_g, ln_b):
    val, gate = jnp.split(p, 2, axis=-1)
    z = val * jax.nn.sigmoid(gate)
    z = _depthwise_conv(z, dw) + db
    z = _normalise(z) * ln_g + ln_b
    return jax.nn.silu(z).astype(p.dtype)


def _s5_discretise(lam_re, lam_im, log_dt, b_re, b_im):
    lam_re = jnp.minimum(lam_re.astype(F32), S5_MAX_RE)
    lam_im = lam_im.astype(F32)
    dt = jnp.exp(log_dt.astype(F32))[:, None]
    mag = jnp.exp(lam_re * dt)
    ang = lam_im * dt
    ab_re, ab_im = mag * jnp.cos(ang), mag * jnp.sin(ang)
    den = lam_re * lam_re + lam_im * lam_im
    nr, ni = ab_re - 1.0, ab_im
    f_re = (nr * lam_re + ni * lam_im) / den
    f_im = (ni * lam_re - nr * lam_im) / den
    b_re, b_im = b_re.astype(F32), b_im.astype(F32)
    bb_re = f_re[..., None] * b_re - f_im[..., None] * b_im
    bb_im = f_re[..., None] * b_im + f_im[..., None] * b_re
    return ab_re, ab_im, bb_re, bb_im


def _s5_mixer(u_lat, u_ctx, lam_re, lam_im, log_dt, b_re, b_im, c_re, c_im, d_skip, glu_w, glu_b):
    n_ctx = u_ctx.shape[1]
    u = jnp.concatenate([u_ctx, u_lat], axis=1).astype(F32)
    eye = jnp.eye(S5_GROUPS, dtype=F32)
    n_state = S5_GROUPS * S5_STATE
    y = d_skip * u
    for d in range(2):
        ab_re, ab_im, bb_re, bb_im = _s5_discretise(lam_re[d], lam_im[d], log_dt[d], b_re[d], b_im[d])
        blk_in = lambda bb: jnp.einsum("gph,gk->ghkp", bb, eye).reshape(S5_WIDTH, n_state).astype(BF16)
        blk_out = lambda cc: jnp.einsum("ghp,gk->kpgh", cc.astype(F32), eye).reshape(n_state, S5_WIDTH)
        c_cat = jnp.concatenate([blk_out(c_re[d]), -blk_out(c_im[d])], axis=0).astype(BF16)
        y = y + s5_scan(u, blk_in(bb_re), blk_in(bb_im), c_cat, ab_re.reshape(1, n_state),
                        ab_im.reshape(1, n_state), n_ctx, reverse=(d == 1))
    z = jax.nn.gelu(y)
    out = z * jax.nn.sigmoid(z @ glu_w + glu_b)
    return out[:, n_ctx:], out[:, :n_ctx]


def _retention_chunkwise(q, k, v, r0, log_gamma, strict, emit):
    b, t, h, _ = q.shape
    n_chunks = t // RET_CHUNK
    chunks = lambda z: z.reshape(b, n_chunks, RET_CHUNK, h, z.shape[-1]).transpose(1, 0, 3, 2, 4)
    pos = jnp.arange(RET_CHUNK, dtype=F32)
    diff = pos[:, None] - pos[None, :]
    keep = diff > 0 if strict else diff >= 0
    decay_in = jnp.where(keep, jnp.exp(jnp.where(keep, diff, 0.0) * log_gamma[:, None, None]), 0.0)
    decay_q = jnp.exp((pos + 1.0) * log_gamma[:, None])[None, :, :, None]
    decay_k = jnp.exp((RET_CHUNK - 1.0 - pos) * log_gamma[:, None])[None, :, :, None]
    decay_chunk = jnp.exp(RET_CHUNK * log_gamma)[None, :, None, None]

    def step(r, inp):
        qc, kc, vc = inp
        r_next = decay_chunk * r + jnp.einsum("bhck,bhcv->bhkv", kc * decay_k, vc)
        if not emit:
            return r_next, None
        scores = jnp.einsum("bhnk,bhmk->bhnm", qc, kc) * decay_in
        o = jnp.einsum("bhnm,bhmv->bhnv", scores, vc) + jnp.einsum("bhnk,bhkv->bhnv", qc * decay_q, r)
        return r_next, o
    r_final, o = lax.scan(step, r0, (chunks(q), chunks(k), chunks(v)))
    if emit:
        o = o.transpose(1, 0, 3, 2, 4).reshape(b, t, h, v.shape[-1])
    return o, r_final


def _retention_mixer(p_lat, p_ctx, gn_g, gn_b, rope_cos, rope_sin):
    log_gamma = jnp.log1p(-jnp.exp2(-5.0 - jnp.arange(RET_HEADS, dtype=F32)))
    k_scale = RET_HEAD_DIM ** -0.5

    def split_heads(p):
        b, t, _ = p.shape
        q, kf, kb, v, g = jnp.split(p.astype(F32), 5, axis=-1)
        hd = lambda z: z.reshape(b, t, RET_HEADS, RET_HEAD_DIM)
        return hd(q), hd(kf) * k_scale, hd(kb) * k_scale, hd(v), g
    ql, kfl, kbl, vl, gl = split_heads(p_lat)
    ql, kfl, kbl = (_apply_rope(z, rope_cos, rope_sin) for z in (ql, kfl, kbl))
    qc, kfc, kbc, vc, gc = split_heads(p_ctx)
    r_zero = jnp.zeros((p_ctx.shape[0], RET_HEADS, RET_HEAD_DIM, RET_HEAD_DIM), F32)
    flip = lambda z: jnp.flip(z, axis=1)
    oc_f, rc_f = _retention_chunkwise(qc, kfc, vc, r_zero, log_gamma, False, True)
    oc_b, rc_b = _retention_chunkwise(flip(qc), flip(kbc), flip(vc), r_zero, log_gamma, True, True)
    ol_f, _ = _retention_chunkwise(ql, kfl, vl, rc_f, log_gamma, False, True)
    ol_b, _ = _retention_chunkwise(flip(ql), flip(kbl), flip(vl), rc_b, log_gamma, True, True)

    def readout(o, g):
        b, t = o.shape[:2]
        return (_normalise(o).reshape(b, t, RET_WIDTH) * gn_g + gn_b) * jax.nn.silu(g)
    return readout(ol_f + flip(ol_b), gl), readout(oc_f + flip(oc_b), gc)


def _merge_branches(n, ys, w_branch, w_merge, b_merge, w_out):
    m = 0.0
    for i, y in enumerate(ys):
        gate = jax.nn.sigmoid(n @ w_merge[i] + b_merge[i])
        m = m + gate * (y.astype(n.dtype) @ w_branch[i])
    return m @ w_out


def _peer_ffn(h, w_q, sub_keys, u_packed, v_packed):
    n, d = h.shape
    q = (h @ w_q).reshape(n, PEER_HEADS, 2, PEER_HALF)
    s = jnp.einsum("nhpk,hpek->nhpe", q, sub_keys).astype(F32)
    s_top, i_top = lax.top_k(s, PEER_TOPK)
    cand = (s_top[:, :, 0, :, None] + s_top[:, :, 1, None, :]).reshape(n, PEER_HEADS, PEER_TOPK * PEER_TOPK)
    cand_id = (i_top[:, :, 0, :, None] * PEER_KEYS + i_top[:, :, 1, None, :]).reshape(
        n, PEER_HEADS, PEER_TOPK * PEER_TOPK)
    best, pos = lax.top_k(cand, PEER_TOPK)
    ids = jnp.take_along_axis(cand_id, pos, axis=-1).reshape(n, PEER_SLOTS).astype(jnp.int32)
    gate = jax.nn.softmax(best, axis=-1).reshape(n, PEER_SLOTS)
    step = SC_WORKERS * GATHER_ROWS // PEER_SLOTS
    n_blk = next(k for k in (8, 4, 2, 1) if n % (k * step) == 0)
    nb = n // n_blk

    def one_block(args):
        z_b, ids_b, gate_b = args
        ug = sc_gather_rows(u_packed, ids_b.reshape(-1)).reshape(nb, PEER_SLOTS, d // 2)
        vg = sc_gather_rows(v_packed, ids_b.reshape(-1)).reshape(nb, PEER_SLOTS, d // 2)
        return peer_experts(z_b, gate_b, ug, vg)
    out = lax.map(one_block, (h.reshape(n_blk, nb, d), ids.reshape(n_blk, nb, PEER_SLOTS),
                              gate.reshape(n_blk, nb, PEER_SLOTS)))
    return out.reshape(n, d)


def kernel(x, c, ctx, c_ctx, ada_w, ada_b, norm1_g, norm2_g, w_in, rwkv_shift, rwkv_w0, rwkv_w2,
           rwkv_a0, rwkv_a2, rwkv_g2, rwkv_kk, rwkv_ka, rwkv_rk, rwkv_gn_g, rwkv_gn_b,
           conv_dw, conv_db, conv_ln_g, conv_ln_b, s5_lam_re, s5_lam_im, s5_log_dt,
           s5_b_re, s5_b_im, s5_c_re, s5_c_im, s5_d, s5_glu_w, s5_glu_b, ret_gn_g, ret_gn_b,
           w_branch, w_merge, b_merge, w_out, peer_wq, peer_keys, peer_u, peer_v, final_g):
    depth = ada_w.shape[0]
    rope_cos, rope_sin = _axial_rope(x.shape[1])
    for l in range(depth):
        mod = jax.nn.silu(c) @ ada_w[l] + ada_b[l]
        sh1, sc1, gt1, sh2, sc2, gt2 = jnp.split(mod[:, None, :], 6, axis=-1)
        cmod = jax.nn.silu(c_ctx) @ ada_w[l] + ada_b[l]
        csh1, csc1, cgt1, csh2, csc2, cgt2 = jnp.split(cmod, 6, axis=-1)
        n_lat = _modulate(_rms_norm(x, norm1_g[l]), sh1, sc1)
        n_ctx = _modulate(_rms_norm(ctx, norm1_g[l]), csh1, csc1)
        proj = lambda z, lo, hi: z @ w_in[l][:, lo:hi]
        ya_l, ya_c = _rwkv_mixer(proj(n_lat, 0, A_END), proj(n_ctx, 0, A_END), rwkv_shift[l], rwkv_w0[l],
                                 rwkv_w2[l], rwkv_a0[l], rwkv_a2[l], rwkv_g2[l], rwkv_kk[l], rwkv_ka[l],
                                 rwkv_rk[l], rwkv_gn_g[l], rwkv_gn_b[l])
        yb_l = _conformer_conv(proj(n_lat, A_END, B_END), conv_dw[l], conv_db[l], conv_ln_g[l], conv_ln_b[l])
        yb_c = _conformer_conv(proj(n_ctx, A_END, B_END), conv_dw[l], conv_db[l], conv_ln_g[l], conv_ln_b[l])
        yc_l, yc_c = _s5_mixer(proj(n_lat, B_END, C_END), proj(n_ctx, B_END, C_END), s5_lam_re[l], s5_lam_im[l],
                               s5_log_dt[l], s5_b_re[l], s5_b_im[l], s5_c_re[l], s5_c_im[l], s5_d[l],
                               s5_glu_w[l], s5_glu_b[l])
        yd_l, yd_c = _retention_mixer(proj(n_lat, C_END, IN_WIDTH), proj(n_ctx, C_END, IN_WIDTH),
                                      ret_gn_g[l], ret_gn_b[l], rope_cos, rope_sin)
        m_lat = _merge_branches(n_lat, (ya_l, yb_l, yc_l, yd_l), w_branch[l], w_merge[l], b_merge[l], w_out[l])
        m_ctx = _merge_branches(n_ctx, (ya_c, yb_c, yc_c, yd_c), w_branch[l], w_merge[l], b_merge[l], w_out[l])
        x = x + gt1 * m_lat
        ctx = ctx + cgt1 * m_ctx
        h_lat = _modulate(_rms_norm(x, norm2_g[l]), sh2, sc2)
        h_ctx = _modulate(_rms_norm(ctx, norm2_g[l]), csh2, csc2)
        bsz, n_ctx, d = ctx.shape
        h_all = jnp.concatenate([h_ctx, h_lat], axis=1)
        f_all = _peer_ffn(h_all.reshape(-1, d), peer_wq[l], peer_keys[l],
                          pack_bf16_pairs(peer_u[l]), pack_bf16_pairs(peer_v[l])).reshape(bsz, -1, d)
        x = x + gt2 * f_all[:, n_ctx:]
        ctx = ctx + cgt2 * f_all[:, :n_ctx]
    return _rms_norm(x, final_g)
```

```python
import functools
import math

import jax
import jax.numpy as jnp
from jax import lax
from jax.experimental import pallas as pl
from jax.experimental.pallas import tpu as pltpu
from jax.experimental.pallas import tpu_sc as plsc

F32 = jnp.float32
BF16 = jnp.bfloat16

D_MODEL = 1024
GRID_W = 64
EPS = 1e-6

RWKV_HEADS = 8
RWKV_HEAD_DIM = 64
RWKV_WIDTH = RWKV_HEADS * RWKV_HEAD_DIM
RWKV_DECAY_RANK = 64
RWKV_ICLR_RANK = 64
RWKV_GATE_RANK = 128
RWKV_CHUNK = 64
LANES = 128


def _bdot(a, b, dims):
    return lax.dot_general(a.astype(BF16), b.astype(BF16), (dims, ((), ())), preferred_element_type=F32)


def _mm(a, b):
    return _bdot(a, b, ((1,), (0,)))


def _mm_nt(a, b):
    return _bdot(a, b, ((1,), (1,)))


def _mm_tn(a, b):
    return _bdot(a, b, ((0,), (0,)))


def _rwkv_chunk_kernel(r_ref, v_ref, kk_ref, k_ref, b_ref, lw_ref, y_ref, s_ref):
    c = RWKV_CHUNK
    d = pl.program_id(1)
    j = pl.program_id(2)

    @pl.when(j == 0)
    def _():
        s_ref[...] = jnp.zeros_like(s_ref)

    sign = jnp.where(d == 0, 1, -1)
    row = lax.broadcasted_iota(jnp.int32, (c, c), 0)
    col = lax.broadcasted_iota(jnp.int32, (c, c), 1)
    cum_mat = jnp.where((row - col) * sign >= 0, 1.0, 0.0).astype(BF16)

    lw = lw_ref[0, 0]
    lw_hi = lw.astype(BF16)
    rem = lw - lw_hi.astype(F32)
    lw_mid = rem.astype(BF16)
    lw_lo = (rem - lw_mid.astype(F32)).astype(BF16)
    lc_incl = _mm(cum_mat, lw_hi) + _mm(cum_mat, lw_mid) + _mm(cum_mat, lw_lo)
    lc_excl = lc_incl - lw
    lc_ref = 0.5 * jnp.sum(lw, axis=0, keepdims=True)

    row2 = lax.broadcasted_iota(jnp.int32, (2 * c, 2 * c), 0)
    col2 = lax.broadcasted_iota(jnp.int32, (2 * c, 2 * c), 1)
    same_head = (row2 // c) == (col2 // c)
    tdiff = jnp.where(same_head, ((row2 % c) - (col2 % c)) * sign, -1)
    strict2 = tdiff > 0
    incl2 = tdiff >= 0
    eye2 = jnp.where(row2 == col2, 1.0, 0.0)
    lane_head = lax.broadcasted_iota(jnp.int32, (c, LANES), 1) // RWKV_HEAD_DIM

    def stack2(z):
        return jnp.concatenate([jnp.where(lane_head == 0, z, 0.0), jnp.where(lane_head == 1, z, 0.0)], axis=0)

    for p in range(RWKV_WIDTH // LANES):
        sl = slice(p * LANES, (p + 1) * LANES)
        r, v, kk = r_ref[0, :, sl], v_ref[0, :, sl], kk_ref[0, :, sl]
        k, b = k_ref[0, 0, :, sl], b_ref[0, 0, :, sl]
        ref = lc_ref[:, sl]
        e_in = jnp.exp(lc_incl[:, sl] - ref)
        e_ex = jnp.exp(lc_excl[:, sl] - ref)
        e_inv = jnp.exp(ref - lc_incl[:, sl])
        e_ref = jnp.exp(ref)
        q_stack = jnp.concatenate([stack2(kk * e_ex), stack2(r * e_in)], axis=0)
        k_stack = jnp.concatenate([stack2(k * e_inv), stack2(b * e_inv)], axis=0)
        x = _mm_nt(q_stack, k_stack)
        m_k = jnp.where(strict2, x[:2 * c, :2 * c], 0.0)
        m_b = jnp.where(strict2, x[:2 * c, 2 * c:], 0.0)
        p_k = jnp.where(incl2, x[2 * c:, :2 * c], 0.0)
        p_b = jnp.where(incl2, x[2 * c:, 2 * c:], 0.0)
        t_inv = eye2 - m_b
        m_pow = m_b
        for _ in range(int(math.log2(c)) - 1):
            m_pow = _mm(m_pow, m_pow)
            t_inv = t_inv + _mm(t_inv, m_pow)
        s0 = s_ref[p]
        v2 = stack2(v)
        q0_stack = jnp.concatenate([stack2(kk * (e_ex * e_ref)), stack2(r * (e_in * e_ref))], axis=0)
        from_state = _mm_nt(q0_stack, s0)
        from_chunk = _mm(jnp.concatenate([m_k, p_k], axis=0), v2)
        u2 = _mm(t_inv, from_state[:2 * c] + from_chunk[:2 * c])
        y2 = from_state[2 * c:] + from_chunk[2 * c:] - _mm(p_b, u2)
        y_ref[0, 0, :, sl] = y2[:c] + y2[c:]
        e_end = e_inv * e_ref
        kb_end = jnp.concatenate([stack2(k * e_end), stack2(-(b * e_end))], axis=0)
        s_ref[p] = s0 * (e_ref * e_ref) + _mm_tn(jnp.concatenate([v2, u2], axis=0), kb_end)


def rwkv_scan(r, v, kk, k_dir, b_dir, lw_dir, n_ctx):
    bsz, length, width = r.shape
    c = RWKV_CHUNK
    nc, nc_ctx = length // c, n_ctx // c

    def chunk_of(d, j):
        back = jnp.where(j < nc_ctx, nc_ctx - 1 - j, nc + nc_ctx - 1 - j)
        return jnp.where(d == 0, j, back)

    shared = pl.BlockSpec((1, c, width), lambda b, d, j: (b, chunk_of(d, j), 0))
    per_dir = pl.BlockSpec((1, 1, c, width), lambda b, d, j: (b, d, chunk_of(d, j), 0))
    return pl.pallas_call(
        _rwkv_chunk_kernel,
        grid=(bsz, 2, nc),
        in_specs=[shared, shared, shared, per_dir, per_dir, per_dir],
        out_specs=per_dir,
        out_shape=jax.ShapeDtypeStruct((bsz, 2, length, width), F32),
        scratch_shapes=[pltpu.VMEM((width // LANES, LANES, LANES), F32)],
        compiler_params=pltpu.CompilerParams(dimension_semantics=("arbitrary", "arbitrary", "arbitrary")),
        name="rwkv_scan",
    )(r, v, kk, k_dir, b_dir, lw_dir)


S5_CHUNK = 128


def _s5_chunk_kernel(u_ref, bre_ref, bim_ref, c_ref, are_ref, aim_ref, y_ref, carry_ref, *, reverse):
    tc = S5_CHUNK
    j = pl.program_id(1)

    @pl.when(j == 0)
    def _():
        carry_ref[...] = jnp.zeros_like(carry_ref)

    u = u_ref[0]
    xr = _mm(u, bre_ref[...])
    xi = _mm(u, bim_ref[...])
    ar, ai = are_ref[...], aim_ref[...]
    cr, ci = carry_ref[0:1, :], carry_ref[1:2, :]
    row = lax.broadcasted_iota(jnp.int32, xr.shape, 0)
    first = tc - 1 if reverse else 0
    xr = xr + jnp.where(row == first, ar * cr - ai * ci, 0.0)
    xi = xi + jnp.where(row == first, ar * ci + ai * cr, 0.0)
    for level in range(int(math.log2(tc))):
        sh = 1 << level
        if reverse:
            sr, si = pltpu.roll(xr, tc - sh, 0), pltpu.roll(xi, tc - sh, 0)
            keep = row < tc - sh
        else:
            sr, si = pltpu.roll(xr, sh, 0), pltpu.roll(xi, sh, 0)
            keep = row >= sh
        sr, si = jnp.where(keep, sr, 0.0), jnp.where(keep, si, 0.0)
        xr, xi = xr + (ar * sr - ai * si), xi + (ar * si + ai * sr)
        ar, ai = ar * ar - ai * ai, 2.0 * (ar * ai)
    last = 0 if reverse else tc - 1
    carry_ref[0:1, :] = xr[last:last + 1, :]
    carry_ref[1:2, :] = xi[last:last + 1, :]
    y_ref[0] = _mm(jnp.concatenate([xr, xi], axis=1), c_ref[...])


def s5_scan(u, b_re, b_im, c_cat, a_re, a_im, n_ctx, reverse):
    bsz, length, width = u.shape
    n_state = b_re.shape[1]
    tc = S5_CHUNK
    nc, nc_ctx = length // tc, n_ctx // tc

    def chunk_of(j):
        if not reverse:
            return j
        return jnp.where(j < nc_ctx, nc_ctx - 1 - j, nc + nc_ctx - 1 - j)

    tok = pl.BlockSpec((1, tc, width), lambda b, j: (b, chunk_of(j), 0))
    full = lambda shape: pl.BlockSpec(shape, lambda b, j: (0,) * len(shape))
    return pl.pallas_call(
        functools.partial(_s5_chunk_kernel, reverse=reverse),
        grid=(bsz, nc),
        in_specs=[tok, full(b_re.shape), full(b_im.shape), full(c_cat.shape), full(a_re.shape), full(a_im.shape)],
        out_specs=tok,
        out_shape=jax.ShapeDtypeStruct((bsz, length, width), F32),
        scratch_shapes=[pltpu.VMEM((8, n_state), F32)],
        compiler_params=pltpu.CompilerParams(dimension_semantics=("arbitrary", "arbitrary"),
                                             vmem_limit_bytes=48 * 1024 * 1024),
        name="s5_scan_rev" if reverse else "s5_scan_fwd",
    )(u, b_re, b_im, c_cat, a_re, a_im)


SC_CORES = 2
SC_SUBCORES = 16
SC_WORKERS = SC_CORES * SC_SUBCORES
GATHER_ROWS = 64
PEER_SLOTS = 128
PEER_TOKENS_PER_STEP = 16


PEER_HEADS = 8
PEER_KEYS = 128
PEER_TOPK = 16
PEER_HALF = 128
PEER_SELECT_TOKENS = 128


def _top_rows(s, payload=None):
    n_rows = s.shape[0]
    iota = lax.broadcasted_iota(jnp.int32, s.shape, 0)
    vals, picks = [], []
    for _ in range(PEER_TOPK):
        m = jnp.max(s, axis=0, keepdims=True)
        pos = jnp.min(jnp.where(s == m, iota, n_rows), axis=0, keepdims=True)
        hit = iota == pos
        vals.append(m)
        picks.append(pos if payload is None else jnp.max(jnp.where(hit, payload, -1), axis=0, keepdims=True))
        s = jnp.where(hit, -jnp.inf, s)
    return jnp.concatenate(vals, axis=0), jnp.concatenate(picks, axis=0)


def _peer_select_kernel(h_ref, wq_ref, keys_ref, ids_ref, gate_ref):
    q = _mm(h_ref[...], wq_ref[...]).astype(BF16)
    for h in range(PEER_HEADS):
        halves = []
        for p in range(2):
            lst = 2 * h + p
            s = _mm_nt(keys_ref[lst], q[:, lst * PEER_HALF:(lst + 1) * PEER_HALF])
            halves.append(_top_rows(s))
        (v1, p1), (v2, p2) = halves
        cand = jnp.concatenate([v1[i:i + 1] + v2 for i in range(PEER_TOPK)], axis=0)
        cand_id = jnp.concatenate([p1[i:i + 1] * PEER_KEYS + p2 for i in range(PEER_TOPK)], axis=0)
        best, ids = _top_rows(cand, cand_id)
        e = jnp.exp(best - best[0:1])
        ids_ref[h * PEER_TOPK:(h + 1) * PEER_TOPK, :] = ids
        gate_ref[h * PEER_TOPK:(h + 1) * PEER_TOPK, :] = e / jnp.sum(e, axis=0, keepdims=True)


def peer_select(h, wq, keys):
    n, d = h.shape
    tn = PEER_SELECT_TOKENS
    slots = PEER_HEADS * PEER_TOPK
    full = lambda a: pl.BlockSpec(a.shape, lambda i: (0,) * a.ndim)
    out = pl.BlockSpec((slots, tn), lambda i: (0, i))
    return pl.pallas_call(
        _peer_select_kernel,
        grid=(n // tn,),
        in_specs=[pl.BlockSpec((tn, d), lambda i: (i, 0)), full(wq), full(keys)],
        out_specs=[out, out],
        out_shape=[jax.ShapeDtypeStruct((slots, n), jnp.int32), jax.ShapeDtypeStruct((slots, n), F32)],
        compiler_params=pltpu.CompilerParams(dimension_semantics=("arbitrary",),
                                             vmem_limit_bytes=48 * 1024 * 1024),
        name="peer_select",
    )(h, wq, keys)


def pack_bf16_pairs(table):
    half = table.shape[1] // 2
    bits = lax.bitcast_convert_type(table.astype(BF16), jnp.uint16).astype(jnp.uint32)
    return bits[:, :half] | (bits[:, half:] << 16)


def sc_gather_rows(table, idx):
    n_rows, width = idx.shape[0], table.shape[1]
    per_worker = n_rows // SC_WORKERS
    n_pairs = per_worker // (2 * GATHER_ROWS)
    assert per_worker * SC_WORKERS == n_rows and n_pairs * 2 * GATHER_ROWS == per_worker
    mesh = plsc.VectorSubcoreMesh(core_axis_name="c", subcore_axis_name="s",
                                  num_cores=SC_CORES, num_subcores=SC_SUBCORES)

    @functools.partial(
        pl.kernel, mesh=mesh,
        out_type=jax.ShapeDtypeStruct((n_rows, width), table.dtype),
        scratch_types=[pltpu.VMEM((2, GATHER_ROWS), jnp.int32),
                       pltpu.VMEM((2, GATHER_ROWS, width), table.dtype),
                       pltpu.SemaphoreType.DMA((2,)),
                       pltpu.SemaphoreType.DMA((2,))],
        name="peer_sc_gather",
    )
    def gather(table_hbm, idx_hbm, out_hbm, idx_v, rows_v, gather_sem, write_sem):
        worker = lax.axis_index("s") * SC_CORES + lax.axis_index("c")
        base = worker * per_worker

        def rows_of(chunk):
            return pl.ds(pl.multiple_of(base + chunk * GATHER_ROWS, GATHER_ROWS), GATHER_ROWS)

        def gather_copy(slot):
            return pltpu.make_async_copy(table_hbm.at[idx_v.at[slot]], rows_v.at[slot], gather_sem.at[slot])

        def write_copy(chunk, slot):
            return pltpu.make_async_copy(rows_v.at[slot], out_hbm.at[rows_of(chunk)], write_sem.at[slot])

        def start_gather(chunk, slot):
            pltpu.sync_copy(idx_hbm.at[rows_of(chunk)], idx_v.at[slot])
            gather_copy(slot).start()

        start_gather(0, 0)

        @pl.loop(0, n_pairs)
        def _(g):
            even, odd = 2 * g, 2 * g + 1

            @pl.when(g > 0)
            def _():
                write_copy(odd - 2, 1).wait()
            start_gather(odd, 1)
            gather_copy(0).wait()
            write_copy(even, 0).start()
            gather_copy(1).wait()
            write_copy(odd, 1).start()
            write_copy(even, 0).wait()

            @pl.when(g + 1 < n_pairs)
            def _():
                start_gather(even + 2, 0)

        write_copy(2 * n_pairs - 1, 1).wait()

    return gather(table, idx)


def _unpack_pairs(words):
    lo = pltpu.bitcast(words << 16, F32)
    hi = pltpu.bitcast(words & jnp.uint32(0xFFFF0000), F32)
    return lo, hi


def _gelu_tanh(x):
    return 0.5 * x * (1.0 + jnp.tanh(0.7978845608028654 * (x + 0.044715 * (x * x * x))))


def _peer_expert_kernel(z_ref, gate_ref, ug_ref, vg_ref, o_ref):
    half = ug_ref.shape[2]
    gate_t = gate_ref[...].T
    for n in range(PEER_TOKENS_PER_STEP):
        z_lo, z_hi = z_ref[n:n + 1, :half], z_ref[n:n + 1, half:]
        u_lo, u_hi = _unpack_pairs(ug_ref[n])
        prod = u_lo * z_lo + u_hi * z_hi
        act = jnp.sum(prod, axis=1, keepdims=True)
        w = _gelu_tanh(act) * gate_t[:, n:n + 1]
        v_lo, v_hi = _unpack_pairs(vg_ref[n])
        o_ref[n:n + 1, :half] = jnp.sum(w * v_lo, axis=0, keepdims=True)
        o_ref[n:n + 1, half:] = jnp.sum(w * v_hi, axis=0, keepdims=True)


def peer_experts(z, gate, ug, vg):
    n, d = z.shape
    tn = PEER_TOKENS_PER_STEP
    slots, half = ug.shape[1], ug.shape[2]
    return pl.pallas_call(
        _peer_expert_kernel,
        grid=(n // tn,),
        in_specs=[pl.BlockSpec((tn, d), lambda i: (i, 0)),
                  pl.BlockSpec((tn, slots), lambda i: (i, 0)),
                  pl.BlockSpec((tn, slots, half), lambda i: (i, 0, 0)),
                  pl.BlockSpec((tn, slots, half), lambda i: (i, 0, 0))],
        out_specs=pl.BlockSpec((tn, d), lambda i: (i, 0)),
        out_shape=jax.ShapeDtypeStruct((n, d), F32),
        compiler_params=pltpu.CompilerParams(dimension_semantics=("arbitrary",),
                                             vmem_limit_bytes=48 * 1024 * 1024),
        name="peer_experts",
    )(z, gate, ug, vg)


CONV_CHANNELS = 512
CONV_TAPS = 31
S5_WIDTH = 512
S5_GROUP = 16
S5_GROUPS = S5_WIDTH // S5_GROUP
S5_STATE = 64
S5_MAX_RE = -1e-4
RET_HEADS = 4
RET_HEAD_DIM = 128
RET_WIDTH = RET_HEADS * RET_HEAD_DIM
RET_CHUNK = 128
ROPE_BASE = 10000.0
N_BRANCH = 4
PEER_HEADS = 8
PEER_KEYS = 128
PEER_TOPK = 16
PEER_QUERY = 256
PEER_HALF = PEER_QUERY // 2
PEER_BLOCK = 128
SHIFT_TAPS = 3
RWKV_IN = 3 * RWKV_WIDTH + 2 * RWKV_DECAY_RANK + 2 * RWKV_ICLR_RANK + RWKV_GATE_RANK
CONV_IN = 2 * CONV_CHANNELS
S5_IN = S5_WIDTH
RET_IN = 5 * RET_WIDTH
A_END = RWKV_IN
B_END = A_END + CONV_IN
C_END = B_END + S5_IN
IN_WIDTH = C_END + RET_IN
RWKV_SPLITS = (RWKV_WIDTH, 2 * RWKV_WIDTH, 3 * RWKV_WIDTH,
               3 * RWKV_WIDTH + RWKV_DECAY_RANK, 3 * RWKV_WIDTH + 2 * RWKV_DECAY_RANK,
               3 * RWKV_WIDTH + 2 * RWKV_DECAY_RANK + RWKV_ICLR_RANK,
               3 * RWKV_WIDTH + 2 * RWKV_DECAY_RANK + 2 * RWKV_ICLR_RANK)


def _rms_norm(z, g):
    zf = z.astype(F32)
    y = zf * lax.rsqrt(jnp.mean(zf * zf, axis=-1, keepdims=True) + EPS)
    return (y * g).astype(z.dtype)


def _normalise(z):
    zf = z.astype(F32)
    zc = zf - jnp.mean(zf, axis=-1, keepdims=True)
    return zc * lax.rsqrt(jnp.mean(zc * zc, axis=-1, keepdims=True) + EPS)


def _modulate(z, shift, scale):
    return z * (1.0 + scale) + shift


def _depthwise_conv(z, w):
    taps = w.shape[0]
    pad = (taps - 1) // 2
    return lax.conv_general_dilated(
        z, w[:, None, :].astype(z.dtype), window_strides=(1,), padding=[(pad, pad)],
        dimension_numbers=("NWC", "WIO", "NWC"), feature_group_count=z.shape[-1])


def _axial_rope(n_tokens):
    rows = n_tokens // GRID_W
    row = jnp.repeat(jnp.arange(rows, dtype=F32), GRID_W)
    col = jnp.tile(jnp.arange(GRID_W, dtype=F32), rows)
    n_freq = RET_HEAD_DIM // 4
    inv = ROPE_BASE ** (-jnp.arange(n_freq, dtype=F32) / n_freq)
    ang = jnp.concatenate([row[:, None] * inv, col[:, None] * inv], axis=-1)
    return jnp.cos(ang), jnp.sin(ang)


def _apply_rope(z, cos, sin):
    z1, z2 = jnp.split(z, 2, axis=-1)
    cs, sn = cos[None, :, None, :], sin[None, :, None, :]
    return jnp.concatenate([z1 * cs - z2 * sn, z1 * sn + z2 * cs], axis=-1)


def _rwkv_prep(p, shift_w, w0, w2, a0, a2, g2, k_k, k_a):
    p = _depthwise_conv(p, shift_w)
    r, k, v, wl_f, wl_b, al_f, al_b, gl = jnp.split(p, RWKV_SPLITS, axis=-1)
    b, t, _ = r.shape
    kk = (k * k_k).reshape(b, t, RWKV_HEADS, RWKV_HEAD_DIM)
    kk = (kk * lax.rsqrt(jnp.sum(kk * kk, axis=-1, keepdims=True) + EPS)).reshape(b, t, RWKV_WIDTH)
    g = jax.nn.sigmoid(gl) @ g2
    ks, bs, lws = [], [], []
    for d, (wl, al) in enumerate(((wl_f, al_f), (wl_b, al_b))):
        w_log = -jax.nn.softplus(-(w0[d] + jnp.tanh(wl) @ w2[d])) - 0.5
        a = jax.nn.sigmoid(a0[d] + al @ a2[d])
        ks.append(k * (1.0 + (a - 1.0) * k_a))
        bs.append(a * kk)
        lws.append(-jnp.exp(w_log))
    return r, v, kk, g, jnp.stack(ks, 1), jnp.stack(bs, 1), jnp.stack(lws, 1)


def _rwkv_mixer(p_lat, p_ctx, shift_w, w0, w2, a0, a2, g2, k_k, k_a, r_k, gn_g, gn_b):
    n_ctx = p_ctx.shape[1]
    prep_l = _rwkv_prep(p_lat, shift_w, w0, w2, a0, a2, g2, k_k, k_a)
    prep_c = _rwkv_prep(p_ctx, shift_w, w0, w2, a0, a2, g2, k_k, k_a)
    r, v, kk, g = (jnp.concatenate([c, l], axis=1) for c, l in zip(prep_c[:4], prep_l[:4]))
    k_dir, b_dir, lw_dir = (jnp.concatenate([c, l], axis=2) for c, l in zip(prep_c[4:], prep_l[4:]))
    y = rwkv_scan(r, v, kk, k_dir, b_dir, lw_dir, n_ctx)
    y = y[:, 0] + y[:, 1]
    b, t, _ = y.shape
    heads = lambda z: z.reshape(b, t, RWKV_HEADS, RWKV_HEAD_DIM)
    k_sum = k_dir[:, 0] + k_dir[:, 1]
    bonus = jnp.sum(heads(r) * heads(k_sum) * r_k, axis=-1, keepdims=True) * heads(v)
    o = _normalise(heads(y)).reshape(b, t, RWKV_WIDTH) * gn_g + gn_b + bonus.reshape(b, t, RWKV_WIDTH)
    o = o * g
    return o[:, n_ctx:], o[:, :n_ctx]


def _conformer_conv(p, dw, db, ln_g, ln_b):
    val, gate = jnp.split(p, 2, axis=-1)
    z = val * jax.nn.sigmoid(gate)
    z = _depthwise_conv(z, dw) + db
    z = _normalise(z) * ln_g + ln_b
    return jax.nn.silu(z).astype(p.dtype)


def _s5_discretise(lam_re, lam_im, log_dt, b_re, b_im):
    lam_re = jnp.minimum(lam_re.astype(F32), S5_MAX_RE)
    lam_im = lam_im.astype(F32)
    dt = jnp.exp(log_dt.astype(F32))[:, None]
    mag = jnp.exp(lam_re * dt)
    ang = lam_im * dt
    ab_re, ab_im = mag * jnp.cos(ang), mag * jnp.sin(ang)
    den = lam_re * lam_re + lam_im * lam_im
    nr, ni = ab_re - 1.0, ab_im
    f_re = (nr * lam_re + ni * lam_im) / den
    f_im = (ni * lam_re - nr * lam_im) / den
    b_re, b_im = b_re.astype(F32), b_im.astype(F32)
    bb_re = f_re[..., None] * b_re - f_im[..., None] * b_im
    bb_im = f_re[..., None] * b_im + f_im[..., None] * b_re
    return ab_re, ab_im, bb_re, bb_im


def _s5_mixer(u_lat, u_ctx, lam_re, lam_im, log_dt, b_re, b_im, c_re, c_im, d_skip, glu_w, glu_b):
    n_ctx = u_ctx.shape[1]
    u = jnp.concatenate([u_ctx, u_lat], axis=1).astype(F32)
    eye = jnp.eye(S5_GROUPS, dtype=F32)
    n_state = S5_GROUPS * S5_STATE
    y = d_skip * u
    for d in range(2):
        ab_re, ab_im, bb_re, bb_im = _s5_discretise(lam_re[d], lam_im[d], log_dt[d], b_re[d], b_im[d])
        blk_in = lambda bb: jnp.einsum("gph,gk->ghkp", bb, eye).reshape(S5_WIDTH, n_state).astype(BF16)
        blk_out = lambda cc: jnp.einsum("ghp,gk->kpgh", cc.astype(F32), eye).reshape(n_state, S5_WIDTH)
        c_cat = jnp.concatenate([blk_out(c_re[d]), -blk_out(c_im[d])], axis=0).astype(BF16)
        y = y + s5_scan(u, blk_in(bb_re), blk_in(bb_im), c_cat, ab_re.reshape(1, n_state),
                        ab_im.reshape(1, n_state), n_ctx, reverse=(d == 1))
    z = jax.nn.gelu(y)
    out = z * jax.nn.sigmoid(z @ glu_w + glu_b)
    return out[:, n_ctx:], out[:, :n_ctx]


def _retention_chunkwise(q, k, v, r0, log_gamma, strict, emit):
    b, t, h, _ = q.shape
    n_chunks = t // RET_CHUNK
    chunks = lambda z: z.reshape(b, n_chunks, RET_CHUNK, h, z.shape[-1]).transpose(1, 0, 3, 2, 4)
    pos = jnp.arange(RET_CHUNK, dtype=F32)
    diff = pos[:, None] - pos[None, :]
    keep = diff > 0 if strict else diff >= 0
    decay_in = jnp.where(keep, jnp.exp(jnp.where(keep, diff, 0.0) * log_gamma[:, None, None]), 0.0)
    decay_q = jnp.exp((pos + 1.0) * log_gamma[:, None])[None, :, :, None]
    decay_k = jnp.exp((RET_CHUNK - 1.0 - pos) * log_gamma[:, None])[None, :, :, None]
    decay_chunk = jnp.exp(RET_CHUNK * log_gamma)[None, :, None, None]

    def step(r, inp):
        qc, kc, vc = inp
        r_next = decay_chunk * r + jnp.einsum("bhck,bhcv->bhkv", kc * decay_k, vc)
        if not emit:
            return r_next, None
        scores = jnp.einsum("bhnk,bhmk->bhnm", qc, kc) * decay_in
        o = jnp.einsum("bhnm,bhmv->bhnv", scores, vc) + jnp.einsum("bhnk,bhkv->bhnv", qc * decay_q, r)
        return r_next, o
    r_final, o = lax.scan(step, r0, (chunks(q), chunks(k), chunks(v)))
    if emit:
        o = o.transpose(1, 0, 3, 2, 4).reshape(b, t, h, v.shape[-1])
    return o, r_final


def _retention_mixer(p_lat, p_ctx, gn_g, gn_b, rope_cos, rope_sin):
    log_gamma = jnp.log1p(-jnp.exp2(-5.0 - jnp.arange(RET_HEADS, dtype=F32)))
    k_scale = RET_HEAD_DIM ** -0.5

    def split_heads(p):
        b, t, _ = p.shape
        q, kf, kb, v, g = jnp.split(p.astype(F32), 5, axis=-1)
        hd = lambda z: z.reshape(b, t, RET_HEADS, RET_HEAD_DIM)
        return hd(q), hd(kf) * k_scale, hd(kb) * k_scale, hd(v), g
    ql, kfl, kbl, vl, gl = split_heads(p_lat)
    ql, kfl, kbl = (_apply_rope(z, rope_cos, rope_sin) for z in (ql, kfl, kbl))
    qc, kfc, kbc, vc, gc = split_heads(p_ctx)
    r_zero = jnp.zeros((p_ctx.shape[0], RET_HEADS, RET_HEAD_DIM, RET_HEAD_DIM), F32)
    flip = lambda z: jnp.flip(z, axis=1)
    oc_f, rc_f = _retention_chunkwise(qc, kfc, vc, r_zero, log_gamma, False, True)
    oc_b, rc_b = _retention_chunkwise(flip(qc), flip(kbc), flip(vc), r_zero, log_gamma, True, True)
    ol_f, _ = _retention_chunkwise(ql, kfl, vl, rc_f, log_gamma, False, True)
    ol_b, _ = _retention_chunkwise(flip(ql), flip(kbl), flip(vl), rc_b, log_gamma, True, True)

    def readout(o, g):
        b, t = o.shape[:2]
        return (_normalise(o).reshape(b, t, RET_WIDTH) * gn_g + gn_b) * jax.nn.silu(g)
    return readout(ol_f + flip(ol_b), gl), readout(oc_f + flip(oc_b), gc)


def _merge_branches(n, ys, w_branch, w_merge, b_merge, w_out):
    m = 0.0
    for i, y in enumerate(ys):
        gate = jax.nn.sigmoid(n @ w_merge[i] + b_merge[i])
        m = m + gate * (y.astype(n.dtype) @ w_branch[i])
    return m @ w_out


def _peer_ffn(h, w_q, sub_keys, u_packed, v_packed):
    n, d = h.shape
    keys = sub_keys.reshape(2 * PEER_HEADS, PEER_KEYS, PEER_HALF).astype(BF16)
    ids_t, gate_t = peer_select(h, w_q.astype(BF16), keys)
    ids, gate = ids_t.T, gate_t.T
    step = 2 * SC_WORKERS * GATHER_ROWS // PEER_SLOTS
    n_blk = next(k for k in (8, 4, 2, 1) if n % (k * step) == 0)
    nb = n // n_blk
    outs = []
    for i in range(n_blk):
        rows = slice(i * nb, (i + 1) * nb)
        flat_ids = ids[rows].reshape(-1)
        ug = sc_gather_rows(u_packed, flat_ids).reshape(nb, PEER_SLOTS, d // 2)
        vg = sc_gather_rows(v_packed, flat_ids).reshape(nb, PEER_SLOTS, d // 2)
        outs.append(peer_experts(h[rows], gate[rows], ug, vg))
    return jnp.concatenate(outs, axis=0)


def kernel(x, c, ctx, c_ctx, ada_w, ada_b, norm1_g, norm2_g, w_in, rwkv_shift, rwkv_w0, rwkv_w2,
           rwkv_a0, rwkv_a2, rwkv_g2, rwkv_kk, rwkv_ka, rwkv_rk, rwkv_gn_g, rwkv_gn_b,
           conv_dw, conv_db, conv_ln_g, conv_ln_b, s5_lam_re, s5_lam_im, s5_log_dt,
           s5_b_re, s5_b_im, s5_c_re, s5_c_im, s5_d, s5_glu_w, s5_glu_b, ret_gn_g, ret_gn_b,
           w_branch, w_merge, b_merge, w_out, peer_wq, peer_keys, peer_u, peer_v, final_g):
    depth = ada_w.shape[0]
    rope_cos, rope_sin = _axial_rope(x.shape[1])
    for l in range(depth):
        mod = jax.nn.silu(c) @ ada_w[l] + ada_b[l]
        sh1, sc1, gt1, sh2, sc2, gt2 = jnp.split(mod[:, None, :], 6, axis=-1)
        cmod = jax.nn.silu(c_ctx) @ ada_w[l] + ada_b[l]
        csh1, csc1, cgt1, csh2, csc2, cgt2 = jnp.split(cmod, 6, axis=-1)
        n_lat = _modulate(_rms_norm(x, norm1_g[l]), sh1, sc1)
        n_ctx = _modulate(_rms_norm(ctx, norm1_g[l]), csh1, csc1)
        proj = lambda z, lo, hi: z @ w_in[l][:, lo:hi]
        ya_l, ya_c = _rwkv_mixer(proj(n_lat, 0, A_END), proj(n_ctx, 0, A_END), rwkv_shift[l], rwkv_w0[l],
                                 rwkv_w2[l], rwkv_a0[l], rwkv_a2[l], rwkv_g2[l], rwkv_kk[l], rwkv_ka[l],
                                 rwkv_rk[l], rwkv_gn_g[l], rwkv_gn_b[l])
        yb_l = _conformer_conv(proj(n_lat, A_END, B_END), conv_dw[l], conv_db[l], conv_ln_g[l], conv_ln_b[l])
        yb_c = _conformer_conv(proj(n_ctx, A_END, B_END), conv_dw[l], conv_db[l], conv_ln_g[l], conv_ln_b[l])
        yc_l, yc_c = _s5_mixer(proj(n_lat, B_END, C_END), proj(n_ctx, B_END, C_END), s5_lam_re[l], s5_lam_im[l],
                               s5_log_dt[l], s5_b_re[l], s5_b_im[l], s5_c_re[l], s5_c_im[l], s5_d[l],
                               s5_glu_w[l], s5_glu_b[l])
        yd_l, yd_c = _retention_mixer(proj(n_lat, C_END, IN_WIDTH), proj(n_ctx, C_END, IN_WIDTH),
                                      ret_gn_g[l], ret_gn_b[l], rope_cos, rope_sin)
        m_lat = _merge_branches(n_lat, (ya_l, yb_l, yc_l, yd_l), w_branch[l], w_merge[l], b_merge[l], w_out[l])
        m_ctx = _merge_branches(n_ctx, (ya_c, yb_c, yc_c, yd_c), w_branch[l], w_merge[l], b_merge[l], w_out[l])
        x = x + gt1 * m_lat
        ctx = ctx + cgt1 * m_ctx
        h_lat = _modulate(_rms_norm(x, norm2_g[l]), sh2, sc2)
        h_ctx = _modulate(_rms_norm(ctx, norm2_g[l]), csh2, csc2)
        bsz, n_ctx, d = ctx.shape
        h_all = jnp.concatenate([h_ctx, h_lat], axis=1)
        f_all = _peer_ffn(h_all.reshape(-1, d), peer_wq[l], peer_keys[l],
                          pack_bf16_pairs(peer_u[l]), pack_bf16_pairs(peer_v[l])).reshape(bsz, -1, d)
        x = x + gt2 * f_all[:, n_ctx:]
        ctx = ctx + cgt2 * f_all[:, :n_ctx]
    return _rms_norm(x, final_g)
```

```python
import functools
import math

import jax
import jax.numpy as jnp
from jax import lax
from jax.experimental import pallas as pl
from jax.experimental.pallas import tpu as pltpu
from jax.experimental.pallas import tpu_sc as plsc

F32 = jnp.float32
BF16 = jnp.bfloat16

D_MODEL = 1024
GRID_W = 64
EPS = 1e-6

RWKV_HEADS = 8
RWKV_HEAD_DIM = 64
RWKV_WIDTH = RWKV_HEADS * RWKV_HEAD_DIM
RWKV_DECAY_RANK = 64
RWKV_ICLR_RANK = 64
RWKV_GATE_RANK = 128
RWKV_CHUNK = 64
LANES = 128


def _bdot(a, b, dims):
    return lax.dot_general(a.astype(BF16), b.astype(BF16), (dims, ((), ())), preferred_element_type=F32)


def _mm(a, b):
    return _bdot(a, b, ((1,), (0,)))


def _mm_nt(a, b):
    return _bdot(a, b, ((1,), (1,)))


def _mm_tn(a, b):
    return _bdot(a, b, ((0,), (0,)))


def _rwkv_chunk_kernel(r_ref, v_ref, kk_ref, k_ref, b_ref, lw_ref, y_ref, s_ref):
    c = RWKV_CHUNK
    d = pl.program_id(1)
    j = pl.program_id(2)

    @pl.when(j == 0)
    def _():
        s_ref[...] = jnp.zeros_like(s_ref)

    sign = jnp.where(d == 0, 1, -1)
    row = lax.broadcasted_iota(jnp.int32, (c, c), 0)
    col = lax.broadcasted_iota(jnp.int32, (c, c), 1)
    cum_mat = jnp.where((row - col) * sign >= 0, 1.0, 0.0).astype(BF16)

    lw = lw_ref[0, 0]
    lw_hi = lw.astype(BF16)
    rem = lw - lw_hi.astype(F32)
    lw_mid = rem.astype(BF16)
    lw_lo = (rem - lw_mid.astype(F32)).astype(BF16)
    lc_incl = _mm(cum_mat, lw_hi) + _mm(cum_mat, lw_mid) + _mm(cum_mat, lw_lo)
    lc_excl = lc_incl - lw
    lc_ref = 0.5 * jnp.sum(lw, axis=0, keepdims=True)

    row2 = lax.broadcasted_iota(jnp.int32, (2 * c, 2 * c), 0)
    col2 = lax.broadcasted_iota(jnp.int32, (2 * c, 2 * c), 1)
    same_head = (row2 // c) == (col2 // c)
    tdiff = jnp.where(same_head, ((row2 % c) - (col2 % c)) * sign, -1)
    strict2 = tdiff > 0
    incl2 = tdiff >= 0
    eye2 = jnp.where(row2 == col2, 1.0, 0.0)
    lane_head = lax.broadcasted_iota(jnp.int32, (c, LANES), 1) // RWKV_HEAD_DIM

    def stack2(z):
        return jnp.concatenate([jnp.where(lane_head == 0, z, 0.0), jnp.where(lane_head == 1, z, 0.0)], axis=0)

    for p in range(RWKV_WIDTH // LANES):
        sl = slice(p * LANES, (p + 1) * LANES)
        r, v, kk = r_ref[0, :, sl], v_ref[0, :, sl], kk_ref[0, :, sl]
        k, b = k_ref[0, 0, :, sl], b_ref[0, 0, :, sl]
        ref = lc_ref[:, sl]
        e_in = jnp.exp(lc_incl[:, sl] - ref)
        e_ex = jnp.exp(lc_excl[:, sl] - ref)
        e_inv = jnp.exp(ref - lc_incl[:, sl])
        e_ref = jnp.exp(ref)
        q_stack = jnp.concatenate([stack2(kk * e_ex), stack2(r * e_in)], axis=0)
        k_stack = jnp.concatenate([stack2(k * e_inv), stack2(b * e_inv)], axis=0)
        x = _mm_nt(q_stack, k_stack)
        m_k = jnp.where(strict2, x[:2 * c, :2 * c], 0.0)
        m_b = jnp.where(strict2, x[:2 * c, 2 * c:], 0.0)
        p_k = jnp.where(incl2, x[2 * c:, :2 * c], 0.0)
        p_b = jnp.where(incl2, x[2 * c:, 2 * c:], 0.0)
        t_inv = eye2 - m_b
        m_pow = m_b
        for _ in range(int(math.log2(c)) - 1):
            m_pow = _mm(m_pow, m_pow)
            t_inv = t_inv + _mm(t_inv, m_pow)
        s0 = s_ref[p]
        v2 = stack2(v)
        q0_stack = jnp.concatenate([stack2(kk * (e_ex * e_ref)), stack2(r * (e_in * e_ref))], axis=0)
        from_state = _mm_nt(q0_stack, s0)
        from_chunk = _mm(jnp.concatenate([m_k, p_k], axis=0), v2)
        u2 = _mm(t_inv, from_state[:2 * c] + from_chunk[:2 * c])
        y2 = from_state[2 * c:] + from_chunk[2 * c:] - _mm(p_b, u2)
        y_ref[0, 0, :, sl] = y2[:c] + y2[c:]
        e_end = e_inv * e_ref
        kb_end = jnp.concatenate([stack2(k * e_end), stack2(-(b * e_end))], axis=0)
        s_ref[p] = s0 * (e_ref * e_ref) + _mm_tn(jnp.concatenate([v2, u2], axis=0), kb_end)


def rwkv_scan(r, v, kk, k_dir, b_dir, lw_dir, n_ctx):
    bsz, length, width = r.shape
    c = RWKV_CHUNK
    nc, nc_ctx = length // c, n_ctx // c

    def chunk_of(d, j):
        back = jnp.where(j < nc_ctx, nc_ctx - 1 - j, nc + nc_ctx - 1 - j)
        return jnp.where(d == 0, j, back)

    shared = pl.BlockSpec((1, c, width), lambda b, d, j: (b, chunk_of(d, j), 0))
    per_dir = pl.BlockSpec((1, 1, c, width), lambda b, d, j: (b, d, chunk_of(d, j), 0))
    return pl.pallas_call(
        _rwkv_chunk_kernel,
        grid=(bsz, 2, nc),
        in_specs=[shared, shared, shared, per_dir, per_dir, per_dir],
        out_specs=per_dir,
        out_shape=jax.ShapeDtypeStruct((bsz, 2, length, width), F32),
        scratch_shapes=[pltpu.VMEM((width // LANES, LANES, LANES), F32)],
        compiler_params=pltpu.CompilerParams(dimension_semantics=("arbitrary", "arbitrary", "arbitrary")),
        name="rwkv_scan",
    )(r, v, kk, k_dir, b_dir, lw_dir)


S5_CHUNK = 128


def _s5_chunk_kernel(u_ref, bre_ref, bim_ref, c_ref, are_ref, aim_ref, y_ref, carry_ref, *, reverse):
    tc = S5_CHUNK
    j = pl.program_id(1)

    @pl.when(j == 0)
    def _():
        carry_ref[...] = jnp.zeros_like(carry_ref)

    u = u_ref[0]
    xr = _mm(u, bre_ref[...])
    xi = _mm(u, bim_ref[...])
    ar, ai = are_ref[...], aim_ref[...]
    cr, ci = carry_ref[0:1, :], carry_ref[1:2, :]
    row = lax.broadcasted_iota(jnp.int32, xr.shape, 0)
    first = tc - 1 if reverse else 0
    xr = xr + jnp.where(row == first, ar * cr - ai * ci, 0.0)
    xi = xi + jnp.where(row == first, ar * ci + ai * cr, 0.0)
    for level in range(int(math.log2(tc))):
        sh = 1 << level
        if reverse:
            sr, si = pltpu.roll(xr, tc - sh, 0), pltpu.roll(xi, tc - sh, 0)
            keep = row < tc - sh
        else:
            sr, si = pltpu.roll(xr, sh, 0), pltpu.roll(xi, sh, 0)
            keep = row >= sh
        sr, si = jnp.where(keep, sr, 0.0), jnp.where(keep, si, 0.0)
        xr, xi = xr + (ar * sr - ai * si), xi + (ar * si + ai * sr)
        ar, ai = ar * ar - ai * ai, 2.0 * (ar * ai)
    last = 0 if reverse else tc - 1
    carry_ref[0:1, :] = xr[last:last + 1, :]
    carry_ref[1:2, :] = xi[last:last + 1, :]
    y_ref[0] = _mm(jnp.concatenate([xr, xi], axis=1), c_ref[...])


def s5_scan(u, b_re, b_im, c_cat, a_re, a_im, n_ctx, reverse):
    bsz, length, width = u.shape
    n_state = b_re.shape[1]
    tc = S5_CHUNK
    nc, nc_ctx = length // tc, n_ctx // tc

    def chunk_of(j):
        if not reverse:
            return j
        return jnp.where(j < nc_ctx, nc_ctx - 1 - j, nc + nc_ctx - 1 - j)

    tok = pl.BlockSpec((1, tc, width), lambda b, j: (b, chunk_of(j), 0))
    full = lambda shape: pl.BlockSpec(shape, lambda b, j: (0,) * len(shape))
    return pl.pallas_call(
        functools.partial(_s5_chunk_kernel, reverse=reverse),
        grid=(bsz, nc),
        in_specs=[tok, full(b_re.shape), full(b_im.shape), full(c_cat.shape), full(a_re.shape), full(a_im.shape)],
        out_specs=tok,
        out_shape=jax.ShapeDtypeStruct((bsz, length, width), F32),
        scratch_shapes=[pltpu.VMEM((8, n_state), F32)],
        compiler_params=pltpu.CompilerParams(dimension_semantics=("arbitrary", "arbitrary"),
                                             vmem_limit_bytes=48 * 1024 * 1024),
        name="s5_scan_rev" if reverse else "s5_scan_fwd",
    )(u, b_re, b_im, c_cat, a_re, a_im)


RET_HEADS = 4
RET_HEAD_DIM = 128
RET_WIDTH = RET_HEADS * RET_HEAD_DIM
RET_CHUNK = 128
ROPE_BASE = 10000.0
ROW_TILE = 256


def _ret_chunk_kernel(q_ref, k_ref, v_ref, cos_ref, sin_ref, dm_ref, dq_ref, dk_ref, o_ref, s_ref, *, chunk_decay):
    j = pl.program_id(2)

    @pl.when(j == 0)
    def _():
        s_ref[...] = jnp.zeros_like(s_ref)

    cos2, sin2 = cos_ref[...], sin_ref[...]
    rope = lambda z: z * cos2 + pltpu.roll(z, RET_HEAD_DIM // 2, 1) * sin2
    for h in range(RET_HEADS):
        sl = slice(h * RET_HEAD_DIM, (h + 1) * RET_HEAD_DIM)
        q = rope(q_ref[0, :, sl])
        k = rope(k_ref[0, :, sl] * (RET_HEAD_DIM ** -0.5))
        v = v_ref[0, :, sl]
        s0 = s_ref[h]
        scores = _mm_nt(q, k) * dm_ref[0, h]
        o_ref[0, 0, :, sl] = _mm(scores, v) + _mm(q * dq_ref[0, h], s0)
        s_ref[h] = chunk_decay[h] * s0 + _mm_tn(k * dk_ref[0, h], v)


def _ret_decay_tables():
    c = RET_CHUNK
    lg = jnp.log1p(-jnp.exp2(-5.0 - jnp.arange(RET_HEADS, dtype=F32)))[:, None, None]
    n = jnp.arange(c, dtype=F32)[:, None]
    m = jnp.arange(c, dtype=F32)[None, :]
    fwd = jnp.where(n >= m, jnp.exp(jnp.where(n >= m, n - m, 0.0) * lg), 0.0)
    bwd = jnp.where(m > n, jnp.exp(jnp.where(m > n, m - n, 0.0) * lg), 0.0)
    ones = jnp.ones((1, c), F32)
    dq = jnp.stack([jnp.exp((n + 1.0) * lg) * ones, jnp.exp((c - n) * lg) * ones])
    dk = jnp.stack([jnp.exp((c - 1.0 - n) * lg) * ones, jnp.exp(n * lg) * ones])
    return jnp.stack([fwd, bwd]), dq, dk


def retention_scan(p_ret, cos2, sin2, n_ctx):
    bsz, length, _ = p_ret.shape
    c, w = RET_CHUNK, RET_WIDTH
    nc, nc_ctx = length // c, n_ctx // c
    dm, dq, dk = _ret_decay_tables()
    chunk_decay = tuple(math.exp(c * math.log1p(-2.0 ** (-5 - h))) for h in range(RET_HEADS))

    def chunk_of(d, j):
        back = jnp.where(j < nc_ctx, nc_ctx - 1 - j, nc + nc_ctx - 1 - j)
        return jnp.where(d == 0, j, back)

    col = lambda which: pl.BlockSpec((1, c, w), lambda b, d, j: (b, chunk_of(d, j), which(d)))
    rope_spec = pl.BlockSpec((c, RET_HEAD_DIM), lambda b, d, j: (chunk_of(d, j), 0))
    table = pl.BlockSpec((1, RET_HEADS, c, c), lambda b, d, j: (d, 0, 0, 0))
    return pl.pallas_call(
        functools.partial(_ret_chunk_kernel, chunk_decay=chunk_decay),
        grid=(bsz, 2, nc),
        in_specs=[col(lambda d: 0), col(lambda d: 1 + d), col(lambda d: 3), rope_spec, rope_spec, table, table, table],
        out_specs=pl.BlockSpec((1, 1, c, w), lambda b, d, j: (b, d, chunk_of(d, j), 0)),
        out_shape=jax.ShapeDtypeStruct((bsz, 2, length, w), F32),
        scratch_shapes=[pltpu.VMEM((RET_HEADS, RET_HEAD_DIM, RET_HEAD_DIM), F32)],
        compiler_params=pltpu.CompilerParams(dimension_semantics=("arbitrary", "arbitrary", "arbitrary")),
        name="retention_scan",
    )(p_ret, p_ret, p_ret, cos2, sin2, dm, dq, dk)


def _ret_out_kernel(o_ref, g_ref, gn_g_ref, gn_b_ref, y_ref):
    o = o_ref[0, 0] + o_ref[0, 1]
    g = g_ref[0]
    for h in range(RET_HEADS):
        sl = slice(h * RET_HEAD_DIM, (h + 1) * RET_HEAD_DIM)
        z = o[:, sl]
        zc = z - jnp.mean(z, axis=1, keepdims=True)
        zn = zc * lax.rsqrt(jnp.mean(zc * zc, axis=1, keepdims=True) + EPS)
        gate = g[:, sl]
        y_ref[0, :, sl] = (zn * gn_g_ref[:, sl] + gn_b_ref[:, sl]) * (gate * jax.nn.sigmoid(gate))


def retention_out(o, p_ret, gn_g, gn_b):
    bsz, _, length, w = o.shape
    tm = ROW_TILE
    vec = pl.BlockSpec((1, w), lambda b, i: (0, 0))
    return pl.pallas_call(
        _ret_out_kernel,
        grid=(bsz, length // tm),
        in_specs=[pl.BlockSpec((1, 2, tm, w), lambda b, i: (b, 0, i, 0)),
                  pl.BlockSpec((1, tm, w), lambda b, i: (b, i, 4)), vec, vec],
        out_specs=pl.BlockSpec((1, tm, w), lambda b, i: (b, i, 0)),
        out_shape=jax.ShapeDtypeStruct((bsz, length, w), F32),
        compiler_params=pltpu.CompilerParams(dimension_semantics=("arbitrary", "arbitrary")),
        name="retention_out",
    )(o, p_ret, gn_g.reshape(1, w), gn_b.reshape(1, w))


def rope_tables(n_tokens, n_ctx):
    rows = n_tokens // GRID_W
    row = jnp.repeat(jnp.arange(rows, dtype=F32), GRID_W)
    col = jnp.tile(jnp.arange(GRID_W, dtype=F32), rows)
    n_freq = RET_HEAD_DIM // 4
    inv = ROPE_BASE ** (-jnp.arange(n_freq, dtype=F32) / n_freq)
    ang = jnp.concatenate([row[:, None] * inv, col[:, None] * inv], axis=-1)
    cos, sin = jnp.cos(ang), jnp.sin(ang)
    cos2 = jnp.concatenate([jnp.ones((n_ctx, RET_HEAD_DIM), F32), jnp.concatenate([cos, cos], axis=-1)], axis=0)
    sin2 = jnp.concatenate([jnp.zeros((n_ctx, RET_HEAD_DIM), F32), jnp.concatenate([-sin, sin], axis=-1)], axis=0)
    return cos2, sin2


VMEM_LIMIT = 56 * 1024 * 1024


def _const_spec(a):
    return pl.BlockSpec(a.shape, lambda b, i: (0,) * a.ndim, pipeline_mode=pl.Buffered(1))


def _mod_spec(d):
    return pl.BlockSpec((1, 1, 6, d), lambda b, i: (b, jnp.minimum(i, 1), 0, 0))


def _tok_spec(width, col=0):
    return pl.BlockSpec((1, ROW_TILE, width), lambda b, i: (b, i, col))


def _norm_mod(x, g, mod, shift_row, scale_row):
    y = x * lax.rsqrt(jnp.mean(x * x, axis=1, keepdims=True) + EPS) * g
    return y * (1.0 + mod[scale_row:scale_row + 1]) + mod[shift_row:shift_row + 1]


def _in_proj_kernel(x_ref, g_ref, mod_ref, w_ref, o_ref):
    n = _norm_mod(x_ref[0], g_ref[...], mod_ref[0, 0], 0, 1)
    o_ref[0] = _mm(n, w_ref[...])


def in_proj(x, norm_g, mods, w):
    bsz, length, d = x.shape
    n_out = w.shape[1]
    return pl.pallas_call(
        _in_proj_kernel,
        grid=(bsz, length // ROW_TILE),
        in_specs=[_tok_spec(d), _const_spec(norm_g), _mod_spec(d), _const_spec(w)],
        out_specs=_tok_spec(n_out),
        out_shape=jax.ShapeDtypeStruct((bsz, length, n_out), F32),
        compiler_params=pltpu.CompilerParams(dimension_semantics=("arbitrary", "arbitrary"),
                                             vmem_limit_bytes=VMEM_LIMIT),
        name="in_proj",
    )(x, norm_g, mods, w)


ADA_COLS = 1024


def _ada_kernel(c_ref, w_ref, b_ref, o_ref):
    cv = c_ref[...]
    o_ref[...] = _mm(cv * jax.nn.sigmoid(cv), w_ref[...]) + b_ref[...]


def ada_modulation(cond, w, b):
    rows, d = cond.shape
    n_out = w.shape[1]
    return pl.pallas_call(
        _ada_kernel,
        grid=(n_out // ADA_COLS,),
        in_specs=[pl.BlockSpec((rows, d), lambda j: (0, 0)), pl.BlockSpec((d, ADA_COLS), lambda j: (0, j)),
                  pl.BlockSpec((1, ADA_COLS), lambda j: (0, j))],
        out_specs=pl.BlockSpec((rows, ADA_COLS), lambda j: (0, j)),
        out_shape=jax.ShapeDtypeStruct((rows, n_out), F32),
        compiler_params=pltpu.CompilerParams(dimension_semantics=("arbitrary",)),
        name="ada_modulation",
    )(cond, w, b)


def _norm_mod_kernel(x_ref, g_ref, mod_ref, o_ref):
    o_ref[0] = _norm_mod(x_ref[0], g_ref[...], mod_ref[0, 0], 3, 4)


def norm_modulate2(x, norm_g, mods):
    bsz, length, d = x.shape
    return pl.pallas_call(
        _norm_mod_kernel,
        grid=(bsz, length // ROW_TILE),
        in_specs=[_tok_spec(d), _const_spec(norm_g), _mod_spec(d)],
        out_specs=_tok_spec(d),
        out_shape=jax.ShapeDtypeStruct((bsz, length, d), F32),
        compiler_params=pltpu.CompilerParams(dimension_semantics=("arbitrary", "arbitrary")),
        name="norm_modulate2",
    )(x, norm_g, mods)


def _residual_kernel(x_ref, f_ref, mod_ref, g_ref, o_ref, *, final_norm):
    y = x_ref[0] + mod_ref[0, 0][5:6] * f_ref[0]
    if final_norm:
        y = y * lax.rsqrt(jnp.mean(y * y, axis=1, keepdims=True) + EPS) * g_ref[...]
    o_ref[0] = y


def residual2(x, f, mods, final_g, final_norm):
    bsz, length, d = x.shape
    return pl.pallas_call(
        functools.partial(_residual_kernel, final_norm=final_norm),
        grid=(bsz, length // ROW_TILE),
        in_specs=[_tok_spec(d), _tok_spec(d), _mod_spec(d), _const_spec(final_g)],
        out_specs=_tok_spec(d),
        out_shape=jax.ShapeDtypeStruct((bsz, length, d), F32),
        compiler_params=pltpu.CompilerParams(dimension_semantics=("arbitrary", "arbitrary")),
        name="residual2",
    )(x, f, mods, final_g)


def _merge_kernel(x_ref, g_ref, mod_ref, ya_ref, yb_ref, yc_ref, yd_ref, wg_ref, bg_ref, wbr_ref, wout_ref, o_ref):
    x = x_ref[0]
    mod = mod_ref[0, 0]
    n = _norm_mod(x, g_ref[...], mod, 0, 1).astype(BF16)
    m = jnp.zeros(x.shape, F32)
    for i, y_ref in enumerate((ya_ref, yb_ref, yc_ref, yd_ref)):
        gate = jax.nn.sigmoid(_mm(n, wg_ref[i]) + bg_ref[i:i + 1])
        m = m + gate * _mm(y_ref[0], wbr_ref[i])
    o_ref[0] = x + mod[2:3] * _mm(m, wout_ref[...])


def merge_residual(x, norm_g, mods, ys, w_merge, b_merge, w_branch, w_out):
    bsz, length, d = x.shape
    bw = ys[0].shape[-1]
    return pl.pallas_call(
        _merge_kernel,
        grid=(bsz, length // ROW_TILE),
        in_specs=[_tok_spec(d), _const_spec(norm_g), _mod_spec(d)] + [_tok_spec(bw)] * 4
        + [_const_spec(w_merge), _const_spec(b_merge), _const_spec(w_branch), _const_spec(w_out)],
        out_specs=_tok_spec(d),
        out_shape=jax.ShapeDtypeStruct((bsz, length, d), F32),
        compiler_params=pltpu.CompilerParams(dimension_semantics=("arbitrary", "arbitrary"),
                                             vmem_limit_bytes=VMEM_LIMIT),
        name="merge_residual",
    )(x, norm_g, mods, *ys, w_merge, b_merge, w_branch, w_out)


def _halo_specs(width, n_tiles):
    prev = pl.BlockSpec((1, ROW_TILE, width), lambda b, i: (b, jnp.maximum(i - 1, 0), 0))
    nxt = pl.BlockSpec((1, ROW_TILE, width), lambda b, i: (b, jnp.minimum(i + 1, n_tiles - 1), 0))
    return [prev, _tok_spec(width), nxt]


def _segment_edges(n_tiles):
    i = pl.program_id(1)
    return i >= 2, jnp.logical_and(i >= 1, i < n_tiles - 1)


CONV_CHANNELS = 512
CONV_TAPS = 31
HALO = 16


def _conformer_kernel(prev_ref, cur_ref, next_ref, dw_ref, db_ref, lng_ref, lnb_ref, o_ref, ext_ref, *, n_tiles):
    ch = CONV_CHANNELS
    glu = lambda p: p[:, :ch] * jax.nn.sigmoid(p[:, ch:])
    has_prev, has_next = _segment_edges(n_tiles)
    ext_ref[0:HALO, :] = jnp.where(has_prev, glu(prev_ref[0, ROW_TILE - HALO:, :]), 0.0)
    ext_ref[HALO:HALO + ROW_TILE, :] = glu(cur_ref[0])
    ext_ref[HALO + ROW_TILE:, :] = jnp.where(has_next, glu(next_ref[0, :HALO, :]), 0.0)
    pad = (CONV_TAPS - 1) // 2
    acc = jnp.zeros((ROW_TILE, ch), F32) + db_ref[...]
    for j in range(CONV_TAPS):
        acc = acc + ext_ref[pl.ds(HALO - pad + j, ROW_TILE), :] * dw_ref[j:j + 1, :]
    zc = acc - jnp.mean(acc, axis=1, keepdims=True)
    z = zc * lax.rsqrt(jnp.mean(zc * zc, axis=1, keepdims=True) + EPS) * lng_ref[...] + lnb_ref[...]
    o_ref[0] = z * jax.nn.sigmoid(z)


def conformer_conv(p, dw, db, ln_g, ln_b):
    bsz, length, width = p.shape
    n_tiles = length // ROW_TILE
    ch = CONV_CHANNELS
    row = lambda v: v.reshape(1, ch)
    return pl.pallas_call(
        functools.partial(_conformer_kernel, n_tiles=n_tiles),
        grid=(bsz, n_tiles),
        in_specs=_halo_specs(width, n_tiles) + [_const_spec(dw)] + [_const_spec(row(db))] * 3,
        out_specs=_tok_spec(ch),
        out_shape=jax.ShapeDtypeStruct((bsz, length, ch), F32),
        scratch_shapes=[pltpu.VMEM((ROW_TILE + 2 * HALO, ch), F32)],
        compiler_params=pltpu.CompilerParams(dimension_semantics=("arbitrary", "arbitrary"),
                                             vmem_limit_bytes=VMEM_LIMIT),
        name="conformer_conv",
    )(p, p, p, dw, row(db), row(ln_g), row(ln_b))


def _s5_out_kernel(u_ref, yf_ref, yb_ref, d_ref, w_ref, b_ref, o_ref):
    z = _gelu_tanh(d_ref[...] * u_ref[0] + yf_ref[0] + yb_ref[0])
    o_ref[0] = z * jax.nn.sigmoid(_mm(z, w_ref[...]) + b_ref[...])


def s5_out(u, y_fwd, y_bwd, d_skip, glu_w, glu_b):
    bsz, length, w = u.shape
    return pl.pallas_call(
        _s5_out_kernel,
        grid=(bsz, length // ROW_TILE),
        in_specs=[_tok_spec(w)] * 3 + [_const_spec(d_skip), _const_spec(glu_w), _const_spec(glu_b)],
        out_specs=_tok_spec(w),
        out_shape=jax.ShapeDtypeStruct((bsz, length, w), F32),
        compiler_params=pltpu.CompilerParams(dimension_semantics=("arbitrary", "arbitrary")),
        name="s5_out",
    )(u, y_fwd, y_bwd, d_skip, glu_w, glu_b)


def _head_sums(z, ones_bd):
    hi = z.astype(BF16)
    lo = (z - hi.astype(F32)).astype(BF16)
    return _mm(hi, ones_bd) + _mm(lo, ones_bd)


def _rwkv_prep_kernel(prev_ref, cur_ref, next_ref, shift_ref, w2_ref, a2_ref, g2_ref, vec_ref, bd_ref,
                      r_ref, v_ref, kk_ref, gb_ref, k_ref, b_ref, lw_ref, ext_ref, *, n_tiles):
    w = RWKV_WIDTH
    has_prev, has_next = _segment_edges(n_tiles)
    ext_ref[0:8, :] = jnp.where(has_prev, prev_ref[0, ROW_TILE - 8:, :], 0.0)
    ext_ref[8:8 + ROW_TILE, :] = cur_ref[0]
    ext_ref[8 + ROW_TILE:, :] = jnp.where(has_next, next_ref[0, :8, :], 0.0)
    p = (ext_ref[pl.ds(7, ROW_TILE), :] * shift_ref[0:1, :] + ext_ref[pl.ds(8, ROW_TILE), :] * shift_ref[1:2, :]
         + ext_ref[pl.ds(9, ROW_TILE), :] * shift_ref[2:3, :])
    r, k, v = p[:, :w], p[:, w:2 * w], p[:, 2 * w:3 * w]
    lowrank_w = jnp.tanh(p[:, 3 * w:3 * w + LANES])
    lowrank_a = p[:, 3 * w + LANES:3 * w + 2 * LANES]
    gl = p[:, 3 * w + 2 * LANES:]
    ones_bd = bd_ref[...]
    kk = k * vec_ref[0:1, :]
    kk = kk * lax.rsqrt(_head_sums(kk * kk, ones_bd) + EPS)
    r_ref[0], v_ref[0], kk_ref[0] = r, v, kk
    gb_ref[0, 0] = _mm(jax.nn.sigmoid(gl), g2_ref[...])
    k_sum = jnp.zeros_like(k)
    for d in range(2):
        w_log = -jax.nn.softplus(-(vec_ref[4 + d:5 + d, :] + _mm(lowrank_w, w2_ref[d]))) - 0.5
        a = jax.nn.sigmoid(vec_ref[6 + d:7 + d, :] + _mm(lowrank_a, a2_ref[d]))
        k_d = k * (1.0 + (a - 1.0) * vec_ref[1:2, :])
        k_ref[0, d], b_ref[0, d], lw_ref[0, d] = k_d, a * kk, -jnp.exp(w_log)
        k_sum = k_sum + k_d
    gb_ref[0, 1] = _head_sums(r * k_sum * vec_ref[2:3, :], ones_bd) * v


def rwkv_prep(p, shift_w, w2, a2, g2, vecs, ones_bd):
    bsz, length, width = p.shape
    n_tiles = length // ROW_TILE
    w = RWKV_WIDTH
    shared = jax.ShapeDtypeStruct((bsz, length, w), F32)
    per_dir = jax.ShapeDtypeStruct((bsz, 2, length, w), F32)
    dir_spec = pl.BlockSpec((1, 2, ROW_TILE, w), lambda b, i: (b, 0, i, 0))
    return pl.pallas_call(
        functools.partial(_rwkv_prep_kernel, n_tiles=n_tiles),
        grid=(bsz, n_tiles),
        in_specs=_halo_specs(width, n_tiles) + [_const_spec(a) for a in (shift_w, w2, a2, g2, vecs, ones_bd)],
        out_specs=[_tok_spec(w)] * 3 + [dir_spec] * 4,
        out_shape=[shared] * 3 + [per_dir] * 4,
        scratch_shapes=[pltpu.VMEM((ROW_TILE + 16, width), F32)],
        compiler_params=pltpu.CompilerParams(dimension_semantics=("arbitrary", "arbitrary"),
                                             vmem_limit_bytes=VMEM_LIMIT),
        name="rwkv_prep",
    )(p, p, p, shift_w, w2, a2, g2, vecs, ones_bd)


def _rwkv_out_kernel(y_ref, gb_ref, gn_ref, bd_ref, o_ref):
    y = y_ref[0, 0] + y_ref[0, 1]
    ones_bd = bd_ref[...]
    inv = 1.0 / RWKV_HEAD_DIM
    yc = y - _head_sums(y, ones_bd) * inv
    yn = yc * lax.rsqrt(_head_sums(yc * yc, ones_bd) * inv + EPS)
    o_ref[0] = (yn * gn_ref[0:1, :] + gn_ref[1:2, :] + gb_ref[0, 1]) * gb_ref[0, 0]


def rwkv_out(y, gate_bonus, gn, ones_bd):
    bsz, _, length, w = y.shape
    dir_spec = pl.BlockSpec((1, 2, ROW_TILE, w), lambda b, i: (b, 0, i, 0))
    return pl.pallas_call(
        _rwkv_out_kernel,
        grid=(bsz, length // ROW_TILE),
        in_specs=[dir_spec, dir_spec, _const_spec(gn), _const_spec(ones_bd)],
        out_specs=_tok_spec(w),
        out_shape=jax.ShapeDtypeStruct((bsz, length, w), F32),
        compiler_params=pltpu.CompilerParams(dimension_semantics=("arbitrary", "arbitrary")),
        name="rwkv_out",
    )(y, gate_bonus, gn, ones_bd)


SC_CORES = 2
SC_SUBCORES = 16
SC_WORKERS = SC_CORES * SC_SUBCORES
GATHER_ROWS = 64
PEER_SLOTS = 128
PEER_TOKENS_PER_STEP = 16


PEER_HEADS = 8
PEER_KEYS = 128
PEER_TOPK = 16
PEER_HALF = 128
PEER_SELECT_TOKENS = 128


def _top_rows(s, payload=None):
    n_rows = s.shape[0]
    iota = lax.broadcasted_iota(jnp.int32, s.shape, 0)
    vals, picks = [], []
    for _ in range(PEER_TOPK):
        m = jnp.max(s, axis=0, keepdims=True)
        pos = jnp.min(jnp.where(s == m, iota, n_rows), axis=0, keepdims=True)
        hit = iota == pos
        vals.append(m)
        picks.append(pos if payload is None else jnp.max(jnp.where(hit, payload, -1), axis=0, keepdims=True))
        s = jnp.where(hit, -jnp.inf, s)
    return jnp.concatenate(vals, axis=0), jnp.concatenate(picks, axis=0)


def _peer_select_kernel(h_ref, wq_ref, keys_ref, ids_ref, gate_ref):
    q = _mm(h_ref[...], wq_ref[...]).astype(BF16)
    for h in range(PEER_HEADS):
        halves = []
        for p in range(2):
            lst = 2 * h + p
            s = _mm_nt(keys_ref[lst], q[:, lst * PEER_HALF:(lst + 1) * PEER_HALF])
            halves.append(_top_rows(s))
        (v1, p1), (v2, p2) = halves
        cand = jnp.concatenate([v1[i:i + 1] + v2 for i in range(PEER_TOPK)], axis=0)
        cand_id = jnp.concatenate([p1[i:i + 1] * PEER_KEYS + p2 for i in range(PEER_TOPK)], axis=0)
        best, ids = _top_rows(cand, cand_id)
        e = jnp.exp(best - best[0:1])
        ids_ref[h * PEER_TOPK:(h + 1) * PEER_TOPK, :] = ids
        gate_ref[h * PEER_TOPK:(h + 1) * PEER_TOPK, :] = e / jnp.sum(e, axis=0, keepdims=True)


def peer_select(h, wq, keys):
    n, d = h.shape
    tn = PEER_SELECT_TOKENS
    slots = PEER_HEADS * PEER_TOPK
    full = lambda a: pl.BlockSpec(a.shape, lambda i: (0,) * a.ndim)
    out = pl.BlockSpec((slots, tn), lambda i: (0, i))
    return pl.pallas_call(
        _peer_select_kernel,
        grid=(n // tn,),
        in_specs=[pl.BlockSpec((tn, d), lambda i: (i, 0)), full(wq), full(keys)],
        out_specs=[out, out],
        out_shape=[jax.ShapeDtypeStruct((slots, n), jnp.int32), jax.ShapeDtypeStruct((slots, n), F32)],
        compiler_params=pltpu.CompilerParams(dimension_semantics=("arbitrary",),
                                             vmem_limit_bytes=48 * 1024 * 1024),
        name="peer_select",
    )(h, wq, keys)


def pack_bf16_pairs(table):
    half = table.shape[1] // 2
    bits = lax.bitcast_convert_type(table.astype(BF16), jnp.uint16).astype(jnp.uint32)
    return bits[:, :half] | (bits[:, half:] << 16)


def sc_gather_rows(table, idx):
    n_rows, width = idx.shape[0], table.shape[1]
    per_worker = n_rows // SC_WORKERS
    n_pairs = per_worker // (2 * GATHER_ROWS)
    assert per_worker * SC_WORKERS == n_rows and n_pairs * 2 * GATHER_ROWS == per_worker
    mesh = plsc.VectorSubcoreMesh(core_axis_name="c", subcore_axis_name="s",
                                  num_cores=SC_CORES, num_subcores=SC_SUBCORES)

    @functools.partial(
        pl.kernel, mesh=mesh,
        out_type=jax.ShapeDtypeStruct((n_rows, width), table.dtype),
        scratch_types=[pltpu.VMEM((2, GATHER_ROWS), jnp.int32),
                       pltpu.VMEM((2, GATHER_ROWS, width), table.dtype),
                       pltpu.SemaphoreType.DMA((2,)),
                       pltpu.SemaphoreType.DMA((2,))],
        name="peer_sc_gather",
    )
    def gather(table_hbm, idx_hbm, out_hbm, idx_v, rows_v, gather_sem, write_sem):
        worker = lax.axis_index("s") * SC_CORES + lax.axis_index("c")
        base = worker * per_worker

        def rows_of(chunk):
            return pl.ds(pl.multiple_of(base + chunk * GATHER_ROWS, GATHER_ROWS), GATHER_ROWS)

        def gather_copy(slot):
            return pltpu.make_async_copy(table_hbm.at[idx_v.at[slot]], rows_v.at[slot], gather_sem.at[slot])

        def write_copy(chunk, slot):
            return pltpu.make_async_copy(rows_v.at[slot], out_hbm.at[rows_of(chunk)], write_sem.at[slot])

        def start_gather(chunk, slot):
            pltpu.sync_copy(idx_hbm.at[rows_of(chunk)], idx_v.at[slot])
            gather_copy(slot).start()

        start_gather(0, 0)

        @pl.loop(0, n_pairs)
        def _(g):
            even, odd = 2 * g, 2 * g + 1

            @pl.when(g > 0)
            def _():
                write_copy(odd - 2, 1).wait()
            start_gather(odd, 1)
            gather_copy(0).wait()
            write_copy(even, 0).start()
            gather_copy(1).wait()
            write_copy(odd, 1).start()
            write_copy(even, 0).wait()

            @pl.when(g + 1 < n_pairs)
            def _():
                start_gather(even + 2, 0)

        write_copy(2 * n_pairs - 1, 1).wait()

    return gather(table, idx)


def _unpack_pairs(words):
    lo = pltpu.bitcast(words << 16, F32)
    hi = pltpu.bitcast(words & jnp.uint32(0xFFFF0000), F32)
    return lo, hi


def _gelu_tanh(x):
    return 0.5 * x * (1.0 + jnp.tanh(0.7978845608028654 * (x + 0.044715 * (x * x * x))))


def _peer_expert_kernel(z_ref, gate_ref, ug_ref, vg_ref, o_ref):
    half = ug_ref.shape[2]
    gate_t = gate_ref[...].T
    for n in range(PEER_TOKENS_PER_STEP):
        z_lo, z_hi = z_ref[n:n + 1, :half], z_ref[n:n + 1, half:]
        u_lo, u_hi = _unpack_pairs(ug_ref[n])
        prod = u_lo * z_lo + u_hi * z_hi
        act = jnp.sum(prod, axis=1, keepdims=True)
        w = _gelu_tanh(act) * gate_t[:, n:n + 1]
        v_lo, v_hi = _unpack_pairs(vg_ref[n])
        o_ref[n:n + 1, :half] = jnp.sum(w * v_lo, axis=0, keepdims=True)
        o_ref[n:n + 1, half:] = jnp.sum(w * v_hi, axis=0, keepdims=True)


def peer_experts(z, gate, ug, vg):
    n, d = z.shape
    tn = PEER_TOKENS_PER_STEP
    slots, half = ug.shape[1], ug.shape[2]
    return pl.pallas_call(
        _peer_expert_kernel,
        grid=(n // tn,),
        in_specs=[pl.BlockSpec((tn, d), lambda i: (i, 0)),
                  pl.BlockSpec((tn, slots), lambda i: (i, 0)),
                  pl.BlockSpec((tn, slots, half), lambda i: (i, 0, 0)),
                  pl.BlockSpec((tn, slots, half), lambda i: (i, 0, 0))],
        out_specs=pl.BlockSpec((tn, d), lambda i: (i, 0)),
        out_shape=jax.ShapeDtypeStruct((n, d), F32),
        compiler_params=pltpu.CompilerParams(dimension_semantics=("arbitrary",),
                                             vmem_limit_bytes=48 * 1024 * 1024),
        name="peer_experts",
    )(z, gate, ug, vg)


CONV_CHANNELS = 512
CONV_TAPS = 31
S5_WIDTH = 512
S5_GROUP = 16
S5_GROUPS = S5_WIDTH // S5_GROUP
S5_STATE = 64
S5_MAX_RE = -1e-4
RET_HEADS = 4
RET_HEAD_DIM = 128
RET_WIDTH = RET_HEADS * RET_HEAD_DIM
RET_CHUNK = 128
ROPE_BASE = 10000.0
N_BRANCH = 4
PEER_HEADS = 8
PEER_KEYS = 128
PEER_TOPK = 16
PEER_QUERY = 256
PEER_HALF = PEER_QUERY // 2
PEER_BLOCK = 128
SHIFT_TAPS = 3
RWKV_IN = 3 * RWKV_WIDTH + 2 * RWKV_DECAY_RANK + 2 * RWKV_ICLR_RANK + RWKV_GATE_RANK
CONV_IN = 2 * CONV_CHANNELS
S5_IN = S5_WIDTH
RET_IN = 5 * RET_WIDTH
A_END = RWKV_IN
B_END = A_END + CONV_IN
C_END = B_END + S5_IN
IN_WIDTH = C_END + RET_IN
RWKV_SPLITS = (RWKV_WIDTH, 2 * RWKV_WIDTH, 3 * RWKV_WIDTH,
               3 * RWKV_WIDTH + RWKV_DECAY_RANK, 3 * RWKV_WIDTH + 2 * RWKV_DECAY_RANK,
               3 * RWKV_WIDTH + 2 * RWKV_DECAY_RANK + RWKV_ICLR_RANK,
               3 * RWKV_WIDTH + 2 * RWKV_DECAY_RANK + 2 * RWKV_ICLR_RANK)


def _rms_norm(z, g):
    zf = z.astype(F32)
    y = zf * lax.rsqrt(jnp.mean(zf * zf, axis=-1, keepdims=True) + EPS)
    return (y * g).astype(z.dtype)


def _normalise(z):
    zf = z.astype(F32)
    zc = zf - jnp.mean(zf, axis=-1, keepdims=True)
    return zc * lax.rsqrt(jnp.mean(zc * zc, axis=-1, keepdims=True) + EPS)


def _modulate(z, shift, scale):
    return z * (1.0 + scale) + shift


def _depthwise_conv(z, w):
    taps = w.shape[0]
    pad = (taps - 1) // 2
    return lax.conv_general_dilated(
        z, w[:, None, :].astype(z.dtype), window_strides=(1,), padding=[(pad, pad)],
        dimension_numbers=("NWC", "WIO", "NWC"), feature_group_count=z.shape[-1])


def _axial_rope(n_tokens):
    rows = n_tokens // GRID_W
    row = jnp.repeat(jnp.arange(rows, dtype=F32), GRID_W)
    col = jnp.tile(jnp.arange(GRID_W, dtype=F32), rows)
    n_freq = RET_HEAD_DIM // 4
    inv = ROPE_BASE ** (-jnp.arange(n_freq, dtype=F32) / n_freq)
    ang = jnp.concatenate([row[:, None] * inv, col[:, None] * inv], axis=-1)
    return jnp.cos(ang), jnp.sin(ang)


def _apply_rope(z, cos, sin):
    z1, z2 = jnp.split(z, 2, axis=-1)
    cs, sn = cos[None, :, None, :], sin[None, :, None, :]
    return jnp.concatenate([z1 * cs - z2 * sn, z1 * sn + z2 * cs], axis=-1)


def _rwkv_prep(p, shift_w, w0, w2, a0, a2, g2, k_k, k_a):
    p = _depthwise_conv(p, shift_w)
    r, k, v, wl_f, wl_b, al_f, al_b, gl = jnp.split(p, RWKV_SPLITS, axis=-1)
    b, t, _ = r.shape
    kk = (k * k_k).reshape(b, t, RWKV_HEADS, RWKV_HEAD_DIM)
    kk = (kk * lax.rsqrt(jnp.sum(kk * kk, axis=-1, keepdims=True) + EPS)).reshape(b, t, RWKV_WIDTH)
    g = jax.nn.sigmoid(gl) @ g2
    ks, bs, lws = [], [], []
    for d, (wl, al) in enumerate(((wl_f, al_f), (wl_b, al_b))):
        w_log = -jax.nn.softplus(-(w0[d] + jnp.tanh(wl) @ w2[d])) - 0.5
        a = jax.nn.sigmoid(a0[d] + al @ a2[d])
        ks.append(k * (1.0 + (a - 1.0) * k_a))
        bs.append(a * kk)
        lws.append(-jnp.exp(w_log))
    return r, v, kk, g, jnp.stack(ks, 1), jnp.stack(bs, 1), jnp.stack(lws, 1)


def _rwkv_mixer(p_lat, p_ctx, shift_w, w0, w2, a0, a2, g2, k_k, k_a, r_k, gn_g, gn_b):
    n_ctx = p_ctx.shape[1]
    prep_l = _rwkv_prep(p_lat, shift_w, w0, w2, a0, a2, g2, k_k, k_a)
    prep_c = _rwkv_prep(p_ctx, shift_w, w0, w2, a0, a2, g2, k_k, k_a)
    r, v, kk, g = (jnp.concatenate([c, l], axis=1) for c, l in zip(prep_c[:4], prep_l[:4]))
    k_dir, b_dir, lw_dir = (jnp.concatenate([c, l], axis=2) for c, l in zip(prep_c[4:], prep_l[4:]))
    y = rwkv_scan(r, v, kk, k_dir, b_dir, lw_dir, n_ctx)
    y = y[:, 0] + y[:, 1]
    b, t, _ = y.shape
    heads = lambda z: z.reshape(b, t, RWKV_HEADS, RWKV_HEAD_DIM)
    k_sum = k_dir[:, 0] + k_dir[:, 1]
    bonus = jnp.sum(heads(r) * heads(k_sum) * r_k, axis=-1, keepdims=True) * heads(v)
    o = _normalise(heads(y)).reshape(b, t, RWKV_WIDTH) * gn_g + gn_b + bonus.reshape(b, t, RWKV_WIDTH)
    o = o * g
    return o[:, n_ctx:], o[:, :n_ctx]


def _conformer_conv(p, dw, db, ln_g, ln_b):
    val, gate = jnp.split(p, 2, axis=-1)
    z = val * jax.nn.sigmoid(gate)
    z = _depthwise_conv(z, dw) + db
    z = _normalise(z) * ln_g + ln_b
    return jax.nn.silu(z).astype(p.dtype)


def _s5_discretise(lam_re, lam_im, log_dt, b_re, b_im):
    lam_re = jnp.minimum(lam_re.astype(F32), S5_MAX_RE)
    lam_im = lam_im.astype(F32)
    dt = jnp.exp(log_dt.astype(F32))[:, None]
    mag = jnp.exp(lam_re * dt)
    ang = lam_im * dt
    ab_re, ab_im = mag * jnp.cos(ang), mag * jnp.sin(ang)
    den = lam_re * lam_re + lam_im * lam_im
    nr, ni = ab_re - 1.0, ab_im
    f_re = (nr * lam_re + ni * lam_im) / den
    f_im = (ni * lam_re - nr * lam_im) / den
    b_re, b_im = b_re.astype(F32), b_im.astype(F32)
    bb_re = f_re[..., None] * b_re - f_im[..., None] * b_im
    bb_im = f_re[..., None] * b_im + f_im[..., None] * b_re
    return ab_re, ab_im, bb_re, bb_im


def _s5_mixer(u_lat, u_ctx, lam_re, lam_im, log_dt, b_re, b_im, c_re, c_im, d_skip, glu_w, glu_b):
    n_ctx = u_ctx.shape[1]
    u = jnp.concatenate([u_ctx, u_lat], axis=1).astype(F32)
    eye = jnp.eye(S5_GROUPS, dtype=F32)
    n_state = S5_GROUPS * S5_STATE
    y = d_skip * u
    for d in range(2):
        ab_re, ab_im, bb_re, bb_im = _s5_discretise(lam_re[d], lam_im[d], log_dt[d], b_re[d], b_im[d])
        blk_in = lambda bb: jnp.einsum("gph,gk->ghkp", bb, eye).reshape(S5_WIDTH, n_state).astype(BF16)
        blk_out = lambda cc: jnp.einsum("ghp,gk->kpgh", cc.astype(F32), eye).reshape(n_state, S5_WIDTH)
        c_cat = jnp.concatenate([blk_out(c_re[d]), -blk_out(c_im[d])], axis=0).astype(BF16)
        y = y + s5_scan(u, blk_in(bb_re), blk_in(bb_im), c_cat, ab_re.reshape(1, n_state),
                        ab_im.reshape(1, n_state), n_ctx, reverse=(d == 1))
    z = jax.nn.gelu(y)
    out = z * jax.nn.sigmoid(z @ glu_w + glu_b)
    return out[:, n_ctx:], out[:, :n_ctx]


def _retention_chunkwise(q, k, v, r0, log_gamma, strict, emit):
    b, t, h, _ = q.shape
    n_chunks = t // RET_CHUNK
    chunks = lambda z: z.reshape(b, n_chunks, RET_CHUNK, h, z.shape[-1]).transpose(1, 0, 3, 2, 4)
    pos = jnp.arange(RET_CHUNK, dtype=F32)
    diff = pos[:, None] - pos[None, :]
    keep = diff > 0 if strict else diff >= 0
    decay_in = jnp.where(keep, jnp.exp(jnp.where(keep, diff, 0.0) * log_gamma[:, None, None]), 0.0)
    decay_q = jnp.exp((pos + 1.0) * log_gamma[:, None])[None, :, :, None]
    decay_k = jnp.exp((RET_CHUNK - 1.0 - pos) * log_gamma[:, None])[None, :, :, None]
    decay_chunk = jnp.exp(RET_CHUNK * log_gamma)[None, :, None, None]

    def step(r, inp):
        qc, kc, vc = inp
        r_next = decay_chunk * r + jnp.einsum("bhck,bhcv->bhkv", kc * decay_k, vc)
        if not emit:
            return r_next, None
        scores = jnp.einsum("bhnk,bhmk->bhnm", qc, kc) * decay_in
        o = jnp.einsum("bhnm,bhmv->bhnv", scores, vc) + jnp.einsum("bhnk,bhkv->bhnv", qc * decay_q, r)
        return r_next, o
    r_final, o = lax.scan(step, r0, (chunks(q), chunks(k), chunks(v)))
    if emit:
        o = o.transpose(1, 0, 3, 2, 4).reshape(b, t, h, v.shape[-1])
    return o, r_final


def _retention_mixer(p_lat, p_ctx, gn_g, gn_b, rope_cos, rope_sin):
    log_gamma = jnp.log1p(-jnp.exp2(-5.0 - jnp.arange(RET_HEADS, dtype=F32)))
    k_scale = RET_HEAD_DIM ** -0.5

    def split_heads(p):
        b, t, _ = p.shape
        q, kf, kb, v, g = jnp.split(p.astype(F32), 5, axis=-1)
        hd = lambda z: z.reshape(b, t, RET_HEADS, RET_HEAD_DIM)
        return hd(q), hd(kf) * k_scale, hd(kb) * k_scale, hd(v), g
    ql, kfl, kbl, vl, gl = split_heads(p_lat)
    ql, kfl, kbl = (_apply_rope(z, rope_cos, rope_sin) for z in (ql, kfl, kbl))
    qc, kfc, kbc, vc, gc = split_heads(p_ctx)
    r_zero = jnp.zeros((p_ctx.shape[0], RET_HEADS, RET_HEAD_DIM, RET_HEAD_DIM), F32)
    flip = lambda z: jnp.flip(z, axis=1)
    oc_f, rc_f = _retention_chunkwise(qc, kfc, vc, r_zero, log_gamma, False, True)
    oc_b, rc_b = _retention_chunkwise(flip(qc), flip(kbc), flip(vc), r_zero, log_gamma, True, True)
    ol_f, _ = _retention_chunkwise(ql, kfl, vl, rc_f, log_gamma, False, True)
    ol_b, _ = _retention_chunkwise(flip(ql), flip(kbl), flip(vl), rc_b, log_gamma, True, True)

    def readout(o, g):
        b, t = o.shape[:2]
        return (_normalise(o).reshape(b, t, RET_WIDTH) * gn_g + gn_b) * jax.nn.silu(g)
    return readout(ol_f + flip(ol_b), gl), readout(oc_f + flip(oc_b), gc)


def _merge_branches(n, ys, w_branch, w_merge, b_merge, w_out):
    m = 0.0
    for i, y in enumerate(ys):
        gate = jax.nn.sigmoid(n @ w_merge[i] + b_merge[i])
        m = m + gate * (y.astype(n.dtype) @ w_branch[i])
    return m @ w_out


def _peer_ffn(h, w_q, sub_keys, u_packed, v_packed):
    n, d = h.shape
    keys = sub_keys.reshape(2 * PEER_HEADS, PEER_KEYS, PEER_HALF).astype(BF16)
    ids_t, gate_t = peer_select(h, w_q.astype(BF16), keys)
    ids, gate = ids_t.T, gate_t.T
    step = 2 * SC_WORKERS * GATHER_ROWS // PEER_SLOTS
    n_blk = next(k for k in (8, 4, 2, 1) if n % (k * step) == 0)
    nb = n // n_blk
    outs = []
    for i in range(n_blk):
        rows = slice(i * nb, (i + 1) * nb)
        flat_ids = ids[rows].reshape(-1)
        ug = sc_gather_rows(u_packed, flat_ids).reshape(nb, PEER_SLOTS, d // 2)
        vg = sc_gather_rows(v_packed, flat_ids).reshape(nb, PEER_SLOTS, d // 2)
        outs.append(peer_experts(h[rows], gate[rows], ug, vg))
    return jnp.concatenate(outs, axis=0)


def kernel(x, c, ctx, c_ctx, ada_w, ada_b, norm1_g, norm2_g, w_in, rwkv_shift, rwkv_w0, rwkv_w2,
           rwkv_a0, rwkv_a2, rwkv_g2, rwkv_kk, rwkv_ka, rwkv_rk, rwkv_gn_g, rwkv_gn_b,
           conv_dw, conv_db, conv_ln_g, conv_ln_b, s5_lam_re, s5_lam_im, s5_log_dt,
           s5_b_re, s5_b_im, s5_c_re, s5_c_im, s5_d, s5_glu_w, s5_glu_b, ret_gn_g, ret_gn_b,
           w_branch, w_merge, b_merge, w_out, peer_wq, peer_keys, peer_u, peer_v, final_g):
    depth = ada_w.shape[0]
    bsz, n_lat, d = x.shape
    n_ctx = ctx.shape[1]
    assert n_ctx == ROW_TILE and n_lat % ROW_TILE == 0
    xa = jnp.concatenate([ctx, x], axis=1)
    cos2, sin2 = rope_tables(n_lat, n_ctx)
    ones_bd = jnp.kron(jnp.eye(RWKV_HEADS, dtype=F32), jnp.ones((RWKV_HEAD_DIM, RWKV_HEAD_DIM), F32)).astype(BF16)
    cond = jnp.zeros((8, d), F32).at[:bsz].set(c).at[bsz].set(c_ctx)
    row = lambda v: v.reshape(1, -1)
    w = RWKV_WIDTH
    for l in range(depth):
        mod = ada_modulation(cond, ada_w[l], row(ada_b[l])).reshape(8, 6, d)
        mods = jnp.stack([jnp.broadcast_to(mod[bsz], (bsz, 6, d)), mod[:bsz]], axis=1)
        g1 = row(norm1_g[l])
        w_in_l = w_in[l].astype(BF16)
        p_a, p_b, p_c, p_d = (in_proj(xa, g1, mods, w_in_l[:, lo:hi])
                              for lo, hi in ((0, A_END), (A_END, B_END), (B_END, C_END), (C_END, IN_WIDTH)))
        half = jnp.zeros((RWKV_DECAY_RANK, w), F32)
        w2 = jnp.stack([jnp.concatenate([rwkv_w2[l, 0], half]), jnp.concatenate([half, rwkv_w2[l, 1]])]).astype(BF16)
        a2 = jnp.stack([jnp.concatenate([rwkv_a2[l, 0], half]), jnp.concatenate([half, rwkv_a2[l, 1]])]).astype(BF16)
        vecs = jnp.stack([rwkv_kk[l], rwkv_ka[l], rwkv_rk[l].reshape(w), jnp.zeros((w,), F32),
                          rwkv_w0[l, 0], rwkv_w0[l, 1], rwkv_a0[l, 0], rwkv_a0[l, 1]])
        r, v, kk, gate_bonus, k_dir, b_dir, lw_dir = rwkv_prep(
            p_a, rwkv_shift[l], w2, a2, rwkv_g2[l].astype(BF16), vecs, ones_bd)
        y = rwkv_scan(r, v, kk, k_dir, b_dir, lw_dir, n_ctx)
        ya = rwkv_out(y, gate_bonus, jnp.stack([rwkv_gn_g[l], rwkv_gn_b[l]]), ones_bd)
        yb = conformer_conv(p_b, conv_dw[l], conv_db[l], conv_ln_g[l], conv_ln_b[l])
        eye = jnp.eye(S5_GROUPS, dtype=F32)
        n_state = S5_GROUPS * S5_STATE
        y_dirs = []
        for dr in range(2):
            ab_re, ab_im, bb_re, bb_im = _s5_discretise(s5_lam_re[l, dr], s5_lam_im[l, dr], s5_log_dt[l, dr],
                                                        s5_b_re[l, dr], s5_b_im[l, dr])
            blk_in = lambda bb: jnp.einsum("gph,gk->ghkp", bb, eye).reshape(S5_WIDTH, n_state).astype(BF16)
            blk_out = lambda cc: jnp.einsum("ghp,gk->kpgh", cc.astype(F32), eye).reshape(n_state, S5_WIDTH)
            c_cat = jnp.concatenate([blk_out(s5_c_re[l, dr]), -blk_out(s5_c_im[l, dr])], axis=0).astype(BF16)
            y_dirs.append(s5_scan(p_c, blk_in(bb_re), blk_in(bb_im), c_cat, ab_re.reshape(1, n_state),
                                  ab_im.reshape(1, n_state), n_ctx, reverse=(dr == 1)))
        yc = s5_out(p_c, y_dirs[0], y_dirs[1], row(s5_d[l]), s5_glu_w[l].astype(BF16), row(s5_glu_b[l]))
        yd = retention_out(retention_scan(p_d, cos2, sin2, n_ctx), p_d, ret_gn_g[l], ret_gn_b[l])
        xa = merge_residual(xa, g1, mods, (ya, yb, yc, yd), w_merge[l].astype(BF16), b_merge[l],
                            w_branch[l].astype(BF16), w_out[l].astype(BF16))
        h = norm_modulate2(xa, row(norm2_g[l]), mods)
        f = _peer_ffn(h.reshape(-1, d), peer_wq[l], peer_keys[l],
                      pack_bf16_pairs(peer_u[l]), pack_bf16_pairs(peer_v[l])).reshape(bsz, -1, d)
        xa = residual2(xa, f, mods, row(final_g), final_norm=(l == depth - 1))
    return xa[:, n_ctx:]
```

```python
import functools
import math

import jax
import jax.numpy as jnp
from jax import lax
from jax.experimental import pallas as pl
from jax.experimental.pallas import tpu as pltpu
from jax.experimental.pallas import tpu_sc as plsc

F32 = jnp.float32
BF16 = jnp.bfloat16

D_MODEL = 1024
GRID_W = 64
EPS = 1e-6

RWKV_HEADS = 8
RWKV_HEAD_DIM = 64
RWKV_WIDTH = RWKV_HEADS * RWKV_HEAD_DIM
RWKV_DECAY_RANK = 64
RWKV_ICLR_RANK = 64
RWKV_GATE_RANK = 128
RWKV_CHUNK = 64
LANES = 128


def _bdot(a, b, dims):
    return lax.dot_general(a.astype(BF16), b.astype(BF16), (dims, ((), ())), preferred_element_type=F32)


def _mm(a, b):
    return _bdot(a, b, ((1,), (0,)))


def _mm_nt(a, b):
    return _bdot(a, b, ((1,), (1,)))


def _mm_tn(a, b):
    return _bdot(a, b, ((0,), (0,)))


def _rwkv_chunk_kernel(r_ref, v_ref, kk_ref, k_ref, b_ref, lw_ref, y_ref, s_ref):
    c = RWKV_CHUNK
    d = pl.program_id(1)
    j = pl.program_id(2)

    @pl.when(j == 0)
    def _():
        s_ref[...] = jnp.zeros_like(s_ref)

    sign = jnp.where(d == 0, 1, -1)
    row = lax.broadcasted_iota(jnp.int32, (c, c), 0)
    col = lax.broadcasted_iota(jnp.int32, (c, c), 1)
    cum_mat = jnp.where((row - col) * sign >= 0, 1.0, 0.0).astype(BF16)

    lw = lw_ref[0, 0]
    lw_hi = lw.astype(BF16)
    rem = lw - lw_hi.astype(F32)
    lw_mid = rem.astype(BF16)
    lw_lo = (rem - lw_mid.astype(F32)).astype(BF16)
    lc_incl = _mm(cum_mat, lw_hi) + _mm(cum_mat, lw_mid) + _mm(cum_mat, lw_lo)
    lc_excl = lc_incl - lw
    lc_ref = 0.5 * jnp.sum(lw, axis=0, keepdims=True)

    row2 = lax.broadcasted_iota(jnp.int32, (2 * c, 2 * c), 0)
    col2 = lax.broadcasted_iota(jnp.int32, (2 * c, 2 * c), 1)
    same_head = (row2 // c) == (col2 // c)
    tdiff = jnp.where(same_head, ((row2 % c) - (col2 % c)) * sign, -1)
    strict2 = tdiff > 0
    incl2 = tdiff >= 0
    eye2 = jnp.where(row2 == col2, 1.0, 0.0)
    lane_head = lax.broadcasted_iota(jnp.int32, (c, LANES), 1) // RWKV_HEAD_DIM

    def stack2(z):
        return jnp.concatenate([jnp.where(lane_head == 0, z, 0.0), jnp.where(lane_head == 1, z, 0.0)], axis=0)

    pairs = range(RWKV_WIDTH // LANES)
    lane = lambda p: slice(p * LANES, (p + 1) * LANES)
    bf = lambda z: z.astype(BF16)
    x, q0_stack, v2, kb_end, decay_c, s0 = [], [], [], [], [], []
    for p in pairs:
        sl = lane(p)
        r, v, kk = r_ref[0, :, sl], v_ref[0, :, sl], kk_ref[0, :, sl]
        k, b = k_ref[0, 0, :, sl], b_ref[0, 0, :, sl]
        ref = lc_ref[:, sl]
        e_in = jnp.exp(lc_incl[:, sl] - ref)
        e_ex = jnp.exp(lc_excl[:, sl] - ref)
        e_inv = jnp.exp(ref - lc_incl[:, sl])
        e_ref = jnp.exp(ref)
        e_end = e_inv * e_ref
        q_stack = jnp.concatenate([stack2(kk * e_ex), stack2(r * e_in)], axis=0)
        k_stack = jnp.concatenate([stack2(k * e_inv), stack2(b * e_inv)], axis=0)
        x.append(_mm_nt(q_stack, k_stack))
        q0_stack.append(bf(jnp.concatenate([stack2(kk * (e_ex * e_ref)), stack2(r * (e_in * e_ref))], axis=0)))
        v2.append(bf(stack2(v)))
        kb_end.append(bf(jnp.concatenate([stack2(k * e_end), stack2(-(b * e_end))], axis=0)))
        decay_c.append(e_ref * e_ref)
        s0.append(s_ref[p])
    m_b = [jnp.where(strict2, x[p][:2 * c, 2 * c:], 0.0) for p in pairs]
    p_b = [bf(jnp.where(incl2, x[p][2 * c:, 2 * c:], 0.0)) for p in pairs]
    mk_pk = [bf(jnp.concatenate([jnp.where(strict2, x[p][:2 * c, :2 * c], 0.0),
                                 jnp.where(incl2, x[p][2 * c:, :2 * c], 0.0)], axis=0)) for p in pairs]
    from_state = [_mm_nt(q0_stack[p], s0[p]) for p in pairs]
    from_chunk = [_mm(mk_pk[p], v2[p]) for p in pairs]
    t_inv = [eye2 - m_b[p] for p in pairs]
    m_pow = [bf(m_b[p]) for p in pairs]
    for _ in range(int(math.log2(c)) - 1):
        m_pow = [bf(_mm(m_pow[p], m_pow[p])) for p in pairs]
        t_inv = [t_inv[p] + _mm(t_inv[p], m_pow[p]) for p in pairs]
    u2 = [_mm(t_inv[p], from_state[p][:2 * c] + from_chunk[p][:2 * c]) for p in pairs]
    y2 = [from_state[p][2 * c:] + from_chunk[p][2 * c:] - _mm(p_b[p], u2[p]) for p in pairs]
    s_new = [s0[p] * decay_c[p] + _mm_tn(jnp.concatenate([v2[p], bf(u2[p])], axis=0), kb_end[p]) for p in pairs]
    for p in pairs:
        y_ref[0, 0, :, lane(p)] = y2[p][:c] + y2[p][c:]
        s_ref[p] = s_new[p]


def rwkv_scan(r, v, kk, k_dir, b_dir, lw_dir, n_ctx):
    bsz, length, width = r.shape
    c = RWKV_CHUNK
    nc, nc_ctx = length // c, n_ctx // c

    def chunk_of(d, j):
        back = jnp.where(j < nc_ctx, nc_ctx - 1 - j, nc + nc_ctx - 1 - j)
        return jnp.where(d == 0, j, back)

    shared = pl.BlockSpec((1, c, width), lambda b, d, j: (b, chunk_of(d, j), 0))
    per_dir = pl.BlockSpec((1, 1, c, width), lambda b, d, j: (b, d, chunk_of(d, j), 0))
    return pl.pallas_call(
        _rwkv_chunk_kernel,
        grid=(bsz, 2, nc),
        in_specs=[shared, shared, shared, per_dir, per_dir, per_dir],
        out_specs=per_dir,
        out_shape=jax.ShapeDtypeStruct((bsz, 2, length, width), F32),
        scratch_shapes=[pltpu.VMEM((width // LANES, LANES, LANES), F32)],
        compiler_params=pltpu.CompilerParams(dimension_semantics=("arbitrary", "arbitrary", "arbitrary")),
        name="rwkv_scan",
    )(r, v, kk, k_dir, b_dir, lw_dir)


S5_CHUNK = 128


def _s5_chunk_kernel(u_ref, bre_ref, bim_ref, c_ref, are_ref, aim_ref, y_ref, carry_ref, *, reverse):
    tc = S5_CHUNK
    j = pl.program_id(1)

    @pl.when(j == 0)
    def _():
        carry_ref[...] = jnp.zeros_like(carry_ref)

    u = u_ref[0]
    xr = _mm(u, bre_ref[...])
    xi = _mm(u, bim_ref[...])
    ar, ai = are_ref[...], aim_ref[...]
    cr, ci = carry_ref[0:1, :], carry_ref[1:2, :]
    row = lax.broadcasted_iota(jnp.int32, xr.shape, 0)
    first = tc - 1 if reverse else 0
    xr = xr + jnp.where(row == first, ar * cr - ai * ci, 0.0)
    xi = xi + jnp.where(row == first, ar * ci + ai * cr, 0.0)
    for level in range(int(math.log2(tc))):
        sh = 1 << level
        if reverse:
            sr, si = pltpu.roll(xr, tc - sh, 0), pltpu.roll(xi, tc - sh, 0)
            keep = row < tc - sh
        else:
            sr, si = pltpu.roll(xr, sh, 0), pltpu.roll(xi, sh, 0)
            keep = row >= sh
        sr, si = jnp.where(keep, sr, 0.0), jnp.where(keep, si, 0.0)
        xr, xi = xr + (ar * sr - ai * si), xi + (ar * si + ai * sr)
        ar, ai = ar * ar - ai * ai, 2.0 * (ar * ai)
    last = 0 if reverse else tc - 1
    carry_ref[0:1, :] = xr[last:last + 1, :]
    carry_ref[1:2, :] = xi[last:last + 1, :]
    y_ref[0] = _mm(jnp.concatenate([xr, xi], axis=1), c_ref[...])


def s5_scan(u, b_re, b_im, c_cat, a_re, a_im, n_ctx, reverse):
    bsz, length, width = u.shape
    n_state = b_re.shape[1]
    tc = S5_CHUNK
    nc, nc_ctx = length // tc, n_ctx // tc

    def chunk_of(j):
        if not reverse:
            return j
        return jnp.where(j < nc_ctx, nc_ctx - 1 - j, nc + nc_ctx - 1 - j)

    tok = pl.BlockSpec((1, tc, width), lambda b, j: (b, chunk_of(j), 0))
    full = lambda shape: pl.BlockSpec(shape, lambda b, j: (0,) * len(shape))
    return pl.pallas_call(
        functools.partial(_s5_chunk_kernel, reverse=reverse),
        grid=(bsz, nc),
        in_specs=[tok, full(b_re.shape), full(b_im.shape), full(c_cat.shape), full(a_re.shape), full(a_im.shape)],
        out_specs=tok,
        out_shape=jax.ShapeDtypeStruct((bsz, length, width), F32),
        scratch_shapes=[pltpu.VMEM((8, n_state), F32)],
        compiler_params=pltpu.CompilerParams(dimension_semantics=("arbitrary", "arbitrary"),
                                             vmem_limit_bytes=48 * 1024 * 1024),
        name="s5_scan_rev" if reverse else "s5_scan_fwd",
    )(u, b_re, b_im, c_cat, a_re, a_im)


RET_HEADS = 4
RET_HEAD_DIM = 128
RET_WIDTH = RET_HEADS * RET_HEAD_DIM
RET_CHUNK = 128
ROPE_BASE = 10000.0
ROW_TILE = 256


def _ret_chunk_kernel(q_ref, k_ref, v_ref, cos_ref, sin_ref, dm_ref, dq_ref, dk_ref, o_ref, s_ref, *, chunk_decay):
    j = pl.program_id(2)

    @pl.when(j == 0)
    def _():
        s_ref[...] = jnp.zeros_like(s_ref)

    cos2, sin2 = cos_ref[...], sin_ref[...]
    rope = lambda z: z * cos2 + pltpu.roll(z, RET_HEAD_DIM // 2, 1) * sin2
    for h in range(RET_HEADS):
        sl = slice(h * RET_HEAD_DIM, (h + 1) * RET_HEAD_DIM)
        q = rope(q_ref[0, :, sl])
        k = rope(k_ref[0, :, sl] * (RET_HEAD_DIM ** -0.5))
        v = v_ref[0, :, sl]
        s0 = s_ref[h]
        scores = _mm_nt(q, k) * dm_ref[0, h]
        o_ref[0, 0, :, sl] = _mm(scores, v) + _mm(q * dq_ref[0, h], s0)
        s_ref[h] = chunk_decay[h] * s0 + _mm_tn(k * dk_ref[0, h], v)


def _ret_decay_tables():
    c = RET_CHUNK
    lg = jnp.log1p(-jnp.exp2(-5.0 - jnp.arange(RET_HEADS, dtype=F32)))[:, None, None]
    n = jnp.arange(c, dtype=F32)[:, None]
    m = jnp.arange(c, dtype=F32)[None, :]
    fwd = jnp.where(n >= m, jnp.exp(jnp.where(n >= m, n - m, 0.0) * lg), 0.0)
    bwd = jnp.where(m > n, jnp.exp(jnp.where(m > n, m - n, 0.0) * lg), 0.0)
    ones = jnp.ones((1, c), F32)
    dq = jnp.stack([jnp.exp((n + 1.0) * lg) * ones, jnp.exp((c - n) * lg) * ones])
    dk = jnp.stack([jnp.exp((c - 1.0 - n) * lg) * ones, jnp.exp(n * lg) * ones])
    return jnp.stack([fwd, bwd]), dq, dk


def retention_scan(p_ret, cos2, sin2, n_ctx):
    bsz, length, _ = p_ret.shape
    c, w = RET_CHUNK, RET_WIDTH
    nc, nc_ctx = length // c, n_ctx // c
    dm, dq, dk = _ret_decay_tables()
    chunk_decay = tuple(math.exp(c * math.log1p(-2.0 ** (-5 - h))) for h in range(RET_HEADS))

    def chunk_of(d, j):
        back = jnp.where(j < nc_ctx, nc_ctx - 1 - j, nc + nc_ctx - 1 - j)
        return jnp.where(d == 0, j, back)

    col = lambda which: pl.BlockSpec((1, c, w), lambda b, d, j: (b, chunk_of(d, j), which(d)))
    rope_spec = pl.BlockSpec((c, RET_HEAD_DIM), lambda b, d, j: (chunk_of(d, j), 0))
    table = pl.BlockSpec((1, RET_HEADS, c, c), lambda b, d, j: (d, 0, 0, 0))
    return pl.pallas_call(
        functools.partial(_ret_chunk_kernel, chunk_decay=chunk_decay),
        grid=(bsz, 2, nc),
        in_specs=[col(lambda d: 0), col(lambda d: 1 + d), col(lambda d: 3), rope_spec, rope_spec, table, table, table],
        out_specs=pl.BlockSpec((1, 1, c, w), lambda b, d, j: (b, d, chunk_of(d, j), 0)),
        out_shape=jax.ShapeDtypeStruct((bsz, 2, length, w), F32),
        scratch_shapes=[pltpu.VMEM((RET_HEADS, RET_HEAD_DIM, RET_HEAD_DIM), F32)],
        compiler_params=pltpu.CompilerParams(dimension_semantics=("arbitrary", "arbitrary", "arbitrary")),
        name="retention_scan",
    )(p_ret, p_ret, p_ret, cos2, sin2, dm, dq, dk)


def _ret_out_kernel(o_ref, g_ref, gn_g_ref, gn_b_ref, y_ref):
    o = o_ref[0, 0] + o_ref[0, 1]
    g = g_ref[0]
    for h in range(RET_HEADS):
        sl = slice(h * RET_HEAD_DIM, (h + 1) * RET_HEAD_DIM)
        z = o[:, sl]
        zc = z - jnp.mean(z, axis=1, keepdims=True)
        zn = zc * lax.rsqrt(jnp.mean(zc * zc, axis=1, keepdims=True) + EPS)
        gate = g[:, sl]
        y_ref[0, :, sl] = (zn * gn_g_ref[:, sl] + gn_b_ref[:, sl]) * (gate * jax.nn.sigmoid(gate))


def retention_out(o, p_ret, gn_g, gn_b):
    bsz, _, length, w = o.shape
    tm = ROW_TILE
    vec = pl.BlockSpec((1, w), lambda b, i: (0, 0))
    return pl.pallas_call(
        _ret_out_kernel,
        grid=(bsz, length // tm),
        in_specs=[pl.BlockSpec((1, 2, tm, w), lambda b, i: (b, 0, i, 0)),
                  pl.BlockSpec((1, tm, w), lambda b, i: (b, i, 4)), vec, vec],
        out_specs=pl.BlockSpec((1, tm, w), lambda b, i: (b, i, 0)),
        out_shape=jax.ShapeDtypeStruct((bsz, length, w), F32),
        compiler_params=pltpu.CompilerParams(dimension_semantics=("arbitrary", "arbitrary")),
        name="retention_out",
    )(o, p_ret, gn_g.reshape(1, w), gn_b.reshape(1, w))


def rope_tables(n_tokens, n_ctx):
    rows = n_tokens // GRID_W
    row = jnp.repeat(jnp.arange(rows, dtype=F32), GRID_W)
    col = jnp.tile(jnp.arange(GRID_W, dtype=F32), rows)
    n_freq = RET_HEAD_DIM // 4
    inv = ROPE_BASE ** (-jnp.arange(n_freq, dtype=F32) / n_freq)
    ang = jnp.concatenate([row[:, None] * inv, col[:, None] * inv], axis=-1)
    cos, sin = jnp.cos(ang), jnp.sin(ang)
    cos2 = jnp.concatenate([jnp.ones((n_ctx, RET_HEAD_DIM), F32), jnp.concatenate([cos, cos], axis=-1)], axis=0)
    sin2 = jnp.concatenate([jnp.zeros((n_ctx, RET_HEAD_DIM), F32), jnp.concatenate([-sin, sin], axis=-1)], axis=0)
    return cos2, sin2


VMEM_LIMIT = 56 * 1024 * 1024


def _const_spec(a):
    return pl.BlockSpec(a.shape, lambda b, i: (0,) * a.ndim, pipeline_mode=pl.Buffered(1))


def _mod_spec(d):
    return pl.BlockSpec((1, 1, 6, d), lambda b, i: (b, jnp.minimum(i, 1), 0, 0))


def _tok_spec(width, col=0):
    return pl.BlockSpec((1, ROW_TILE, width), lambda b, i: (b, i, col))


def _norm_mod(x, g, mod, shift_row, scale_row):
    y = x * lax.rsqrt(jnp.mean(x * x, axis=1, keepdims=True) + EPS) * g
    return y * (1.0 + mod[scale_row:scale_row + 1]) + mod[shift_row:shift_row + 1]


def _in_proj_kernel(x_ref, g_ref, mod_ref, w_ref, o_ref):
    n = _norm_mod(x_ref[0], g_ref[...], mod_ref[0, 0], 0, 1)
    o_ref[0] = _mm(n, w_ref[...])


def in_proj(x, norm_g, mods, w):
    bsz, length, d = x.shape
    n_out = w.shape[1]
    return pl.pallas_call(
        _in_proj_kernel,
        grid=(bsz, length // ROW_TILE),
        in_specs=[_tok_spec(d), _const_spec(norm_g), _mod_spec(d), _const_spec(w)],
        out_specs=_tok_spec(n_out),
        out_shape=jax.ShapeDtypeStruct((bsz, length, n_out), F32),
        compiler_params=pltpu.CompilerParams(dimension_semantics=("arbitrary", "arbitrary"),
                                             vmem_limit_bytes=VMEM_LIMIT),
        name="in_proj",
    )(x, norm_g, mods, w)


ADA_COLS = 1024


def _ada_kernel(c_ref, w_ref, b_ref, o_ref):
    cv = c_ref[...]
    o_ref[...] = _mm(cv * jax.nn.sigmoid(cv), w_ref[...]) + b_ref[...]


def ada_modulation(cond, w, b):
    rows, d = cond.shape
    n_out = w.shape[1]
    return pl.pallas_call(
        _ada_kernel,
        grid=(n_out // ADA_COLS,),
        in_specs=[pl.BlockSpec((rows, d), lambda j: (0, 0)), pl.BlockSpec((d, ADA_COLS), lambda j: (0, j)),
                  pl.BlockSpec((1, ADA_COLS), lambda j: (0, j))],
        out_specs=pl.BlockSpec((rows, ADA_COLS), lambda j: (0, j)),
        out_shape=jax.ShapeDtypeStruct((rows, n_out), F32),
        compiler_params=pltpu.CompilerParams(dimension_semantics=("arbitrary",)),
        name="ada_modulation",
    )(cond, w, b)


def _norm_mod_kernel(x_ref, g_ref, mod_ref, o_ref):
    o_ref[0] = _norm_mod(x_ref[0], g_ref[...], mod_ref[0, 0], 3, 4)


def norm_modulate2(x, norm_g, mods):
    bsz, length, d = x.shape
    return pl.pallas_call(
        _norm_mod_kernel,
        grid=(bsz, length // ROW_TILE),
        in_specs=[_tok_spec(d), _const_spec(norm_g), _mod_spec(d)],
        out_specs=_tok_spec(d),
        out_shape=jax.ShapeDtypeStruct((bsz, length, d), F32),
        compiler_params=pltpu.CompilerParams(dimension_semantics=("arbitrary", "arbitrary")),
        name="norm_modulate2",
    )(x, norm_g, mods)


def _residual_kernel(x_ref, f_ref, mod_ref, g_ref, o_ref, *, final_norm):
    y = x_ref[0] + mod_ref[0, 0][5:6] * f_ref[0]
    if final_norm:
        y = y * lax.rsqrt(jnp.mean(y * y, axis=1, keepdims=True) + EPS) * g_ref[...]
    o_ref[0] = y


def residual2(x, f, mods, final_g, final_norm):
    bsz, length, d = x.shape
    return pl.pallas_call(
        functools.partial(_residual_kernel, final_norm=final_norm),
        grid=(bsz, length // ROW_TILE),
        in_specs=[_tok_spec(d), _tok_spec(d), _mod_spec(d), _const_spec(final_g)],
        out_specs=_tok_spec(d),
        out_shape=jax.ShapeDtypeStruct((bsz, length, d), F32),
        compiler_params=pltpu.CompilerParams(dimension_semantics=("arbitrary", "arbitrary")),
        name="residual2",
    )(x, f, mods, final_g)


def _merge_kernel(x_ref, g_ref, mod_ref, ya_ref, yb_ref, yc_ref, yd_ref, wg_ref, bg_ref, wbr_ref, wout_ref, o_ref):
    x = x_ref[0]
    mod = mod_ref[0, 0]
    n = _norm_mod(x, g_ref[...], mod, 0, 1).astype(BF16)
    m = jnp.zeros(x.shape, F32)
    for i, y_ref in enumerate((ya_ref, yb_ref, yc_ref, yd_ref)):
        gate = jax.nn.sigmoid(_mm(n, wg_ref[i]) + bg_ref[i:i + 1])
        m = m + gate * _mm(y_ref[0], wbr_ref[i])
    o_ref[0] = x + mod[2:3] * _mm(m, wout_ref[...])


def merge_residual(x, norm_g, mods, ys, w_merge, b_merge, w_branch, w_out):
    bsz, length, d = x.shape
    bw = ys[0].shape[-1]
    return pl.pallas_call(
        _merge_kernel,
        grid=(bsz, length // ROW_TILE),
        in_specs=[_tok_spec(d), _const_spec(norm_g), _mod_spec(d)] + [_tok_spec(bw)] * 4
        + [_const_spec(w_merge), _const_spec(b_merge), _const_spec(w_branch), _const_spec(w_out)],
        out_specs=_tok_spec(d),
        out_shape=jax.ShapeDtypeStruct((bsz, length, d), F32),
        compiler_params=pltpu.CompilerParams(dimension_semantics=("arbitrary", "arbitrary"),
                                             vmem_limit_bytes=VMEM_LIMIT),
        name="merge_residual",
    )(x, norm_g, mods, *ys, w_merge, b_merge, w_branch, w_out)


def _halo_specs(width, n_tiles):
    prev = pl.BlockSpec((1, ROW_TILE, width), lambda b, i: (b, jnp.maximum(i - 1, 0), 0))
    nxt = pl.BlockSpec((1, ROW_TILE, width), lambda b, i: (b, jnp.minimum(i + 1, n_tiles - 1), 0))
    return [prev, _tok_spec(width), nxt]


def _segment_edges(n_tiles):
    i = pl.program_id(1)
    return i >= 2, jnp.logical_and(i >= 1, i < n_tiles - 1)


CONV_CHANNELS = 512
CONV_TAPS = 31
HALO = 16


def _conformer_kernel(prev_ref, cur_ref, next_ref, dw_ref, db_ref, lng_ref, lnb_ref, o_ref, ext_ref, *, n_tiles):
    ch = CONV_CHANNELS
    glu = lambda p: p[:, :ch] * jax.nn.sigmoid(p[:, ch:])
    has_prev, has_next = _segment_edges(n_tiles)
    ext_ref[0:HALO, :] = jnp.where(has_prev, glu(prev_ref[0, ROW_TILE - HALO:, :]), 0.0)
    ext_ref[HALO:HALO + ROW_TILE, :] = glu(cur_ref[0])
    ext_ref[HALO + ROW_TILE:, :] = jnp.where(has_next, glu(next_ref[0, :HALO, :]), 0.0)
    pad = (CONV_TAPS - 1) // 2
    acc = jnp.zeros((ROW_TILE, ch), F32) + db_ref[...]
    for j in range(CONV_TAPS):
        acc = acc + ext_ref[pl.ds(HALO - pad + j, ROW_TILE), :] * dw_ref[j:j + 1, :]
    zc = acc - jnp.mean(acc, axis=1, keepdims=True)
    z = zc * lax.rsqrt(jnp.mean(zc * zc, axis=1, keepdims=True) + EPS) * lng_ref[...] + lnb_ref[...]
    o_ref[0] = z * jax.nn.sigmoid(z)


def conformer_conv(p, dw, db, ln_g, ln_b):
    bsz, length, width = p.shape
    n_tiles = length // ROW_TILE
    ch = CONV_CHANNELS
    row = lambda v: v.reshape(1, ch)
    return pl.pallas_call(
        functools.partial(_conformer_kernel, n_tiles=n_tiles),
        grid=(bsz, n_tiles),
        in_specs=_halo_specs(width, n_tiles) + [_const_spec(dw)] + [_const_spec(row(db))] * 3,
        out_specs=_tok_spec(ch),
        out_shape=jax.ShapeDtypeStruct((bsz, length, ch), F32),
        scratch_shapes=[pltpu.VMEM((ROW_TILE + 2 * HALO, ch), F32)],
        compiler_params=pltpu.CompilerParams(dimension_semantics=("arbitrary", "arbitrary"),
                                             vmem_limit_bytes=VMEM_LIMIT),
        name="conformer_conv",
    )(p, p, p, dw, row(db), row(ln_g), row(ln_b))


def _s5_out_kernel(u_ref, yf_ref, yb_ref, d_ref, w_ref, b_ref, o_ref):
    z = _gelu_tanh(d_ref[...] * u_ref[0] + yf_ref[0] + yb_ref[0])
    o_ref[0] = z * jax.nn.sigmoid(_mm(z, w_ref[...]) + b_ref[...])


def s5_out(u, y_fwd, y_bwd, d_skip, glu_w, glu_b):
    bsz, length, w = u.shape
    return pl.pallas_call(
        _s5_out_kernel,
        grid=(bsz, length // ROW_TILE),
        in_specs=[_tok_spec(w)] * 3 + [_const_spec(d_skip), _const_spec(glu_w), _const_spec(glu_b)],
        out_specs=_tok_spec(w),
        out_shape=jax.ShapeDtypeStruct((bsz, length, w), F32),
        compiler_params=pltpu.CompilerParams(dimension_semantics=("arbitrary", "arbitrary")),
        name="s5_out",
    )(u, y_fwd, y_bwd, d_skip, glu_w, glu_b)


def _head_sums(z, ones_bd):
    hi = z.astype(BF16)
    lo = (z - hi.astype(F32)).astype(BF16)
    return _mm(hi, ones_bd) + _mm(lo, ones_bd)


def _rwkv_prep_kernel(prev_ref, cur_ref, next_ref, shift_ref, w2_ref, a2_ref, g2_ref, vec_ref, bd_ref,
                      r_ref, v_ref, kk_ref, gb_ref, k_ref, b_ref, lw_ref, ext_ref, *, n_tiles):
    w = RWKV_WIDTH
    has_prev, has_next = _segment_edges(n_tiles)
    ext_ref[0:8, :] = jnp.where(has_prev, prev_ref[0, ROW_TILE - 8:, :], 0.0)
    ext_ref[8:8 + ROW_TILE, :] = cur_ref[0]
    ext_ref[8 + ROW_TILE:, :] = jnp.where(has_next, next_ref[0, :8, :], 0.0)
    p = (ext_ref[pl.ds(7, ROW_TILE), :] * shift_ref[0:1, :] + ext_ref[pl.ds(8, ROW_TILE), :] * shift_ref[1:2, :]
         + ext_ref[pl.ds(9, ROW_TILE), :] * shift_ref[2:3, :])
    r, k, v = p[:, :w], p[:, w:2 * w], p[:, 2 * w:3 * w]
    lowrank_w = jnp.tanh(p[:, 3 * w:3 * w + LANES])
    lowrank_a = p[:, 3 * w + LANES:3 * w + 2 * LANES]
    gl = p[:, 3 * w + 2 * LANES:]
    ones_bd = bd_ref[...]
    kk = k * vec_ref[0:1, :]
    kk = kk * lax.rsqrt(_head_sums(kk * kk, ones_bd) + EPS)
    r_ref[0], v_ref[0], kk_ref[0] = r, v, kk
    gb_ref[0, 0] = _mm(jax.nn.sigmoid(gl), g2_ref[...])
    k_sum = jnp.zeros_like(k)
    for d in range(2):
        w_log = -jax.nn.softplus(-(vec_ref[4 + d:5 + d, :] + _mm(lowrank_w, w2_ref[d]))) - 0.5
        a = jax.nn.sigmoid(vec_ref[6 + d:7 + d, :] + _mm(lowrank_a, a2_ref[d]))
        k_d = k * (1.0 + (a - 1.0) * vec_ref[1:2, :])
        k_ref[0, d], b_ref[0, d], lw_ref[0, d] = k_d, a * kk, -jnp.exp(w_log)
        k_sum = k_sum + k_d
    gb_ref[0, 1] = _head_sums(r * k_sum * vec_ref[2:3, :], ones_bd) * v


def rwkv_prep(p, shift_w, w2, a2, g2, vecs, ones_bd):
    bsz, length, width = p.shape
    n_tiles = length // ROW_TILE
    w = RWKV_WIDTH
    shared = jax.ShapeDtypeStruct((bsz, length, w), F32)
    per_dir = jax.ShapeDtypeStruct((bsz, 2, length, w), F32)
    dir_spec = pl.BlockSpec((1, 2, ROW_TILE, w), lambda b, i: (b, 0, i, 0))
    return pl.pallas_call(
        functools.partial(_rwkv_prep_kernel, n_tiles=n_tiles),
        grid=(bsz, n_tiles),
        in_specs=_halo_specs(width, n_tiles) + [_const_spec(a) for a in (shift_w, w2, a2, g2, vecs, ones_bd)],
        out_specs=[_tok_spec(w)] * 3 + [dir_spec] * 4,
        out_shape=[shared] * 3 + [per_dir] * 4,
        scratch_shapes=[pltpu.VMEM((ROW_TILE + 16, width), F32)],
        compiler_params=pltpu.CompilerParams(dimension_semantics=("arbitrary", "arbitrary"),
                                             vmem_limit_bytes=VMEM_LIMIT),
        name="rwkv_prep",
    )(p, p, p, shift_w, w2, a2, g2, vecs, ones_bd)


def _rwkv_out_kernel(y_ref, gb_ref, gn_ref, bd_ref, o_ref):
    y = y_ref[0, 0] + y_ref[0, 1]
    ones_bd = bd_ref[...]
    inv = 1.0 / RWKV_HEAD_DIM
    yc = y - _head_sums(y, ones_bd) * inv
    yn = yc * lax.rsqrt(_head_sums(yc * yc, ones_bd) * inv + EPS)
    o_ref[0] = (yn * gn_ref[0:1, :] + gn_ref[1:2, :] + gb_ref[0, 1]) * gb_ref[0, 0]


def rwkv_out(y, gate_bonus, gn, ones_bd):
    bsz, _, length, w = y.shape
    dir_spec = pl.BlockSpec((1, 2, ROW_TILE, w), lambda b, i: (b, 0, i, 0))
    return pl.pallas_call(
        _rwkv_out_kernel,
        grid=(bsz, length // ROW_TILE),
        in_specs=[dir_spec, dir_spec, _const_spec(gn), _const_spec(ones_bd)],
        out_specs=_tok_spec(w),
        out_shape=jax.ShapeDtypeStruct((bsz, length, w), F32),
        compiler_params=pltpu.CompilerParams(dimension_semantics=("arbitrary", "arbitrary")),
        name="rwkv_out",
    )(y, gate_bonus, gn, ones_bd)


SC_CORES = 2
SC_SUBCORES = 16
SC_WORKERS = SC_CORES * SC_SUBCORES
GATHER_ROWS = 64
PEER_SLOTS = 128
PEER_TOKENS_PER_STEP = 16


PEER_HEADS = 8
PEER_KEYS = 128
PEER_TOPK = 16
PEER_HALF = 128
PEER_SELECT_TOKENS = 128


def _top_rows(s, payload=None):
    n_rows = s.shape[0]
    iota = lax.broadcasted_iota(jnp.int32, s.shape, 0)
    vals, picks = [], []
    for _ in range(PEER_TOPK):
        m = jnp.max(s, axis=0, keepdims=True)
        pos = jnp.min(jnp.where(s == m, iota, n_rows), axis=0, keepdims=True)
        hit = iota == pos
        vals.append(m)
        picks.append(pos if payload is None else jnp.max(jnp.where(hit, payload, -1), axis=0, keepdims=True))
        s = jnp.where(hit, -jnp.inf, s)
    return jnp.concatenate(vals, axis=0), jnp.concatenate(picks, axis=0)


def _peer_select_kernel(h_ref, wq_ref, keys_ref, ids_ref, gate_ref):
    q = _mm(h_ref[...], wq_ref[...]).astype(BF16)
    for h in range(PEER_HEADS):
        halves = []
        for p in range(2):
            lst = 2 * h + p
            s = _mm_nt(keys_ref[lst], q[:, lst * PEER_HALF:(lst + 1) * PEER_HALF])
            halves.append(_top_rows(s))
        (v1, p1), (v2, p2) = halves
        cand = jnp.concatenate([v1[i:i + 1] + v2 for i in range(PEER_TOPK)], axis=0)
        cand_id = jnp.concatenate([p1[i:i + 1] * PEER_KEYS + p2 for i in range(PEER_TOPK)], axis=0)
        best, ids = _top_rows(cand, cand_id)
        e = jnp.exp(best - best[0:1])
        ids_ref[h * PEER_TOPK:(h + 1) * PEER_TOPK, :] = ids
        gate_ref[h * PEER_TOPK:(h + 1) * PEER_TOPK, :] = e / jnp.sum(e, axis=0, keepdims=True)


def peer_select(h, wq, keys):
    n, d = h.shape
    tn = PEER_SELECT_TOKENS
    slots = PEER_HEADS * PEER_TOPK
    full = lambda a: pl.BlockSpec(a.shape, lambda i: (0,) * a.ndim)
    out = pl.BlockSpec((slots, tn), lambda i: (0, i))
    return pl.pallas_call(
        _peer_select_kernel,
        grid=(n // tn,),
        in_specs=[pl.BlockSpec((tn, d), lambda i: (i, 0)), full(wq), full(keys)],
        out_specs=[out, out],
        out_shape=[jax.ShapeDtypeStruct((slots, n), jnp.int32), jax.ShapeDtypeStruct((slots, n), F32)],
        compiler_params=pltpu.CompilerParams(dimension_semantics=("arbitrary",),
                                             vmem_limit_bytes=48 * 1024 * 1024),
        name="peer_select",
    )(h, wq, keys)


def pack_bf16_pairs(table):
    half = table.shape[1] // 2
    bits = lax.bitcast_convert_type(table.astype(BF16), jnp.uint16).astype(jnp.uint32)
    return bits[:, :half] | (bits[:, half:] << 16)


def sc_gather_rows(table, idx):
    n_rows, width = idx.shape[0], table.shape[1]
    per_worker = n_rows // SC_WORKERS
    n_pairs = per_worker // (2 * GATHER_ROWS)
    assert per_worker * SC_WORKERS == n_rows and n_pairs * 2 * GATHER_ROWS == per_worker
    mesh = plsc.VectorSubcoreMesh(core_axis_name="c", subcore_axis_name="s",
                                  num_cores=SC_CORES, num_subcores=SC_SUBCORES)

    @functools.partial(
        pl.kernel, mesh=mesh,
        out_type=jax.ShapeDtypeStruct((n_rows, width), table.dtype),
        scratch_types=[pltpu.VMEM((2, GATHER_ROWS), jnp.int32),
                       pltpu.VMEM((2, GATHER_ROWS, width), table.dtype),
                       pltpu.SemaphoreType.DMA((2,)),
                       pltpu.SemaphoreType.DMA((2,))],
        name="peer_sc_gather",
    )
    def gather(table_hbm, idx_hbm, out_hbm, idx_v, rows_v, gather_sem, write_sem):
        worker = lax.axis_index("s") * SC_CORES + lax.axis_index("c")
        base = worker * per_worker

        def rows_of(chunk):
            return pl.ds(pl.multiple_of(base + chunk * GATHER_ROWS, GATHER_ROWS), GATHER_ROWS)

        def gather_copy(slot):
            return pltpu.make_async_copy(table_hbm.at[idx_v.at[slot]], rows_v.at[slot], gather_sem.at[slot])

        def write_copy(chunk, slot):
            return pltpu.make_async_copy(rows_v.at[slot], out_hbm.at[rows_of(chunk)], write_sem.at[slot])

        def start_gather(chunk, slot):
            pltpu.sync_copy(idx_hbm.at[rows_of(chunk)], idx_v.at[slot])
            gather_copy(slot).start()

        start_gather(0, 0)

        @pl.loop(0, n_pairs)
        def _(g):
            even, odd = 2 * g, 2 * g + 1

            @pl.when(g > 0)
            def _():
                write_copy(odd - 2, 1).wait()
            start_gather(odd, 1)
            gather_copy(0).wait()
            write_copy(even, 0).start()
            gather_copy(1).wait()
            write_copy(odd, 1).start()
            write_copy(even, 0).wait()

            @pl.when(g + 1 < n_pairs)
            def _():
                start_gather(even + 2, 0)

        write_copy(2 * n_pairs - 1, 1).wait()

    return gather(table, idx)


def _unpack_pairs(words):
    lo = pltpu.bitcast(words << 16, F32)
    hi = pltpu.bitcast(words & jnp.uint32(0xFFFF0000), F32)
    return lo, hi


def _gelu_tanh(x):
    return 0.5 * x * (1.0 + jnp.tanh(0.7978845608028654 * (x + 0.044715 * (x * x * x))))


def _peer_expert_kernel(z_ref, gate_ref, ug_ref, vg_ref, o_ref):
    half = ug_ref.shape[2]
    gate_t = gate_ref[...].T
    for n in range(PEER_TOKENS_PER_STEP):
        z_lo, z_hi = z_ref[n:n + 1, :half], z_ref[n:n + 1, half:]
        u_lo, u_hi = _unpack_pairs(ug_ref[n])
        prod = u_lo * z_lo + u_hi * z_hi
        act = jnp.sum(prod, axis=1, keepdims=True)
        w = _gelu_tanh(act) * gate_t[:, n:n + 1]
        v_lo, v_hi = _unpack_pairs(vg_ref[n])
        o_ref[n:n + 1, :half] = jnp.sum(w * v_lo, axis=0, keepdims=True)
        o_ref[n:n + 1, half:] = jnp.sum(w * v_hi, axis=0, keepdims=True)


def peer_experts(z, gate, ug, vg):
    n, d = z.shape
    tn = PEER_TOKENS_PER_STEP
    slots, half = ug.shape[1], ug.shape[2]
    return pl.pallas_call(
        _peer_expert_kernel,
        grid=(n // tn,),
        in_specs=[pl.BlockSpec((tn, d), lambda i: (i, 0)),
                  pl.BlockSpec((tn, slots), lambda i: (i, 0)),
                  pl.BlockSpec((tn, slots, half), lambda i: (i, 0, 0)),
                  pl.BlockSpec((tn, slots, half), lambda i: (i, 0, 0))],
        out_specs=pl.BlockSpec((tn, d), lambda i: (i, 0)),
        out_shape=jax.ShapeDtypeStruct((n, d), F32),
        compiler_params=pltpu.CompilerParams(dimension_semantics=("arbitrary",),
                                             vmem_limit_bytes=48 * 1024 * 1024),
        name="peer_experts",
    )(z, gate, ug, vg)


CONV_CHANNELS = 512
CONV_TAPS = 31
S5_WIDTH = 512
S5_GROUP = 16
S5_GROUPS = S5_WIDTH // S5_GROUP
S5_STATE = 64
S5_MAX_RE = -1e-4
RET_HEADS = 4
RET_HEAD_DIM = 128
RET_WIDTH = RET_HEADS * RET_HEAD_DIM
RET_CHUNK = 128
ROPE_BASE = 10000.0
N_BRANCH = 4
PEER_HEADS = 8
PEER_KEYS = 128
PEER_TOPK = 16
PEER_QUERY = 256
PEER_HALF = PEER_QUERY // 2
PEER_BLOCK = 128
SHIFT_TAPS = 3
RWKV_IN = 3 * RWKV_WIDTH + 2 * RWKV_DECAY_RANK + 2 * RWKV_ICLR_RANK + RWKV_GATE_RANK
CONV_IN = 2 * CONV_CHANNELS
S5_IN = S5_WIDTH
RET_IN = 5 * RET_WIDTH
A_END = RWKV_IN
B_END = A_END + CONV_IN
C_END = B_END + S5_IN
IN_WIDTH = C_END + RET_IN
RWKV_SPLITS = (RWKV_WIDTH, 2 * RWKV_WIDTH, 3 * RWKV_WIDTH,
               3 * RWKV_WIDTH + RWKV_DECAY_RANK, 3 * RWKV_WIDTH + 2 * RWKV_DECAY_RANK,
               3 * RWKV_WIDTH + 2 * RWKV_DECAY_RANK + RWKV_ICLR_RANK,
               3 * RWKV_WIDTH + 2 * RWKV_DECAY_RANK + 2 * RWKV_ICLR_RANK)


def _rms_norm(z, g):
    zf = z.astype(F32)
    y = zf * lax.rsqrt(jnp.mean(zf * zf, axis=-1, keepdims=True) + EPS)
    return (y * g).astype(z.dtype)


def _normalise(z):
    zf = z.astype(F32)
    zc = zf - jnp.mean(zf, axis=-1, keepdims=True)
    return zc * lax.rsqrt(jnp.mean(zc * zc, axis=-1, keepdims=True) + EPS)


def _modulate(z, shift, scale):
    return z * (1.0 + scale) + shift


def _depthwise_conv(z, w):
    taps = w.shape[0]
    pad = (taps - 1) // 2
    return lax.conv_general_dilated(
        z, w[:, None, :].astype(z.dtype), window_strides=(1,), padding=[(pad, pad)],
        dimension_numbers=("NWC", "WIO", "NWC"), feature_group_count=z.shape[-1])


def _axial_rope(n_tokens):
    rows = n_tokens // GRID_W
    row = jnp.repeat(jnp.arange(rows, dtype=F32), GRID_W)
    col = jnp.tile(jnp.arange(GRID_W, dtype=F32), rows)
    n_freq = RET_HEAD_DIM // 4
    inv = ROPE_BASE ** (-jnp.arange(n_freq, dtype=F32) / n_freq)
    ang = jnp.concatenate([row[:, None] * inv, col[:, None] * inv], axis=-1)
    return jnp.cos(ang), jnp.sin(ang)


def _apply_rope(z, cos, sin):
    z1, z2 = jnp.split(z, 2, axis=-1)
    cs, sn = cos[None, :, None, :], sin[None, :, None, :]
    return jnp.concatenate([z1 * cs - z2 * sn, z1 * sn + z2 * cs], axis=-1)


def _rwkv_prep(p, shift_w, w0, w2, a0, a2, g2, k_k, k_a):
    p = _depthwise_conv(p, shift_w)
    r, k, v, wl_f, wl_b, al_f, al_b, gl = jnp.split(p, RWKV_SPLITS, axis=-1)
    b, t, _ = r.shape
    kk = (k * k_k).reshape(b, t, RWKV_HEADS, RWKV_HEAD_DIM)
    kk = (kk * lax.rsqrt(jnp.sum(kk * kk, axis=-1, keepdims=True) + EPS)).reshape(b, t, RWKV_WIDTH)
    g = jax.nn.sigmoid(gl) @ g2
    ks, bs, lws = [], [], []
    for d, (wl, al) in enumerate(((wl_f, al_f), (wl_b, al_b))):
        w_log = -jax.nn.softplus(-(w0[d] + jnp.tanh(wl) @ w2[d])) - 0.5
        a = jax.nn.sigmoid(a0[d] + al @ a2[d])
        ks.append(k * (1.0 + (a - 1.0) * k_a))
        bs.append(a * kk)
        lws.append(-jnp.exp(w_log))
    return r, v, kk, g, jnp.stack(ks, 1), jnp.stack(bs, 1), jnp.stack(lws, 1)


def _rwkv_mixer(p_lat, p_ctx, shift_w, w0, w2, a0, a2, g2, k_k, k_a, r_k, gn_g, gn_b):
    n_ctx = p_ctx.shape[1]
    prep_l = _rwkv_prep(p_lat, shift_w, w0, w2, a0, a2, g2, k_k, k_a)
    prep_c = _rwkv_prep(p_ctx, shift_w, w0, w2, a0, a2, g2, k_k, k_a)
    r, v, kk, g = (jnp.concatenate([c, l], axis=1) for c, l in zip(prep_c[:4], prep_l[:4]))
    k_dir, b_dir, lw_dir = (jnp.concatenate([c, l], axis=2) for c, l in zip(prep_c[4:], prep_l[4:]))
    y = rwkv_scan(r, v, kk, k_dir, b_dir, lw_dir, n_ctx)
    y = y[:, 0] + y[:, 1]
    b, t, _ = y.shape
    heads = lambda z: z.reshape(b, t, RWKV_HEADS, RWKV_HEAD_DIM)
    k_sum = k_dir[:, 0] + k_dir[:, 1]
    bonus = jnp.sum(heads(r) * heads(k_sum) * r_k, axis=-1, keepdims=True) * heads(v)
    o = _normalise(heads(y)).reshape(b, t, RWKV_WIDTH) * gn_g + gn_b + bonus.reshape(b, t, RWKV_WIDTH)
    o = o * g
    return o[:, n_ctx:], o[:, :n_ctx]


def _conformer_conv(p, dw, db, ln_g, ln_b):
    val, gate = jnp.split(p, 2, axis=-1)
    z = val * jax.nn.sigmoid(gate)
    z = _depthwise_conv(z, dw) + db
    z = _normalise(z) * ln_g + ln_b
    return jax.nn.silu(z).astype(p.dtype)


def _s5_discretise(lam_re, lam_im, log_dt, b_re, b_im):
    lam_re = jnp.minimum(lam_re.astype(F32), S5_MAX_RE)
    lam_im = lam_im.astype(F32)
    dt = jnp.exp(log_dt.astype(F32))[:, None]
    mag = jnp.exp(lam_re * dt)
    ang = lam_im * dt
    ab_re, ab_im = mag * jnp.cos(ang), mag * jnp.sin(ang)
    den = lam_re * lam_re + lam_im * lam_im
    nr, ni = ab_re - 1.0, ab_im
    f_re = (nr * lam_re + ni * lam_im) / den
    f_im = (ni * lam_re - nr * lam_im) / den
    b_re, b_im = b_re.astype(F32), b_im.astype(F32)
    bb_re = f_re[..., None] * b_re - f_im[..., None] * b_im
    bb_im = f_re[..., None] * b_im + f_im[..., None] * b_re
    return ab_re, ab_im, bb_re, bb_im


def _s5_mixer(u_lat, u_ctx, lam_re, lam_im, log_dt, b_re, b_im, c_re, c_im, d_skip, glu_w, glu_b):
    n_ctx = u_ctx.shape[1]
    u = jnp.concatenate([u_ctx, u_lat], axis=1).astype(F32)
    eye = jnp.eye(S5_GROUPS, dtype=F32)
    n_state = S5_GROUPS * S5_STATE
    y = d_skip * u
    for d in range(2):
        ab_re, ab_im, bb_re, bb_im = _s5_discretise(lam_re[d], lam_im[d], log_dt[d], b_re[d], b_im[d])
        blk_in = lambda bb: jnp.einsum("gph,gk->ghkp", bb, eye).reshape(S5_WIDTH, n_state).astype(BF16)
        blk_out = lambda cc: jnp.einsum("ghp,gk->kpgh", cc.astype(F32), eye).reshape(n_state, S5_WIDTH)
        c_cat = jnp.concatenate([blk_out(c_re[d]), -blk_out(c_im[d])], axis=0).astype(BF16)
        y = y + s5_scan(u, blk_in(bb_re), blk_in(bb_im), c_cat, ab_re.reshape(1, n_state),
                        ab_im.reshape(1, n_state), n_ctx, reverse=(d == 1))
    z = jax.nn.gelu(y)
    out = z * jax.nn.sigmoid(z @ glu_w + glu_b)
    return out[:, n_ctx:], out[:, :n_ctx]


def _retention_chunkwise(q, k, v, r0, log_gamma, strict, emit):
    b, t, h, _ = q.shape
    n_chunks = t // RET_CHUNK
    chunks = lambda z: z.reshape(b, n_chunks, RET_CHUNK, h, z.shape[-1]).transpose(1, 0, 3, 2, 4)
    pos = jnp.arange(RET_CHUNK, dtype=F32)
    diff = pos[:, None] - pos[None, :]
    keep = diff > 0 if strict else diff >= 0
    decay_in = jnp.where(keep, jnp.exp(jnp.where(keep, diff, 0.0) * log_gamma[:, None, None]), 0.0)
    decay_q = jnp.exp((pos + 1.0) * log_gamma[:, None])[None, :, :, None]
    decay_k = jnp.exp((RET_CHUNK - 1.0 - pos) * log_gamma[:, None])[None, :, :, None]
    decay_chunk = jnp.exp(RET_CHUNK * log_gamma)[None, :, None, None]

    def step(r, inp):
        qc, kc, vc = inp
        r_next = decay_chunk * r + jnp.einsum("bhck,bhcv->bhkv", kc * decay_k, vc)
        if not emit:
            return r_next, None
        scores = jnp.einsum("bhnk,bhmk->bhnm", qc, kc) * decay_in
        o = jnp.einsum("bhnm,bhmv->bhnv", scores, vc) + jnp.einsum("bhnk,bhkv->bhnv", qc * decay_q, r)
        return r_next, o
    r_final, o = lax.scan(step, r0, (chunks(q), chunks(k), chunks(v)))
    if emit:
        o = o.transpose(1, 0, 3, 2, 4).reshape(b, t, h, v.shape[-1])
    return o, r_final


def _retention_mixer(p_lat, p_ctx, gn_g, gn_b, rope_cos, rope_sin):
    log_gamma = jnp.log1p(-jnp.exp2(-5.0 - jnp.arange(RET_HEADS, dtype=F32)))
    k_scale = RET_HEAD_DIM ** -0.5

    def split_heads(p):
        b, t, _ = p.shape
        q, kf, kb, v, g = jnp.split(p.astype(F32), 5, axis=-1)
        hd = lambda z: z.reshape(b, t, RET_HEADS, RET_HEAD_DIM)
        return hd(q), hd(kf) * k_scale, hd(kb) * k_scale, hd(v), g
    ql, kfl, kbl, vl, gl = split_heads(p_lat)
    ql, kfl, kbl = (_apply_rope(z, rope_cos, rope_sin) for z in (ql, kfl, kbl))
    qc, kfc, kbc, vc, gc = split_heads(p_ctx)
    r_zero = jnp.zeros((p_ctx.shape[0], RET_HEADS, RET_HEAD_DIM, RET_HEAD_DIM), F32)
    flip = lambda z: jnp.flip(z, axis=1)
    oc_f, rc_f = _retention_chunkwise(qc, kfc, vc, r_zero, log_gamma, False, True)
    oc_b, rc_b = _retention_chunkwise(flip(qc), flip(kbc), flip(vc), r_zero, log_gamma, True, True)
    ol_f, _ = _retention_chunkwise(ql, kfl, vl, rc_f, log_gamma, False, True)
    ol_b, _ = _retention_chunkwise(flip(ql), flip(kbl), flip(vl), rc_b, log_gamma, True, True)

    def readout(o, g):
        b, t = o.shape[:2]
        return (_normalise(o).reshape(b, t, RET_WIDTH) * gn_g + gn_b) * jax.nn.silu(g)
    return readout(ol_f + flip(ol_b), gl), readout(oc_f + flip(oc_b), gc)


def _merge_branches(n, ys, w_branch, w_merge, b_merge, w_out):
    m = 0.0
    for i, y in enumerate(ys):
        gate = jax.nn.sigmoid(n @ w_merge[i] + b_merge[i])
        m = m + gate * (y.astype(n.dtype) @ w_branch[i])
    return m @ w_out


def _peer_ffn(h, w_q, sub_keys, u_packed, v_packed):
    n, d = h.shape
    keys = sub_keys.reshape(2 * PEER_HEADS, PEER_KEYS, PEER_HALF).astype(BF16)
    wq = w_q.astype(BF16)
    step = 2 * SC_WORKERS * GATHER_ROWS
    n_blk = next(k for k in (8, 6, 4, 3, 2, 1)
                 if n % (k * PEER_SELECT_TOKENS) == 0 and (n // k * PEER_SLOTS) % step == 0)
    nb = n // n_blk
    outs = []
    for i in range(n_blk):
        h_b = h[i * nb:(i + 1) * nb]
        ids_t, gate_t = peer_select(h_b, wq, keys)
        flat_ids = ids_t.T.reshape(-1)
        ug = sc_gather_rows(u_packed, flat_ids).reshape(nb, PEER_SLOTS, d // 2)
        vg = sc_gather_rows(v_packed, flat_ids).reshape(nb, PEER_SLOTS, d // 2)
        outs.append(peer_experts(h_b, gate_t.T, ug, vg))
    return jnp.concatenate(outs, axis=0)


def kernel(x, c, ctx, c_ctx, ada_w, ada_b, norm1_g, norm2_g, w_in, rwkv_shift, rwkv_w0, rwkv_w2,
           rwkv_a0, rwkv_a2, rwkv_g2, rwkv_kk, rwkv_ka, rwkv_rk, rwkv_gn_g, rwkv_gn_b,
           conv_dw, conv_db, conv_ln_g, conv_ln_b, s5_lam_re, s5_lam_im, s5_log_dt,
           s5_b_re, s5_b_im, s5_c_re, s5_c_im, s5_d, s5_glu_w, s5_glu_b, ret_gn_g, ret_gn_b,
           w_branch, w_merge, b_merge, w_out, peer_wq, peer_keys, peer_u, peer_v, final_g):
    depth = ada_w.shape[0]
    bsz, n_lat, d = x.shape
    n_ctx = ctx.shape[1]
    assert n_ctx == ROW_TILE and n_lat % ROW_TILE == 0
    xa = jnp.concatenate([ctx, x], axis=1)
    cos2, sin2 = rope_tables(n_lat, n_ctx)
    ones_bd = jnp.kron(jnp.eye(RWKV_HEADS, dtype=F32), jnp.ones((RWKV_HEAD_DIM, RWKV_HEAD_DIM), F32)).astype(BF16)
    cond = jnp.zeros((8, d), F32).at[:bsz].set(c).at[bsz].set(c_ctx)
    row = lambda v: v.reshape(1, -1)
    w = RWKV_WIDTH
    for l in range(depth):
        mod = ada_modulation(cond, ada_w[l], row(ada_b[l])).reshape(8, 6, d)
        mods = jnp.stack([jnp.broadcast_to(mod[bsz], (bsz, 6, d)), mod[:bsz]], axis=1)
        g1 = row(norm1_g[l])
        w_in_l = w_in[l].astype(BF16)
        p_a, p_b, p_c, p_d = (in_proj(xa, g1, mods, w_in_l[:, lo:hi])
                              for lo, hi in ((0, A_END), (A_END, B_END), (B_END, C_END), (C_END, IN_WIDTH)))
        half = jnp.zeros((RWKV_DECAY_RANK, w), F32)
        w2 = jnp.stack([jnp.concatenate([rwkv_w2[l, 0], half]), jnp.concatenate([half, rwkv_w2[l, 1]])]).astype(BF16)
        a2 = jnp.stack([jnp.concatenate([rwkv_a2[l, 0], half]), jnp.concatenate([half, rwkv_a2[l, 1]])]).astype(BF16)
        vecs = jnp.stack([rwkv_kk[l], rwkv_ka[l], rwkv_rk[l].reshape(w), jnp.zeros((w,), F32),
                          rwkv_w0[l, 0], rwkv_w0[l, 1], rwkv_a0[l, 0], rwkv_a0[l, 1]])
        r, v, kk, gate_bonus, k_dir, b_dir, lw_dir = rwkv_prep(
            p_a, rwkv_shift[l], w2, a2, rwkv_g2[l].astype(BF16), vecs, ones_bd)
        y = rwkv_scan(r, v, kk, k_dir, b_dir, lw_dir, n_ctx)
        ya = rwkv_out(y, gate_bonus, jnp.stack([rwkv_gn_g[l], rwkv_gn_b[l]]), ones_bd)
        yb = conformer_conv(p_b, conv_dw[l], conv_db[l], conv_ln_g[l], conv_ln_b[l])
        eye = jnp.eye(S5_GROUPS, dtype=F32)
        n_state = S5_GROUPS * S5_STATE
        y_dirs = []
        for dr in range(2):
            ab_re, ab_im, bb_re, bb_im = _s5_discretise(s5_lam_re[l, dr], s5_lam_im[l, dr], s5_log_dt[l, dr],
                                                        s5_b_re[l, dr], s5_b_im[l, dr])
            blk_in = lambda bb: jnp.einsum("gph,gk->ghkp", bb, eye).reshape(S5_WIDTH, n_state).astype(BF16)
            blk_out = lambda cc: jnp.einsum("ghp,gk->kpgh", cc.astype(F32), eye).reshape(n_state, S5_WIDTH)
            c_cat = jnp.concatenate([blk_out(s5_c_re[l, dr]), -blk_out(s5_c_im[l, dr])], axis=0).astype(BF16)
            y_dirs.append(s5_scan(p_c, blk_in(bb_re), blk_in(bb_im), c_cat, ab_re.reshape(1, n_state),
                                  ab_im.reshape(1, n_state), n_ctx, reverse=(dr == 1)))
        yc = s5_out(p_c, y_dirs[0], y_dirs[1], row(s5_d[l]), s5_glu_w[l].astype(BF16), row(s5_glu_b[l]))
        yd = retention_out(retention_scan(p_d, cos2, sin2, n_ctx), p_d, ret_gn_g[l], ret_gn_b[l])
        xa = merge_residual(xa, g1, mods, (ya, yb, yc, yd), w_merge[l].astype(BF16), b_merge[l],
                            w_branch[l].astype(BF16), w_out[l].astype(BF16))
        h = norm_modulate2(xa, row(norm2_g[l]), mods)
        f = _peer_ffn(h.reshape(-1, d), peer_wq[l], peer_keys[l],
                      pack_bf16_pairs(peer_u[l]), pack_bf16_pairs(peer_v[l])).reshape(bsz, -1, d)
        xa = residual2(xa, f, mods, row(final_g), final_norm=(l == depth - 1))
    return xa[:, n_ctx:]
```

```python
import functools
import math

import jax
import jax.numpy as jnp
from jax import lax
from jax.experimental import pallas as pl
from jax.experimental.pallas import tpu as pltpu
from jax.experimental.pallas import tpu_sc as plsc

F32 = jnp.float32
BF16 = jnp.bfloat16

D_MODEL = 1024
GRID_W = 64
EPS = 1e-6

RWKV_HEADS = 8
RWKV_HEAD_DIM = 64
RWKV_WIDTH = RWKV_HEADS * RWKV_HEAD_DIM
RWKV_DECAY_RANK = 64
RWKV_ICLR_RANK = 64
RWKV_GATE_RANK = 128
RWKV_CHUNK = 64
LANES = 128


def _bdot(a, b, dims):
    return lax.dot_general(a.astype(BF16), b.astype(BF16), (dims, ((), ())), preferred_element_type=F32)


def _mm(a, b):
    return _bdot(a, b, ((1,), (0,)))


def _mm_nt(a, b):
    return _bdot(a, b, ((1,), (1,)))


def _mm_tn(a, b):
    return _bdot(a, b, ((0,), (0,)))


def _rwkv_chunk_kernel(r_ref, v_ref, kk_ref, k_ref, b_ref, lw_ref, y_ref, s_ref):
    c = RWKV_CHUNK
    d = pl.program_id(1)
    j = pl.program_id(2)

    @pl.when(j == 0)
    def _():
        s_ref[...] = jnp.zeros_like(s_ref)

    sign = jnp.where(d == 0, 1, -1)
    row = lax.broadcasted_iota(jnp.int32, (c, c), 0)
    col = lax.broadcasted_iota(jnp.int32, (c, c), 1)
    cum_mat = jnp.where((row - col) * sign >= 0, 1.0, 0.0).astype(BF16)

    lw = lw_ref[0, 0]
    lw_hi = lw.astype(BF16)
    rem = lw - lw_hi.astype(F32)
    lw_mid = rem.astype(BF16)
    lw_lo = (rem - lw_mid.astype(F32)).astype(BF16)
    lc_incl = _mm(cum_mat, lw_hi) + _mm(cum_mat, lw_mid) + _mm(cum_mat, lw_lo)
    lc_excl = lc_incl - lw
    lc_ref = 0.5 * jnp.sum(lw, axis=0, keepdims=True)

    row2 = lax.broadcasted_iota(jnp.int32, (2 * c, 2 * c), 0)
    col2 = lax.broadcasted_iota(jnp.int32, (2 * c, 2 * c), 1)
    same_head = (row2 // c) == (col2 // c)
    tdiff = jnp.where(same_head, ((row2 % c) - (col2 % c)) * sign, -1)
    strict2 = tdiff > 0
    incl2 = tdiff >= 0
    eye2 = jnp.where(row2 == col2, 1.0, 0.0)
    lane_head = lax.broadcasted_iota(jnp.int32, (c, LANES), 1) // RWKV_HEAD_DIM

    def stack2(z):
        return jnp.concatenate([jnp.where(lane_head == 0, z, 0.0), jnp.where(lane_head == 1, z, 0.0)], axis=0)

    pairs = range(RWKV_WIDTH // LANES)
    lane = lambda p: slice(p * LANES, (p + 1) * LANES)
    bf = lambda z: z.astype(BF16)
    x, q0_stack, v2, kb_end, decay_c, s0 = [], [], [], [], [], []
    for p in pairs:
        sl = lane(p)
        r, v, kk = r_ref[0, :, sl], v_ref[0, :, sl], kk_ref[0, :, sl]
        k, b = k_ref[0, 0, :, sl], b_ref[0, 0, :, sl]
        ref = lc_ref[:, sl]
        e_in = jnp.exp(lc_incl[:, sl] - ref)
        e_ex = jnp.exp(lc_excl[:, sl] - ref)
        e_inv = jnp.exp(ref - lc_incl[:, sl])
        e_ref = jnp.exp(ref)
        e_end = e_inv * e_ref
        q_stack = jnp.concatenate([stack2(kk * e_ex), stack2(r * e_in)], axis=0)
        k_stack = jnp.concatenate([stack2(k * e_inv), stack2(b * e_inv)], axis=0)
        x.append(_mm_nt(q_stack, k_stack))
        q0_stack.append(bf(jnp.concatenate([stack2(kk * (e_ex * e_ref)), stack2(r * (e_in * e_ref))], axis=0)))
        v2.append(bf(stack2(v)))
        kb_end.append(bf(jnp.concatenate([stack2(k * e_end), stack2(-(b * e_end))], axis=0)))
        decay_c.append(e_ref * e_ref)
        s0.append(s_ref[p])
    m_b = [jnp.where(strict2, x[p][:2 * c, 2 * c:], 0.0) for p in pairs]
    p_b = [bf(jnp.where(incl2, x[p][2 * c:, 2 * c:], 0.0)) for p in pairs]
    mk_pk = [bf(jnp.concatenate([jnp.where(strict2, x[p][:2 * c, :2 * c], 0.0),
                                 jnp.where(incl2, x[p][2 * c:, :2 * c], 0.0)], axis=0)) for p in pairs]
    from_state = [_mm_nt(q0_stack[p], s0[p]) for p in pairs]
    from_chunk = [_mm(mk_pk[p], v2[p]) for p in pairs]
    t_inv = [eye2 - m_b[p] for p in pairs]
    m_pow = [bf(m_b[p]) for p in pairs]
    for _ in range(int(math.log2(c)) - 1):
        m_pow = [bf(_mm(m_pow[p], m_pow[p])) for p in pairs]
        t_inv = [t_inv[p] + _mm(t_inv[p], m_pow[p]) for p in pairs]
    u2 = [_mm(t_inv[p], from_state[p][:2 * c] + from_chunk[p][:2 * c]) for p in pairs]
    y2 = [from_state[p][2 * c:] + from_chunk[p][2 * c:] - _mm(p_b[p], u2[p]) for p in pairs]
    s_new = [s0[p] * decay_c[p] + _mm_tn(jnp.concatenate([v2[p], bf(u2[p])], axis=0), kb_end[p]) for p in pairs]
    for p in pairs:
        y_ref[0, 0, :, lane(p)] = y2[p][:c] + y2[p][c:]
        s_ref[p] = s_new[p]


def rwkv_scan(r, v, kk, k_dir, b_dir, lw_dir, n_ctx):
    bsz, length, width = r.shape
    c = RWKV_CHUNK
    nc, nc_ctx = length // c, n_ctx // c

    def chunk_of(d, j):
        back = jnp.where(j < nc_ctx, nc_ctx - 1 - j, nc + nc_ctx - 1 - j)
        return jnp.where(d == 0, j, back)

    shared = pl.BlockSpec((1, c, width), lambda b, d, j: (b, chunk_of(d, j), 0))
    per_dir = pl.BlockSpec((1, 1, c, width), lambda b, d, j: (b, d, chunk_of(d, j), 0))
    return pl.pallas_call(
        _rwkv_chunk_kernel,
        grid=(bsz, 2, nc),
        in_specs=[shared, shared, shared, per_dir, per_dir, per_dir],
        out_specs=per_dir,
        out_shape=jax.ShapeDtypeStruct((bsz, 2, length, width), F32),
        scratch_shapes=[pltpu.VMEM((width // LANES, LANES, LANES), F32)],
        compiler_params=pltpu.CompilerParams(dimension_semantics=("arbitrary", "arbitrary", "arbitrary")),
        name="rwkv_scan",
    )(r, v, kk, k_dir, b_dir, lw_dir)


S5_CHUNK = 128


def _s5_chunk_kernel(u_ref, bre_ref, bim_ref, c_ref, are_ref, aim_ref, y_ref, carry_ref, *, reverse):
    tc = S5_CHUNK
    j = pl.program_id(1)

    @pl.when(j == 0)
    def _():
        carry_ref[...] = jnp.zeros_like(carry_ref)

    u = u_ref[0]
    xr = _mm(u, bre_ref[...])
    xi = _mm(u, bim_ref[...])
    pw_r, pw_i = are_ref[...], aim_ref[...]
    a_pow = lambda n: (pw_r[8 - n:9 - n], pw_i[8 - n:9 - n]) if reverse else (pw_r[n - 1:n], pw_i[n - 1:n])
    sub = lax.broadcasted_iota(jnp.int32, xr.shape, 0) % 8
    for sh in (1, 2, 4):
        ar, ai = a_pow(sh)
        if reverse:
            sr, si = pltpu.roll(xr, tc - sh, 0), pltpu.roll(xi, tc - sh, 0)
            keep = sub < 8 - sh
        else:
            sr, si = pltpu.roll(xr, sh, 0), pltpu.roll(xi, sh, 0)
            keep = sub >= sh
        sr, si = jnp.where(keep, sr, 0.0), jnp.where(keep, si, 0.0)
        xr, xi = xr + (ar * sr - ai * si), xi + (ar * si + ai * sr)
    a8r, a8i = a_pow(8)
    cr, ci = carry_ref[0:1, :], carry_ref[1:2, :]
    n_groups = tc // 8
    enter_r, enter_i = [None] * n_groups, [None] * n_groups
    for g in (range(n_groups - 1, -1, -1) if reverse else range(n_groups)):
        enter_r[g], enter_i[g] = cr, ci
        close = 8 * g if reverse else 8 * g + 7
        cr, ci = (xr[close:close + 1, :] + (a8r * cr - a8i * ci), xi[close:close + 1, :] + (a8r * ci + a8i * cr))
    carry_ref[0:1, :] = cr
    carry_ref[1:2, :] = ci
    er = jnp.concatenate([jnp.broadcast_to(z, (8, z.shape[1])) for z in enter_r], axis=0)
    ei = jnp.concatenate([jnp.broadcast_to(z, (8, z.shape[1])) for z in enter_i], axis=0)
    tr, ti = jnp.tile(pw_r, (n_groups, 1)), jnp.tile(pw_i, (n_groups, 1))
    xr, xi = xr + (tr * er - ti * ei), xi + (tr * ei + ti * er)
    y_ref[0] = _mm(jnp.concatenate([xr, xi], axis=1), c_ref[...])


def s5_scan(u, b_re, b_im, c_cat, a_re, a_im, n_ctx, reverse):
    bsz, length, width = u.shape
    n_state = b_re.shape[1]
    tc = S5_CHUNK
    nc, nc_ctx = length // tc, n_ctx // tc

    def chunk_of(j):
        if not reverse:
            return j
        return jnp.where(j < nc_ctx, nc_ctx - 1 - j, nc + nc_ctx - 1 - j)

    tok = pl.BlockSpec((1, tc, width), lambda b, j: (b, chunk_of(j), 0))
    full = lambda shape: pl.BlockSpec(shape, lambda b, j: (0,) * len(shape))
    return pl.pallas_call(
        functools.partial(_s5_chunk_kernel, reverse=reverse),
        grid=(bsz, nc),
        in_specs=[tok, full(b_re.shape), full(b_im.shape), full(c_cat.shape), full(a_re.shape), full(a_im.shape)],
        out_specs=tok,
        out_shape=jax.ShapeDtypeStruct((bsz, length, width), F32),
        scratch_shapes=[pltpu.VMEM((8, n_state), F32)],
        compiler_params=pltpu.CompilerParams(dimension_semantics=("arbitrary", "arbitrary"),
                                             vmem_limit_bytes=48 * 1024 * 1024),
        name="s5_scan_rev" if reverse else "s5_scan_fwd",
    )(u, b_re, b_im, c_cat, a_re, a_im)


RET_HEADS = 4
RET_HEAD_DIM = 128
RET_WIDTH = RET_HEADS * RET_HEAD_DIM
RET_CHUNK = 128
ROPE_BASE = 10000.0
ROW_TILE = 256


def _ret_chunk_kernel(q_ref, k_ref, v_ref, cos_ref, sin_ref, dm_ref, dq_ref, dk_ref, o_ref, s_ref, *, chunk_decay):
    j = pl.program_id(2)

    @pl.when(j == 0)
    def _():
        s_ref[...] = jnp.zeros_like(s_ref)

    cos2, sin2 = cos_ref[...], sin_ref[...]
    rope = lambda z: z * cos2 + pltpu.roll(z, RET_HEAD_DIM // 2, 1) * sin2
    for h in range(RET_HEADS):
        sl = slice(h * RET_HEAD_DIM, (h + 1) * RET_HEAD_DIM)
        q = rope(q_ref[0, :, sl])
        k = rope(k_ref[0, :, sl] * (RET_HEAD_DIM ** -0.5))
        v = v_ref[0, :, sl]
        s0 = s_ref[h]
        scores = _mm_nt(q, k) * dm_ref[0, h]
        o_ref[0, 0, :, sl] = _mm(scores, v) + _mm(q * dq_ref[0, h], s0)
        s_ref[h] = chunk_decay[h] * s0 + _mm_tn(k * dk_ref[0, h], v)


def _ret_decay_tables():
    c = RET_CHUNK
    lg = jnp.log1p(-jnp.exp2(-5.0 - jnp.arange(RET_HEADS, dtype=F32)))[:, None, None]
    n = jnp.arange(c, dtype=F32)[:, None]
    m = jnp.arange(c, dtype=F32)[None, :]
    fwd = jnp.where(n >= m, jnp.exp(jnp.where(n >= m, n - m, 0.0) * lg), 0.0)
    bwd = jnp.where(m > n, jnp.exp(jnp.where(m > n, m - n, 0.0) * lg), 0.0)
    ones = jnp.ones((1, c), F32)
    dq = jnp.stack([jnp.exp((n + 1.0) * lg) * ones, jnp.exp((c - n) * lg) * ones])
    dk = jnp.stack([jnp.exp((c - 1.0 - n) * lg) * ones, jnp.exp(n * lg) * ones])
    return jnp.stack([fwd, bwd]), dq, dk


def retention_scan(p_ret, cos2, sin2, n_ctx):
    bsz, length, _ = p_ret.shape
    c, w = RET_CHUNK, RET_WIDTH
    nc, nc_ctx = length // c, n_ctx // c
    dm, dq, dk = _ret_decay_tables()
    chunk_decay = tuple(math.exp(c * math.log1p(-2.0 ** (-5 - h))) for h in range(RET_HEADS))

    def chunk_of(d, j):
        back = jnp.where(j < nc_ctx, nc_ctx - 1 - j, nc + nc_ctx - 1 - j)
        return jnp.where(d == 0, j, back)

    col = lambda which: pl.BlockSpec((1, c, w), lambda b, d, j: (b, chunk_of(d, j), which(d)))
    rope_spec = pl.BlockSpec((c, RET_HEAD_DIM), lambda b, d, j: (chunk_of(d, j), 0))
    table = pl.BlockSpec((1, RET_HEADS, c, c), lambda b, d, j: (d, 0, 0, 0))
    return pl.pallas_call(
        functools.partial(_ret_chunk_kernel, chunk_decay=chunk_decay),
        grid=(bsz, 2, nc),
        in_specs=[col(lambda d: 0), col(lambda d: 1 + d), col(lambda d: 3), rope_spec, rope_spec, table, table, table],
        out_specs=pl.BlockSpec((1, 1, c, w), lambda b, d, j: (b, d, chunk_of(d, j), 0)),
        out_shape=jax.ShapeDtypeStruct((bsz, 2, length, w), F32),
        scratch_shapes=[pltpu.VMEM((RET_HEADS, RET_HEAD_DIM, RET_HEAD_DIM), F32)],
        compiler_params=pltpu.CompilerParams(dimension_semantics=("arbitrary", "arbitrary", "arbitrary")),
        name="retention_scan",
    )(p_ret, p_ret, p_ret, cos2, sin2, dm, dq, dk)


def _ret_out_kernel(o_ref, g_ref, gn_g_ref, gn_b_ref, y_ref):
    o = o_ref[0, 0] + o_ref[0, 1]
    g = g_ref[0]
    for h in range(RET_HEADS):
        sl = slice(h * RET_HEAD_DIM, (h + 1) * RET_HEAD_DIM)
        z = o[:, sl]
        zc = z - jnp.mean(z, axis=1, keepdims=True)
        zn = zc * lax.rsqrt(jnp.mean(zc * zc, axis=1, keepdims=True) + EPS)
        gate = g[:, sl]
        y_ref[0, :, sl] = (zn * gn_g_ref[:, sl] + gn_b_ref[:, sl]) * (gate * jax.nn.sigmoid(gate))


def retention_out(o, p_ret, gn_g, gn_b):
    bsz, _, length, w = o.shape
    tm = ROW_TILE
    vec = pl.BlockSpec((1, w), lambda b, i: (0, 0))
    return pl.pallas_call(
        _ret_out_kernel,
        grid=(bsz, length // tm),
        in_specs=[pl.BlockSpec((1, 2, tm, w), lambda b, i: (b, 0, i, 0)),
                  pl.BlockSpec((1, tm, w), lambda b, i: (b, i, 4)), vec, vec],
        out_specs=pl.BlockSpec((1, tm, w), lambda b, i: (b, i, 0)),
        out_shape=jax.ShapeDtypeStruct((bsz, length, w), F32),
        compiler_params=pltpu.CompilerParams(dimension_semantics=("arbitrary", "arbitrary")),
        name="retention_out",
    )(o, p_ret, gn_g.reshape(1, w), gn_b.reshape(1, w))


def rope_tables(n_tokens, n_ctx):
    rows = n_tokens // GRID_W
    row = jnp.repeat(jnp.arange(rows, dtype=F32), GRID_W)
    col = jnp.tile(jnp.arange(GRID_W, dtype=F32), rows)
    n_freq = RET_HEAD_DIM // 4
    inv = ROPE_BASE ** (-jnp.arange(n_freq, dtype=F32) / n_freq)
    ang = jnp.concatenate([row[:, None] * inv, col[:, None] * inv], axis=-1)
    cos, sin = jnp.cos(ang), jnp.sin(ang)
    cos2 = jnp.concatenate([jnp.ones((n_ctx, RET_HEAD_DIM), F32), jnp.concatenate([cos, cos], axis=-1)], axis=0)
    sin2 = jnp.concatenate([jnp.zeros((n_ctx, RET_HEAD_DIM), F32), jnp.concatenate([-sin, sin], axis=-1)], axis=0)
    return cos2, sin2


VMEM_LIMIT = 56 * 1024 * 1024


def _const_spec(a):
    return pl.BlockSpec(a.shape, lambda b, i: (0,) * a.ndim, pipeline_mode=pl.Buffered(1))


def _mod_spec(d):
    return pl.BlockSpec((1, 1, 6, d), lambda b, i: (b, jnp.minimum(i, 1), 0, 0))


def _tok_spec(width, col=0):
    return pl.BlockSpec((1, ROW_TILE, width), lambda b, i: (b, i, col))


def _norm_mod(x, g, mod, shift_row, scale_row):
    y = x * lax.rsqrt(jnp.mean(x * x, axis=1, keepdims=True) + EPS) * g
    return y * (1.0 + mod[scale_row:scale_row + 1]) + mod[shift_row:shift_row + 1]


def _in_proj_kernel(x_ref, g_ref, mod_ref, w_ref, o_ref):
    n = _norm_mod(x_ref[0], g_ref[...], mod_ref[0, 0], 0, 1)
    o_ref[0] = _mm(n, w_ref[...])


def in_proj(x, norm_g, mods, w):
    bsz, length, d = x.shape
    n_out = w.shape[1]
    return pl.pallas_call(
        _in_proj_kernel,
        grid=(bsz, length // ROW_TILE),
        in_specs=[_tok_spec(d), _const_spec(norm_g), _mod_spec(d), _const_spec(w)],
        out_specs=_tok_spec(n_out),
        out_shape=jax.ShapeDtypeStruct((bsz, length, n_out), F32),
        compiler_params=pltpu.CompilerParams(dimension_semantics=("arbitrary", "arbitrary"),
                                             vmem_limit_bytes=VMEM_LIMIT),
        name="in_proj",
    )(x, norm_g, mods, w)


ADA_COLS = 1024


def _ada_kernel(c_ref, w_ref, b_ref, o_ref):
    cv = c_ref[...]
    o_ref[...] = _mm(cv * jax.nn.sigmoid(cv), w_ref[...]) + b_ref[...]


def ada_modulation(cond, w, b):
    rows, d = cond.shape
    n_out = w.shape[1]
    return pl.pallas_call(
        _ada_kernel,
        grid=(n_out // ADA_COLS,),
        in_specs=[pl.BlockSpec((rows, d), lambda j: (0, 0)), pl.BlockSpec((d, ADA_COLS), lambda j: (0, j)),
                  pl.BlockSpec((1, ADA_COLS), lambda j: (0, j))],
        out_specs=pl.BlockSpec((rows, ADA_COLS), lambda j: (0, j)),
        out_shape=jax.ShapeDtypeStruct((rows, n_out), F32),
        compiler_params=pltpu.CompilerParams(dimension_semantics=("arbitrary",)),
        name="ada_modulation",
    )(cond, w, b)


def _norm_mod_kernel(x_ref, g_ref, mod_ref, o_ref):
    o_ref[0] = _norm_mod(x_ref[0], g_ref[...], mod_ref[0, 0], 3, 4)


def norm_modulate2(x, norm_g, mods):
    bsz, length, d = x.shape
    return pl.pallas_call(
        _norm_mod_kernel,
        grid=(bsz, length // ROW_TILE),
        in_specs=[_tok_spec(d), _const_spec(norm_g), _mod_spec(d)],
        out_specs=_tok_spec(d),
        out_shape=jax.ShapeDtypeStruct((bsz, length, d), F32),
        compiler_params=pltpu.CompilerParams(dimension_semantics=("arbitrary", "arbitrary")),
        name="norm_modulate2",
    )(x, norm_g, mods)


def _residual_kernel(x_ref, f_ref, mod_ref, g_ref, o_ref, *, final_norm):
    y = x_ref[0] + mod_ref[0, 0][5:6] * f_ref[0]
    if final_norm:
        y = y * lax.rsqrt(jnp.mean(y * y, axis=1, keepdims=True) + EPS) * g_ref[...]
    o_ref[0] = y


def residual2(x, f, mods, final_g, final_norm):
    bsz, length, d = x.shape
    return pl.pallas_call(
        functools.partial(_residual_kernel, final_norm=final_norm),
        grid=(bsz, length // ROW_TILE),
        in_specs=[_tok_spec(d), _tok_spec(d), _mod_spec(d), _const_spec(final_g)],
        out_specs=_tok_spec(d),
        out_shape=jax.ShapeDtypeStruct((bsz, length, d), F32),
        compiler_params=pltpu.CompilerParams(dimension_semantics=("arbitrary", "arbitrary")),
        name="residual2",
    )(x, f, mods, final_g)


def _merge_kernel(x_ref, g_ref, mod_ref, ya_ref, yb_ref, yc_ref, yd_ref, wg_ref, bg_ref, wbr_ref, wout_ref, o_ref):
    x = x_ref[0]
    mod = mod_ref[0, 0]
    n = _norm_mod(x, g_ref[...], mod, 0, 1).astype(BF16)
    m = jnp.zeros(x.shape, F32)
    for i, y_ref in enumerate((ya_ref, yb_ref, yc_ref, yd_ref)):
        gate = jax.nn.sigmoid(_mm(n, wg_ref[i]) + bg_ref[i:i + 1])
        m = m + gate * _mm(y_ref[0], wbr_ref[i])
    o_ref[0] = x + mod[2:3] * _mm(m, wout_ref[...])


def merge_residual(x, norm_g, mods, ys, w_merge, b_merge, w_branch, w_out):
    bsz, length, d = x.shape
    bw = ys[0].shape[-1]
    return pl.pallas_call(
        _merge_kernel,
        grid=(bsz, length // ROW_TILE),
        in_specs=[_tok_spec(d), _const_spec(norm_g), _mod_spec(d)] + [_tok_spec(bw)] * 4
        + [_const_spec(w_merge), _const_spec(b_merge), _const_spec(w_branch), _const_spec(w_out)],
        out_specs=_tok_spec(d),
        out_shape=jax.ShapeDtypeStruct((bsz, length, d), F32),
        compiler_params=pltpu.CompilerParams(dimension_semantics=("arbitrary", "arbitrary"),
                                             vmem_limit_bytes=VMEM_LIMIT),
        name="merge_residual",
    )(x, norm_g, mods, *ys, w_merge, b_merge, w_branch, w_out)


def _halo_specs(width, n_tiles):
    prev = pl.BlockSpec((1, ROW_TILE, width), lambda b, i: (b, jnp.maximum(i - 1, 0), 0))
    nxt = pl.BlockSpec((1, ROW_TILE, width), lambda b, i: (b, jnp.minimum(i + 1, n_tiles - 1), 0))
    return [prev, _tok_spec(width), nxt]


def _segment_edges(n_tiles):
    i = pl.program_id(1)
    return i >= 2, jnp.logical_and(i >= 1, i < n_tiles - 1)


CONV_CHANNELS = 512
CONV_TAPS = 31
HALO = 16


def _conformer_kernel(prev_ref, cur_ref, next_ref, dw_ref, db_ref, lng_ref, lnb_ref, o_ref, ext_ref, *, n_tiles):
    ch = CONV_CHANNELS
    glu = lambda p: p[:, :ch] * jax.nn.sigmoid(p[:, ch:])
    has_prev, has_next = _segment_edges(n_tiles)
    ext_ref[0:HALO, :] = jnp.where(has_prev, glu(prev_ref[0, ROW_TILE - HALO:, :]), 0.0)
    ext_ref[HALO:HALO + ROW_TILE, :] = glu(cur_ref[0])
    ext_ref[HALO + ROW_TILE:, :] = jnp.where(has_next, glu(next_ref[0, :HALO, :]), 0.0)
    pad = (CONV_TAPS - 1) // 2
    acc = jnp.zeros((ROW_TILE, ch), F32) + db_ref[...]
    for j in range(CONV_TAPS):
        acc = acc + ext_ref[pl.ds(HALO - pad + j, ROW_TILE), :] * dw_ref[j:j + 1, :]
    zc = acc - jnp.mean(acc, axis=1, keepdims=True)
    z = zc * lax.rsqrt(jnp.mean(zc * zc, axis=1, keepdims=True) + EPS) * lng_ref[...] + lnb_ref[...]
    o_ref[0] = z * jax.nn.sigmoid(z)


def conformer_conv(p, dw, db, ln_g, ln_b):
    bsz, length, width = p.shape
    n_tiles = length // ROW_TILE
    ch = CONV_CHANNELS
    row = lambda v: v.reshape(1, ch)
    return pl.pallas_call(
        functools.partial(_conformer_kernel, n_tiles=n_tiles),
        grid=(bsz, n_tiles),
        in_specs=_halo_specs(width, n_tiles) + [_const_spec(dw)] + [_const_spec(row(db))] * 3,
        out_specs=_tok_spec(ch),
        out_shape=jax.ShapeDtypeStruct((bsz, length, ch), F32),
        scratch_shapes=[pltpu.VMEM((ROW_TILE + 2 * HALO, ch), F32)],
        compiler_params=pltpu.CompilerParams(dimension_semantics=("arbitrary", "arbitrary"),
                                             vmem_limit_bytes=VMEM_LIMIT),
        name="conformer_conv",
    )(p, p, p, dw, row(db), row(ln_g), row(ln_b))


def _s5_out_kernel(u_ref, yf_ref, yb_ref, d_ref, w_ref, b_ref, o_ref):
    z = _gelu_tanh(d_ref[...] * u_ref[0] + yf_ref[0] + yb_ref[0])
    o_ref[0] = z * jax.nn.sigmoid(_mm(z, w_ref[...]) + b_ref[...])


def s5_out(u, y_fwd, y_bwd, d_skip, glu_w, glu_b):
    bsz, length, w = u.shape
    return pl.pallas_call(
        _s5_out_kernel,
        grid=(bsz, length // ROW_TILE),
        in_specs=[_tok_spec(w)] * 3 + [_const_spec(d_skip), _const_spec(glu_w), _const_spec(glu_b)],
        out_specs=_tok_spec(w),
        out_shape=jax.ShapeDtypeStruct((bsz, length, w), F32),
        compiler_params=pltpu.CompilerParams(dimension_semantics=("arbitrary", "arbitrary")),
        name="s5_out",
    )(u, y_fwd, y_bwd, d_skip, glu_w, glu_b)


def _head_sums(z, ones_bd):
    hi = z.astype(BF16)
    lo = (z - hi.astype(F32)).astype(BF16)
    return _mm(hi, ones_bd) + _mm(lo, ones_bd)


def _rwkv_prep_kernel(prev_ref, cur_ref, next_ref, shift_ref, w2_ref, a2_ref, g2_ref, vec_ref, bd_ref,
                      r_ref, v_ref, kk_ref, gb_ref, k_ref, b_ref, lw_ref, ext_ref, *, n_tiles):
    w = RWKV_WIDTH
    has_prev, has_next = _segment_edges(n_tiles)
    ext_ref[0:8, :] = jnp.where(has_prev, prev_ref[0, ROW_TILE - 8:, :], 0.0)
    ext_ref[8:8 + ROW_TILE, :] = cur_ref[0]
    ext_ref[8 + ROW_TILE:, :] = jnp.where(has_next, next_ref[0, :8, :], 0.0)
    p = (ext_ref[pl.ds(7, ROW_TILE), :] * shift_ref[0:1, :] + ext_ref[pl.ds(8, ROW_TILE), :] * shift_ref[1:2, :]
         + ext_ref[pl.ds(9, ROW_TILE), :] * shift_ref[2:3, :])
    r, k, v = p[:, :w], p[:, w:2 * w], p[:, 2 * w:3 * w]
    lowrank_w = jnp.tanh(p[:, 3 * w:3 * w + LANES])
    lowrank_a = p[:, 3 * w + LANES:3 * w + 2 * LANES]
    gl = p[:, 3 * w + 2 * LANES:]
    ones_bd = bd_ref[...]
    kk = k * vec_ref[0:1, :]
    kk = kk * lax.rsqrt(_head_sums(kk * kk, ones_bd) + EPS)
    r_ref[0], v_ref[0], kk_ref[0] = r, v, kk
    gb_ref[0, 0] = _mm(jax.nn.sigmoid(gl), g2_ref[...])
    k_sum = jnp.zeros_like(k)
    for d in range(2):
        w_log = -jax.nn.softplus(-(vec_ref[4 + d:5 + d, :] + _mm(lowrank_w, w2_ref[d]))) - 0.5
        a = jax.nn.sigmoid(vec_ref[6 + d:7 + d, :] + _mm(lowrank_a, a2_ref[d]))
        k_d = k * (1.0 + (a - 1.0) * vec_ref[1:2, :])
        k_ref[0, d], b_ref[0, d], lw_ref[0, d] = k_d, a * kk, -jnp.exp(w_log)
        k_sum = k_sum + k_d
    gb_ref[0, 1] = _head_sums(r * k_sum * vec_ref[2:3, :], ones_bd) * v


def rwkv_prep(p, shift_w, w2, a2, g2, vecs, ones_bd):
    bsz, length, width = p.shape
    n_tiles = length // ROW_TILE
    w = RWKV_WIDTH
    shared = jax.ShapeDtypeStruct((bsz, length, w), F32)
    per_dir = jax.ShapeDtypeStruct((bsz, 2, length, w), F32)
    dir_spec = pl.BlockSpec((1, 2, ROW_TILE, w), lambda b, i: (b, 0, i, 0))
    return pl.pallas_call(
        functools.partial(_rwkv_prep_kernel, n_tiles=n_tiles),
        grid=(bsz, n_tiles),
        in_specs=_halo_specs(width, n_tiles) + [_const_spec(a) for a in (shift_w, w2, a2, g2, vecs, ones_bd)],
        out_specs=[_tok_spec(w)] * 3 + [dir_spec] * 4,
        out_shape=[shared] * 3 + [per_dir] * 4,
        scratch_shapes=[pltpu.VMEM((ROW_TILE + 16, width), F32)],
        compiler_params=pltpu.CompilerParams(dimension_semantics=("arbitrary", "arbitrary"),
                                             vmem_limit_bytes=VMEM_LIMIT),
        name="rwkv_prep",
    )(p, p, p, shift_w, w2, a2, g2, vecs, ones_bd)


def _rwkv_out_kernel(y_ref, gb_ref, gn_ref, bd_ref, o_ref):
    y = y_ref[0, 0] + y_ref[0, 1]
    ones_bd = bd_ref[...]
    inv = 1.0 / RWKV_HEAD_DIM
    yc = y - _head_sums(y, ones_bd) * inv
    yn = yc * lax.rsqrt(_head_sums(yc * yc, ones_bd) * inv + EPS)
    o_ref[0] = (yn * gn_ref[0:1, :] + gn_ref[1:2, :] + gb_ref[0, 1]) * gb_ref[0, 0]


def rwkv_out(y, gate_bonus, gn, ones_bd):
    bsz, _, length, w = y.shape
    dir_spec = pl.BlockSpec((1, 2, ROW_TILE, w), lambda b, i: (b, 0, i, 0))
    return pl.pallas_call(
        _rwkv_out_kernel,
        grid=(bsz, length // ROW_TILE),
        in_specs=[dir_spec, dir_spec, _const_spec(gn), _const_spec(ones_bd)],
        out_specs=_tok_spec(w),
        out_shape=jax.ShapeDtypeStruct((bsz, length, w), F32),
        compiler_params=pltpu.CompilerParams(dimension_semantics=("arbitrary", "arbitrary")),
        name="rwkv_out",
    )(y, gate_bonus, gn, ones_bd)


SC_CORES = 2
SC_SUBCORES = 16
SC_WORKERS = SC_CORES * SC_SUBCORES
GATHER_ROWS = 64
PEER_SLOTS = 128
PEER_TOKENS_PER_STEP = 16


PEER_HEADS = 8
PEER_KEYS = 128
PEER_TOPK = 16
PEER_HALF = 128
PEER_SELECT_TOKENS = 128


def _top_rows(s, payload=None):
    n_rows = s.shape[0]
    iota = lax.broadcasted_iota(jnp.int32, s.shape, 0).astype(F32)
    vals, picks = [], []
    for _ in range(PEER_TOPK):
        m = jnp.max(s, axis=0, keepdims=True)
        pos = jnp.min(jnp.where(s == m, iota, float(n_rows)), axis=0, keepdims=True)
        hit = iota == pos
        vals.append(m)
        picks.append(pos if payload is None else jnp.max(jnp.where(hit, payload, -1.0), axis=0, keepdims=True))
        s = jnp.where(hit, -jnp.inf, s)
    return jnp.concatenate(vals, axis=0), jnp.concatenate(picks, axis=0)


def _peer_select_kernel(h_ref, wq_ref, keys_ref, ids_ref, gate_ref):
    q = _mm(h_ref[...], wq_ref[...]).astype(BF16)
    for h in range(PEER_HEADS):
        halves = []
        for p in range(2):
            lst = 2 * h + p
            s = _mm_nt(keys_ref[lst], q[:, lst * PEER_HALF:(lst + 1) * PEER_HALF])
            halves.append(_top_rows(s))
        (v1, p1), (v2, p2) = halves
        sub8 = lax.broadcasted_iota(jnp.int32, (8, v1.shape[1]), 0)
        cand_rows, id_rows = [v1[0:1] + v2], [p1[0:1] * PEER_KEYS + p2]
        for i in range(1, 8):
            bound = PEER_TOPK // (i + 1)
            slab = v1[i:i + 1] + v2[0:8]
            cand_rows.append(slab if bound >= 8 else jnp.where(sub8 < bound, slab, -jnp.inf))
            id_rows.append(p1[i:i + 1] * PEER_KEYS + p2[0:8])
        cand_rows.append(v1[8:] + v2[0:1])
        id_rows.append(p1[8:] * PEER_KEYS + p2[0:1])
        best, ids = _top_rows(jnp.concatenate(cand_rows, axis=0), jnp.concatenate(id_rows, axis=0))
        e = jnp.exp(best - best[0:1])
        ids_ref[h * PEER_TOPK:(h + 1) * PEER_TOPK, :] = ids.astype(jnp.int32)
        gate_ref[h * PEER_TOPK:(h + 1) * PEER_TOPK, :] = e / jnp.sum(e, axis=0, keepdims=True)


def peer_select(h, wq, keys):
    n, d = h.shape
    tn = PEER_SELECT_TOKENS
    slots = PEER_HEADS * PEER_TOPK
    full = lambda a: pl.BlockSpec(a.shape, lambda i: (0,) * a.ndim)
    out = pl.BlockSpec((slots, tn), lambda i: (0, i))
    return pl.pallas_call(
        _peer_select_kernel,
        grid=(n // tn,),
        in_specs=[pl.BlockSpec((tn, d), lambda i: (i, 0)), full(wq), full(keys)],
        out_specs=[out, out],
        out_shape=[jax.ShapeDtypeStruct((slots, n), jnp.int32), jax.ShapeDtypeStruct((slots, n), F32)],
        compiler_params=pltpu.CompilerParams(dimension_semantics=("arbitrary",),
                                             vmem_limit_bytes=48 * 1024 * 1024),
        name="peer_select",
    )(h, wq, keys)


def pack_bf16_pairs(table):
    half = table.shape[1] // 2
    bits = lax.bitcast_convert_type(table.astype(BF16), jnp.uint16).astype(jnp.uint32)
    return bits[:, :half] | (bits[:, half:] << 16)


def sc_gather_rows(table, idx):
    n_rows, width = idx.shape[0], table.shape[1]
    per_worker = n_rows // SC_WORKERS
    n_pairs = per_worker // (2 * GATHER_ROWS)
    assert per_worker * SC_WORKERS == n_rows and n_pairs * 2 * GATHER_ROWS == per_worker
    mesh = plsc.VectorSubcoreMesh(core_axis_name="c", subcore_axis_name="s",
                                  num_cores=SC_CORES, num_subcores=SC_SUBCORES)

    @functools.partial(
        pl.kernel, mesh=mesh,
        out_type=jax.ShapeDtypeStruct((n_rows, width), table.dtype),
        scratch_types=[pltpu.VMEM((2, GATHER_ROWS), jnp.int32),
                       pltpu.VMEM((2, GATHER_ROWS, width), table.dtype),
                       pltpu.SemaphoreType.DMA((2,)),
                       pltpu.SemaphoreType.DMA((2,))],
        name="peer_sc_gather",
    )
    def gather(table_hbm, idx_hbm, out_hbm, idx_v, rows_v, gather_sem, write_sem):
        worker = lax.axis_index("s") * SC_CORES + lax.axis_index("c")
        base = worker * per_worker

        def rows_of(chunk):
            return pl.ds(pl.multiple_of(base + chunk * GATHER_ROWS, GATHER_ROWS), GATHER_ROWS)

        def gather_copy(slot):
            return pltpu.make_async_copy(table_hbm.at[idx_v.at[slot]], rows_v.at[slot], gather_sem.at[slot])

        def write_copy(chunk, slot):
            return pltpu.make_async_copy(rows_v.at[slot], out_hbm.at[rows_of(chunk)], write_sem.at[slot])

        def start_gather(chunk, slot):
            pltpu.sync_copy(idx_hbm.at[rows_of(chunk)], idx_v.at[slot])
            gather_copy(slot).start()

        start_gather(0, 0)

        @pl.loop(0, n_pairs)
        def _(g):
            even, odd = 2 * g, 2 * g + 1

            @pl.when(g > 0)
            def _():
                write_copy(odd - 2, 1).wait()
            start_gather(odd, 1)
            gather_copy(0).wait()
            write_copy(even, 0).start()
            gather_copy(1).wait()
            write_copy(odd, 1).start()
            write_copy(even, 0).wait()

            @pl.when(g + 1 < n_pairs)
            def _():
                start_gather(even + 2, 0)

        write_copy(2 * n_pairs - 1, 1).wait()

    return gather(table, idx)


def _unpack_pairs(words):
    lo = pltpu.bitcast(words << 16, F32)
    hi = pltpu.bitcast(words & jnp.uint32(0xFFFF0000), F32)
    return lo, hi


def _gelu_tanh(x):
    return 0.5 * x * (1.0 + jnp.tanh(0.7978845608028654 * (x + 0.044715 * (x * x * x))))


def _peer_expert_kernel(z_ref, gate_ref, ug_ref, vg_ref, o_ref):
    half = ug_ref.shape[2]
    gate_t = gate_ref[...].T
    for n in range(PEER_TOKENS_PER_STEP):
        z_lo, z_hi = z_ref[n:n + 1, :half], z_ref[n:n + 1, half:]
        u_lo, u_hi = _unpack_pairs(ug_ref[n])
        prod = u_lo * z_lo + u_hi * z_hi
        act = jnp.sum(prod, axis=1, keepdims=True)
        w = _gelu_tanh(act) * gate_t[:, n:n + 1]
        v_lo, v_hi = _unpack_pairs(vg_ref[n])
        o_ref[n:n + 1, :half] = jnp.sum(w * v_lo, axis=0, keepdims=True)
        o_ref[n:n + 1, half:] = jnp.sum(w * v_hi, axis=0, keepdims=True)


def peer_experts(z, gate, ug, vg):
    n, d = z.shape
    tn = PEER_TOKENS_PER_STEP
    slots, half = ug.shape[1], ug.shape[2]
    return pl.pallas_call(
        _peer_expert_kernel,
        grid=(n // tn,),
        in_specs=[pl.BlockSpec((tn, d), lambda i: (i, 0)),
                  pl.BlockSpec((tn, slots), lambda i: (i, 0)),
                  pl.BlockSpec((tn, slots, half), lambda i: (i, 0, 0)),
                  pl.BlockSpec((tn, slots, half), lambda i: (i, 0, 0))],
        out_specs=pl.BlockSpec((tn, d), lambda i: (i, 0)),
        out_shape=jax.ShapeDtypeStruct((n, d), F32),
        compiler_params=pltpu.CompilerParams(dimension_semantics=("arbitrary",),
                                             vmem_limit_bytes=48 * 1024 * 1024),
        name="peer_experts",
    )(z, gate, ug, vg)


S5_WIDTH = 512
S5_GROUP = 16
S5_GROUPS = S5_WIDTH // S5_GROUP
S5_STATE = 64
S5_MAX_RE = -1e-4
A_END = 3 * RWKV_WIDTH + 2 * RWKV_DECAY_RANK + 2 * RWKV_ICLR_RANK + RWKV_GATE_RANK
B_END = A_END + 2 * CONV_CHANNELS
C_END = B_END + S5_WIDTH
IN_WIDTH = C_END + 5 * RET_WIDTH


def _s5_discretise(lam_re, lam_im, log_dt, b_re, b_im):
    lam_re = jnp.minimum(lam_re.astype(F32), S5_MAX_RE)
    lam_im = lam_im.astype(F32)
    dt = jnp.exp(log_dt.astype(F32))[:, None]
    mag = jnp.exp(lam_re * dt)
    ang = lam_im * dt
    ab_re, ab_im = mag * jnp.cos(ang), mag * jnp.sin(ang)
    den = lam_re * lam_re + lam_im * lam_im
    nr, ni = ab_re - 1.0, ab_im
    f_re = (nr * lam_re + ni * lam_im) / den
    f_im = (ni * lam_re - nr * lam_im) / den
    b_re, b_im = b_re.astype(F32), b_im.astype(F32)
    bb_re = f_re[..., None] * b_re - f_im[..., None] * b_im
    bb_im = f_re[..., None] * b_im + f_im[..., None] * b_re
    return ab_re, ab_im, bb_re, bb_im


def _peer_ffn(h, w_q, sub_keys, u_packed, v_packed):
    n, d = h.shape
    keys = sub_keys.reshape(2 * PEER_HEADS, PEER_KEYS, PEER_HALF).astype(BF16)
    wq = w_q.astype(BF16)
    step = 2 * SC_WORKERS * GATHER_ROWS
    n_blk = next(k for k in (8, 6, 4, 3, 2, 1)
                 if n % (k * PEER_SELECT_TOKENS) == 0 and (n // k * PEER_SLOTS) % step == 0)
    nb = n // n_blk
    outs = []
    for i in range(n_blk):
        h_b = h[i * nb:(i + 1) * nb]
        ids_t, gate_t = peer_select(h_b, wq, keys)
        flat_ids = ids_t.T.reshape(-1)
        ug = sc_gather_rows(u_packed, flat_ids).reshape(nb, PEER_SLOTS, d // 2)
        vg = sc_gather_rows(v_packed, flat_ids).reshape(nb, PEER_SLOTS, d // 2)
        outs.append(peer_experts(h_b, gate_t.T, ug, vg))
    return jnp.concatenate(outs, axis=0)


def kernel(x, c, ctx, c_ctx, ada_w, ada_b, norm1_g, norm2_g, w_in, rwkv_shift, rwkv_w0, rwkv_w2,
           rwkv_a0, rwkv_a2, rwkv_g2, rwkv_kk, rwkv_ka, rwkv_rk, rwkv_gn_g, rwkv_gn_b,
           conv_dw, conv_db, conv_ln_g, conv_ln_b, s5_lam_re, s5_lam_im, s5_log_dt,
           s5_b_re, s5_b_im, s5_c_re, s5_c_im, s5_d, s5_glu_w, s5_glu_b, ret_gn_g, ret_gn_b,
           w_branch, w_merge, b_merge, w_out, peer_wq, peer_keys, peer_u, peer_v, final_g):
    depth = ada_w.shape[0]
    bsz, n_lat, d = x.shape
    n_ctx = ctx.shape[1]
    assert n_ctx == ROW_TILE and n_lat % ROW_TILE == 0
    xa = jnp.concatenate([ctx, x], axis=1)
    cos2, sin2 = rope_tables(n_lat, n_ctx)
    ones_bd = jnp.kron(jnp.eye(RWKV_HEADS, dtype=F32), jnp.ones((RWKV_HEAD_DIM, RWKV_HEAD_DIM), F32)).astype(BF16)
    cond = jnp.zeros((8, d), F32).at[:bsz].set(c).at[bsz].set(c_ctx)
    row = lambda v: v.reshape(1, -1)
    w = RWKV_WIDTH
    for l in range(depth):
        mod = ada_modulation(cond, ada_w[l], row(ada_b[l])).reshape(8, 6, d)
        mods = jnp.stack([jnp.broadcast_to(mod[bsz], (bsz, 6, d)), mod[:bsz]], axis=1)
        g1 = row(norm1_g[l])
        w_in_l = w_in[l].astype(BF16)
        p_a, p_b, p_c, p_d = (in_proj(xa, g1, mods, w_in_l[:, lo:hi])
                              for lo, hi in ((0, A_END), (A_END, B_END), (B_END, C_END), (C_END, IN_WIDTH)))
        half = jnp.zeros((RWKV_DECAY_RANK, w), F32)
        w2 = jnp.stack([jnp.concatenate([rwkv_w2[l, 0], half]), jnp.concatenate([half, rwkv_w2[l, 1]])]).astype(BF16)
        a2 = jnp.stack([jnp.concatenate([rwkv_a2[l, 0], half]), jnp.concatenate([half, rwkv_a2[l, 1]])]).astype(BF16)
        vecs = jnp.stack([rwkv_kk[l], rwkv_ka[l], rwkv_rk[l].reshape(w), jnp.zeros((w,), F32),
                          rwkv_w0[l, 0], rwkv_w0[l, 1], rwkv_a0[l, 0], rwkv_a0[l, 1]])
        r, v, kk, gate_bonus, k_dir, b_dir, lw_dir = rwkv_prep(
            p_a, rwkv_shift[l], w2, a2, rwkv_g2[l].astype(BF16), vecs, ones_bd)
        y = rwkv_scan(r, v, kk, k_dir, b_dir, lw_dir, n_ctx)
        ya = rwkv_out(y, gate_bonus, jnp.stack([rwkv_gn_g[l], rwkv_gn_b[l]]), ones_bd)
        yb = conformer_conv(p_b, conv_dw[l], conv_db[l], conv_ln_g[l], conv_ln_b[l])
        eye = jnp.eye(S5_GROUPS, dtype=F32)
        n_state = S5_GROUPS * S5_STATE
        y_dirs = []
        for dr in range(2):
            ab_re, ab_im, bb_re, bb_im = _s5_discretise(s5_lam_re[l, dr], s5_lam_im[l, dr], s5_log_dt[l, dr],
                                                        s5_b_re[l, dr], s5_b_im[l, dr])
            blk_in = lambda bb: jnp.einsum("gph,gk->ghkp", bb, eye).reshape(S5_WIDTH, n_state).astype(BF16)
            blk_out = lambda cc: jnp.einsum("ghp,gk->kpgh", cc.astype(F32), eye).reshape(n_state, S5_WIDTH)
            c_cat = jnp.concatenate([blk_out(s5_c_re[l, dr]), -blk_out(s5_c_im[l, dr])], axis=0).astype(BF16)
            pw_re, pw_im = [ab_re.reshape(n_state)], [ab_im.reshape(n_state)]
            for _ in range(7):
                pw_re, pw_im = (pw_re + [pw_re[-1] * pw_re[0] - pw_im[-1] * pw_im[0]],
                                pw_im + [pw_re[-1] * pw_im[0] + pw_im[-1] * pw_re[0]])
            order = slice(None, None, -1) if dr == 1 else slice(None)
            y_dirs.append(s5_scan(p_c, blk_in(bb_re), blk_in(bb_im), c_cat, jnp.stack(pw_re[order]),
                                  jnp.stack(pw_im[order]), n_ctx, reverse=(dr == 1)))
        yc = s5_out(p_c, y_dirs[0], y_dirs[1], row(s5_d[l]), s5_glu_w[l].astype(BF16), row(s5_glu_b[l]))
        yd = retention_out(retention_scan(p_d, cos2, sin2, n_ctx), p_d, ret_gn_g[l], ret_gn_b[l])
        xa = merge_residual(xa, g1, mods, (ya, yb, yc, yd), w_merge[l].astype(BF16), b_merge[l],
                            w_branch[l].astype(BF16), w_out[l].astype(BF16))
        h = norm_modulate2(xa, row(norm2_g[l]), mods)
        f = _peer_ffn(h.reshape(-1, d), peer_wq[l], peer_keys[l],
                      pack_bf16_pairs(peer_u[l]), pack_bf16_pairs(peer_v[l])).reshape(bsz, -1, d)
        xa = residual2(xa, f, mods, row(final_g), final_norm=(l == depth - 1))
    return xa[:, n_ctx:]
```

```python
import functools
import math

import jax
import jax.numpy as jnp
from jax import lax
from jax.experimental import pallas as pl
from jax.experimental.pallas import tpu as pltpu
from jax.experimental.pallas import tpu_sc as plsc

F32 = jnp.float32
BF16 = jnp.bfloat16

D_MODEL = 1024
GRID_W = 64
EPS = 1e-6

RWKV_HEADS = 8
RWKV_HEAD_DIM = 64
RWKV_WIDTH = RWKV_HEADS * RWKV_HEAD_DIM
RWKV_DECAY_RANK = 64
RWKV_ICLR_RANK = 64
RWKV_GATE_RANK = 128
RWKV_CHUNK = 64
LANES = 128


def _bdot(a, b, dims):
    return lax.dot_general(a.astype(BF16), b.astype(BF16), (dims, ((), ())), preferred_element_type=F32)


def _mm(a, b):
    return _bdot(a, b, ((1,), (0,)))


def _mm_nt(a, b):
    return _bdot(a, b, ((1,), (1,)))


def _mm_tn(a, b):
    return _bdot(a, b, ((0,), (0,)))


def _rwkv_chunk_kernel(r_ref, v_ref, kk_ref, k_ref, b_ref, lw_ref, y_ref, s_ref):
    c = RWKV_CHUNK
    d = pl.program_id(1)
    j = pl.program_id(2)

    @pl.when(j == 0)
    def _():
        s_ref[...] = jnp.zeros_like(s_ref)

    sign = jnp.where(d == 0, 1, -1)
    row = lax.broadcasted_iota(jnp.int32, (c, c), 0)
    col = lax.broadcasted_iota(jnp.int32, (c, c), 1)
    cum_mat = jnp.where((row - col) * sign >= 0, 1.0, 0.0).astype(BF16)

    lw = lw_ref[0, 0]
    lw_hi = lw.astype(BF16)
    rem = lw - lw_hi.astype(F32)
    lw_mid = rem.astype(BF16)
    lw_lo = (rem - lw_mid.astype(F32)).astype(BF16)
    lc_incl = _mm(cum_mat, lw_hi) + _mm(cum_mat, lw_mid) + _mm(cum_mat, lw_lo)
    lc_excl = lc_incl - lw
    lc_ref = 0.5 * jnp.sum(lw, axis=0, keepdims=True)

    row2 = lax.broadcasted_iota(jnp.int32, (2 * c, 2 * c), 0)
    col2 = lax.broadcasted_iota(jnp.int32, (2 * c, 2 * c), 1)
    same_head = (row2 // c) == (col2 // c)
    tdiff = jnp.where(same_head, ((row2 % c) - (col2 % c)) * sign, -1)
    strict2 = tdiff > 0
    incl2 = tdiff >= 0
    eye2 = jnp.where(row2 == col2, 1.0, 0.0)
    lane_head = lax.broadcasted_iota(jnp.int32, (c, LANES), 1) // RWKV_HEAD_DIM

    def stack2(z):
        return jnp.concatenate([jnp.where(lane_head == 0, z, 0.0), jnp.where(lane_head == 1, z, 0.0)], axis=0)

    pairs = range(RWKV_WIDTH // LANES)
    lane = lambda p: slice(p * LANES, (p + 1) * LANES)
    bf = lambda z: z.astype(BF16)
    x, q0_stack, v2, kb_end, decay_c, s0 = [], [], [], [], [], []
    for p in pairs:
        sl = lane(p)
        r, v, kk = r_ref[0, :, sl], v_ref[0, :, sl], kk_ref[0, :, sl]
        k, b = k_ref[0, 0, :, sl], b_ref[0, 0, :, sl]
        ref = lc_ref[:, sl]
        e_in = jnp.exp(lc_incl[:, sl] - ref)
        e_ex = jnp.exp(lc_excl[:, sl] - ref)
        e_inv = jnp.exp(ref - lc_incl[:, sl])
        e_ref = jnp.exp(ref)
        e_end = e_inv * e_ref
        q_stack = jnp.concatenate([stack2(kk * e_ex), stack2(r * e_in)], axis=0)
        k_stack = jnp.concatenate([stack2(k * e_inv), stack2(b * e_inv)], axis=0)
        x.append(_mm_nt(q_stack, k_stack))
        q0_stack.append(bf(jnp.concatenate([stack2(kk * (e_ex * e_ref)), stack2(r * (e_in * e_ref))], axis=0)))
        v2.append(bf(stack2(v)))
        kb_end.append(bf(jnp.concatenate([stack2(k * e_end), stack2(-(b * e_end))], axis=0)))
        decay_c.append(e_ref * e_ref)
        s0.append(s_ref[p])
    m_b = [jnp.where(strict2, x[p][:2 * c, 2 * c:], 0.0) for p in pairs]
    p_b = [bf(jnp.where(incl2, x[p][2 * c:, 2 * c:], 0.0)) for p in pairs]
    mk_pk = [bf(jnp.concatenate([jnp.where(strict2, x[p][:2 * c, :2 * c], 0.0),
                                 jnp.where(incl2, x[p][2 * c:, :2 * c], 0.0)], axis=0)) for p in pairs]
    from_state = [_mm_nt(q0_stack[p], s0[p]) for p in pairs]
    from_chunk = [_mm(mk_pk[p], v2[p]) for p in pairs]
    t_inv = [eye2 - m_b[p] for p in pairs]
    m_pow = [bf(m_b[p]) for p in pairs]
    for _ in range(int(math.log2(c)) - 1):
        m_pow = [bf(_mm(m_pow[p], m_pow[p])) for p in pairs]
        t_inv = [t_inv[p] + _mm(t_inv[p], m_pow[p]) for p in pairs]
    u2 = [_mm(t_inv[p], from_state[p][:2 * c] + from_chunk[p][:2 * c]) for p in pairs]
    y2 = [from_state[p][2 * c:] + from_chunk[p][2 * c:] - _mm(p_b[p], u2[p]) for p in pairs]
    s_new = [s0[p] * decay_c[p] + _mm_tn(jnp.concatenate([v2[p], bf(u2[p])], axis=0), kb_end[p]) for p in pairs]
    for p in pairs:
        y_ref[0, 0, :, lane(p)] = y2[p][:c] + y2[p][c:]
        s_ref[p] = s_new[p]


def rwkv_scan(r, v, kk, k_dir, b_dir, lw_dir, n_ctx):
    bsz, length, width = r.shape
    c = RWKV_CHUNK
    nc, nc_ctx = length // c, n_ctx // c

    def chunk_of(d, j):
        back = jnp.where(j < nc_ctx, nc_ctx - 1 - j, nc + nc_ctx - 1 - j)
        return jnp.where(d == 0, j, back)

    shared = pl.BlockSpec((1, c, width), lambda b, d, j: (b, chunk_of(d, j), 0))
    per_dir = pl.BlockSpec((1, 1, c, width), lambda b, d, j: (b, d, chunk_of(d, j), 0))
    return pl.pallas_call(
        _rwkv_chunk_kernel,
        grid=(bsz, 2, nc),
        in_specs=[shared, shared, shared, per_dir, per_dir, per_dir],
        out_specs=per_dir,
        out_shape=jax.ShapeDtypeStruct((bsz, 2, length, width), F32),
        scratch_shapes=[pltpu.VMEM((width // LANES, LANES, LANES), F32)],
        compiler_params=pltpu.CompilerParams(dimension_semantics=("arbitrary", "arbitrary", "arbitrary")),
        name="rwkv_scan",
    )(r, v, kk, k_dir, b_dir, lw_dir)


S5_CHUNK = 128


def _s5_chunk_kernel(u_ref, bre_ref, bim_ref, c_ref, are_ref, aim_ref, y_ref, carry_ref, *, reverse):
    tc = S5_CHUNK
    j = pl.program_id(1)

    @pl.when(j == 0)
    def _():
        carry_ref[...] = jnp.zeros_like(carry_ref)

    u = u_ref[0]
    xr = _mm(u, bre_ref[...])
    xi = _mm(u, bim_ref[...])
    pw_r, pw_i = are_ref[...], aim_ref[...]
    a_pow = lambda n: (pw_r[8 - n:9 - n], pw_i[8 - n:9 - n]) if reverse else (pw_r[n - 1:n], pw_i[n - 1:n])
    sub = lax.broadcasted_iota(jnp.int32, xr.shape, 0) % 8
    for sh in (1, 2, 4):
        ar, ai = a_pow(sh)
        if reverse:
            sr, si = pltpu.roll(xr, tc - sh, 0), pltpu.roll(xi, tc - sh, 0)
            keep = sub < 8 - sh
        else:
            sr, si = pltpu.roll(xr, sh, 0), pltpu.roll(xi, sh, 0)
            keep = sub >= sh
        sr, si = jnp.where(keep, sr, 0.0), jnp.where(keep, si, 0.0)
        xr, xi = xr + (ar * sr - ai * si), xi + (ar * si + ai * sr)
    a8r, a8i = a_pow(8)
    cr, ci = carry_ref[0:1, :], carry_ref[1:2, :]
    n_groups = tc // 8
    enter_r, enter_i = [None] * n_groups, [None] * n_groups
    for g in (range(n_groups - 1, -1, -1) if reverse else range(n_groups)):
        enter_r[g], enter_i[g] = cr, ci
        close = 8 * g if reverse else 8 * g + 7
        cr, ci = (xr[close:close + 1, :] + (a8r * cr - a8i * ci), xi[close:close + 1, :] + (a8r * ci + a8i * cr))
    carry_ref[0:1, :] = cr
    carry_ref[1:2, :] = ci
    er = jnp.concatenate([jnp.broadcast_to(z, (8, z.shape[1])) for z in enter_r], axis=0)
    ei = jnp.concatenate([jnp.broadcast_to(z, (8, z.shape[1])) for z in enter_i], axis=0)
    tr, ti = jnp.tile(pw_r, (n_groups, 1)), jnp.tile(pw_i, (n_groups, 1))
    xr, xi = xr + (tr * er - ti * ei), xi + (tr * ei + ti * er)
    y_ref[0] = _mm(jnp.concatenate([xr, xi], axis=1), c_ref[...])


def s5_scan(u, b_re, b_im, c_cat, a_re, a_im, n_ctx, reverse):
    bsz, length, width = u.shape
    n_state = b_re.shape[1]
    tc = S5_CHUNK
    nc, nc_ctx = length // tc, n_ctx // tc

    def chunk_of(j):
        if not reverse:
            return j
        return jnp.where(j < nc_ctx, nc_ctx - 1 - j, nc + nc_ctx - 1 - j)

    tok = pl.BlockSpec((1, tc, width), lambda b, j: (b, chunk_of(j), 0))
    full = lambda shape: pl.BlockSpec(shape, lambda b, j: (0,) * len(shape))
    return pl.pallas_call(
        functools.partial(_s5_chunk_kernel, reverse=reverse),
        grid=(bsz, nc),
        in_specs=[tok, full(b_re.shape), full(b_im.shape), full(c_cat.shape), full(a_re.shape), full(a_im.shape)],
        out_specs=tok,
        out_shape=jax.ShapeDtypeStruct((bsz, length, width), F32),
        scratch_shapes=[pltpu.VMEM((8, n_state), F32)],
        compiler_params=pltpu.CompilerParams(dimension_semantics=("arbitrary", "arbitrary"),
                                             vmem_limit_bytes=48 * 1024 * 1024),
        name="s5_scan_rev" if reverse else "s5_scan_fwd",
    )(u, b_re, b_im, c_cat, a_re, a_im)


RET_HEADS = 4
RET_HEAD_DIM = 128
RET_WIDTH = RET_HEADS * RET_HEAD_DIM
RET_CHUNK = 128
ROPE_BASE = 10000.0
ROW_TILE = 256


def _ret_chunk_kernel(q_ref, k_ref, v_ref, cos_ref, sin_ref, dm_ref, dq_ref, dk_ref, o_ref, s_ref, *, chunk_decay):
    j = pl.program_id(2)

    @pl.when(j == 0)
    def _():
        s_ref[...] = jnp.zeros_like(s_ref)

    cos2, sin2 = cos_ref[...], sin_ref[...]
    rope = lambda z: z * cos2 + pltpu.roll(z, RET_HEAD_DIM // 2, 1) * sin2
    for h in range(RET_HEADS):
        sl = slice(h * RET_HEAD_DIM, (h + 1) * RET_HEAD_DIM)
        q = rope(q_ref[0, :, sl])
        k = rope(k_ref[0, :, sl] * (RET_HEAD_DIM ** -0.5))
        v = v_ref[0, :, sl]
        s0 = s_ref[h]
        scores = _mm_nt(q, k) * dm_ref[0, h]
        o_ref[0, 0, :, sl] = _mm(scores, v) + _mm(q * dq_ref[0, h], s0)
        s_ref[h] = chunk_decay[h] * s0 + _mm_tn(k * dk_ref[0, h], v)


def _ret_decay_tables():
    c = RET_CHUNK
    lg = jnp.log1p(-jnp.exp2(-5.0 - jnp.arange(RET_HEADS, dtype=F32)))[:, None, None]
    n = jnp.arange(c, dtype=F32)[:, None]
    m = jnp.arange(c, dtype=F32)[None, :]
    fwd = jnp.where(n >= m, jnp.exp(jnp.where(n >= m, n - m, 0.0) * lg), 0.0)
    bwd = jnp.where(m > n, jnp.exp(jnp.where(m > n, m - n, 0.0) * lg), 0.0)
    ones = jnp.ones((1, c), F32)
    dq = jnp.stack([jnp.exp((n + 1.0) * lg) * ones, jnp.exp((c - n) * lg) * ones])
    dk = jnp.stack([jnp.exp((c - 1.0 - n) * lg) * ones, jnp.exp(n * lg) * ones])
    return jnp.stack([fwd, bwd]), dq, dk


def retention_scan(p_ret, cos2, sin2, n_ctx):
    bsz, length, _ = p_ret.shape
    c, w = RET_CHUNK, RET_WIDTH
    nc, nc_ctx = length // c, n_ctx // c
    dm, dq, dk = _ret_decay_tables()
    chunk_decay = tuple(math.exp(c * math.log1p(-2.0 ** (-5 - h))) for h in range(RET_HEADS))

    def chunk_of(d, j):
        back = jnp.where(j < nc_ctx, nc_ctx - 1 - j, nc + nc_ctx - 1 - j)
        return jnp.where(d == 0, j, back)

    col = lambda which: pl.BlockSpec((1, c, w), lambda b, d, j: (b, chunk_of(d, j), which(d)))
    rope_spec = pl.BlockSpec((c, RET_HEAD_DIM), lambda b, d, j: (chunk_of(d, j), 0))
    table = pl.BlockSpec((1, RET_HEADS, c, c), lambda b, d, j: (d, 0, 0, 0))
    return pl.pallas_call(
        functools.partial(_ret_chunk_kernel, chunk_decay=chunk_decay),
        grid=(bsz, 2, nc),
        in_specs=[col(lambda d: 0), col(lambda d: 1 + d), col(lambda d: 3), rope_spec, rope_spec, table, table, table],
        out_specs=pl.BlockSpec((1, 1, c, w), lambda b, d, j: (b, d, chunk_of(d, j), 0)),
        out_shape=jax.ShapeDtypeStruct((bsz, 2, length, w), F32),
        scratch_shapes=[pltpu.VMEM((RET_HEADS, RET_HEAD_DIM, RET_HEAD_DIM), F32)],
        compiler_params=pltpu.CompilerParams(dimension_semantics=("arbitrary", "arbitrary", "arbitrary")),
        name="retention_scan",
    )(p_ret, p_ret, p_ret, cos2, sin2, dm, dq, dk)


def _ret_out_kernel(o_ref, g_ref, gn_g_ref, gn_b_ref, y_ref):
    o = o_ref[0, 0] + o_ref[0, 1]
    g = g_ref[0]
    for h in range(RET_HEADS):
        sl = slice(h * RET_HEAD_DIM, (h + 1) * RET_HEAD_DIM)
        z = o[:, sl]
        zc = z - jnp.mean(z, axis=1, keepdims=True)
        zn = zc * lax.rsqrt(jnp.mean(zc * zc, axis=1, keepdims=True) + EPS)
        gate = g[:, sl]
        y_ref[0, :, sl] = (zn * gn_g_ref[:, sl] + gn_b_ref[:, sl]) * (gate * jax.nn.sigmoid(gate))


def retention_out(o, p_ret, gn_g, gn_b):
    bsz, _, length, w = o.shape
    tm = ROW_TILE
    vec = pl.BlockSpec((1, w), lambda b, i: (0, 0))
    return pl.pallas_call(
        _ret_out_kernel,
        grid=(bsz, length // tm),
        in_specs=[pl.BlockSpec((1, 2, tm, w), lambda b, i: (b, 0, i, 0)),
                  pl.BlockSpec((1, tm, w), lambda b, i: (b, i, 4)), vec, vec],
        out_specs=pl.BlockSpec((1, tm, w), lambda b, i: (b, i, 0)),
        out_shape=jax.ShapeDtypeStruct((bsz, length, w), F32),
        compiler_params=pltpu.CompilerParams(dimension_semantics=("arbitrary", "arbitrary")),
        name="retention_out",
    )(o, p_ret, gn_g.reshape(1, w), gn_b.reshape(1, w))


def rope_tables(n_tokens, n_ctx):
    rows = n_tokens // GRID_W
    row = jnp.repeat(jnp.arange(rows, dtype=F32), GRID_W)
    col = jnp.tile(jnp.arange(GRID_W, dtype=F32), rows)
    n_freq = RET_HEAD_DIM // 4
    inv = ROPE_BASE ** (-jnp.arange(n_freq, dtype=F32) / n_freq)
    ang = jnp.concatenate([row[:, None] * inv, col[:, None] * inv], axis=-1)
    cos, sin = jnp.cos(ang), jnp.sin(ang)
    cos2 = jnp.concatenate([jnp.ones((n_ctx, RET_HEAD_DIM), F32), jnp.concatenate([cos, cos], axis=-1)], axis=0)
    sin2 = jnp.concatenate([jnp.zeros((n_ctx, RET_HEAD_DIM), F32), jnp.concatenate([-sin, sin], axis=-1)], axis=0)
    return cos2, sin2


VMEM_LIMIT = 56 * 1024 * 1024


def _const_spec(a):
    return pl.BlockSpec(a.shape, lambda b, i: (0,) * a.ndim, pipeline_mode=pl.Buffered(1))


def _mod_spec(d):
    return pl.BlockSpec((1, 1, 6, d), lambda b, i: (b, jnp.minimum(i, 1), 0, 0))


def _tok_spec(width, col=0):
    return pl.BlockSpec((1, ROW_TILE, width), lambda b, i: (b, i, col))


def _norm_mod(x, g, mod, shift_row, scale_row):
    y = x * lax.rsqrt(jnp.mean(x * x, axis=1, keepdims=True) + EPS) * g
    return y * (1.0 + mod[scale_row:scale_row + 1]) + mod[shift_row:shift_row + 1]


def _in_proj_kernel(x_ref, g_ref, mod_ref, w_ref, o_ref):
    n = _norm_mod(x_ref[0], g_ref[...], mod_ref[0, 0], 0, 1)
    o_ref[0] = _mm(n, w_ref[...])


def in_proj(x, norm_g, mods, w):
    bsz, length, d = x.shape
    n_out = w.shape[1]
    return pl.pallas_call(
        _in_proj_kernel,
        grid=(bsz, length // ROW_TILE),
        in_specs=[_tok_spec(d), _const_spec(norm_g), _mod_spec(d), _const_spec(w)],
        out_specs=_tok_spec(n_out),
        out_shape=jax.ShapeDtypeStruct((bsz, length, n_out), F32),
        compiler_params=pltpu.CompilerParams(dimension_semantics=("arbitrary", "arbitrary"),
                                             vmem_limit_bytes=VMEM_LIMIT),
        name="in_proj",
    )(x, norm_g, mods, w)


ADA_COLS = 1024


def _ada_kernel(c_ref, w_ref, b_ref, o_ref):
    cv = c_ref[...]
    o_ref[...] = _mm(cv * jax.nn.sigmoid(cv), w_ref[...]) + b_ref[...]


def ada_modulation(cond, w, b):
    rows, d = cond.shape
    n_out = w.shape[1]
    return pl.pallas_call(
        _ada_kernel,
        grid=(n_out // ADA_COLS,),
        in_specs=[pl.BlockSpec((rows, d), lambda j: (0, 0)), pl.BlockSpec((d, ADA_COLS), lambda j: (0, j)),
                  pl.BlockSpec((1, ADA_COLS), lambda j: (0, j))],
        out_specs=pl.BlockSpec((rows, ADA_COLS), lambda j: (0, j)),
        out_shape=jax.ShapeDtypeStruct((rows, n_out), F32),
        compiler_params=pltpu.CompilerParams(dimension_semantics=("arbitrary",)),
        name="ada_modulation",
    )(cond, w, b)


def _norm_mod_kernel(x_ref, g_ref, mod_ref, o_ref):
    o_ref[0] = _norm_mod(x_ref[0], g_ref[...], mod_ref[0, 0], 3, 4)


def norm_modulate2(x, norm_g, mods):
    bsz, length, d = x.shape
    return pl.pallas_call(
        _norm_mod_kernel,
        grid=(bsz, length // ROW_TILE),
        in_specs=[_tok_spec(d), _const_spec(norm_g), _mod_spec(d)],
        out_specs=_tok_spec(d),
        out_shape=jax.ShapeDtypeStruct((bsz, length, d), F32),
        compiler_params=pltpu.CompilerParams(dimension_semantics=("arbitrary", "arbitrary")),
        name="norm_modulate2",
    )(x, norm_g, mods)


def _residual_kernel(x_ref, f_ref, mod_ref, g_ref, o_ref, *, final_norm):
    y = x_ref[0] + mod_ref[0, 0][5:6] * f_ref[0]
    if final_norm:
        y = y * lax.rsqrt(jnp.mean(y * y, axis=1, keepdims=True) + EPS) * g_ref[...]
    o_ref[0] = y


def residual2(x, f, mods, final_g, final_norm):
    bsz, length, d = x.shape
    return pl.pallas_call(
        functools.partial(_residual_kernel, final_norm=final_norm),
        grid=(bsz, length // ROW_TILE),
        in_specs=[_tok_spec(d), _tok_spec(d), _mod_spec(d), _const_spec(final_g)],
        out_specs=_tok_spec(d),
        out_shape=jax.ShapeDtypeStruct((bsz, length, d), F32),
        compiler_params=pltpu.CompilerParams(dimension_semantics=("arbitrary", "arbitrary")),
        name="residual2",
    )(x, f, mods, final_g)


def _merge_kernel(x_ref, g_ref, mod_ref, ya_ref, yb_ref, yc_ref, yd_ref, wg_ref, bg_ref, wbr_ref, wout_ref, o_ref):
    x = x_ref[0]
    mod = mod_ref[0, 0]
    n = _norm_mod(x, g_ref[...], mod, 0, 1).astype(BF16)
    m = jnp.zeros(x.shape, F32)
    for i, y_ref in enumerate((ya_ref, yb_ref, yc_ref, yd_ref)):
        gate = jax.nn.sigmoid(_mm(n, wg_ref[i]) + bg_ref[i:i + 1])
        m = m + gate * _mm(y_ref[0], wbr_ref[i])
    o_ref[0] = x + mod[2:3] * _mm(m, wout_ref[...])


def merge_residual(x, norm_g, mods, ys, w_merge, b_merge, w_branch, w_out):
    bsz, length, d = x.shape
    bw = ys[0].shape[-1]
    return pl.pallas_call(
        _merge_kernel,
        grid=(bsz, length // ROW_TILE),
        in_specs=[_tok_spec(d), _const_spec(norm_g), _mod_spec(d)] + [_tok_spec(bw)] * 4
        + [_const_spec(w_merge), _const_spec(b_merge), _const_spec(w_branch), _const_spec(w_out)],
        out_specs=_tok_spec(d),
        out_shape=jax.ShapeDtypeStruct((bsz, length, d), F32),
        compiler_params=pltpu.CompilerParams(dimension_semantics=("arbitrary", "arbitrary"),
                                             vmem_limit_bytes=VMEM_LIMIT),
        name="merge_residual",
    )(x, norm_g, mods, *ys, w_merge, b_merge, w_branch, w_out)


def _halo_specs(width, n_tiles):
    prev = pl.BlockSpec((1, ROW_TILE, width), lambda b, i: (b, jnp.maximum(i - 1, 0), 0))
    nxt = pl.BlockSpec((1, ROW_TILE, width), lambda b, i: (b, jnp.minimum(i + 1, n_tiles - 1), 0))
    return [prev, _tok_spec(width), nxt]


def _segment_edges(n_tiles):
    i = pl.program_id(1)
    return i >= 2, jnp.logical_and(i >= 1, i < n_tiles - 1)


CONV_CHANNELS = 512
CONV_TAPS = 31
HALO = 16


def _conformer_kernel(prev_ref, cur_ref, next_ref, dw_ref, db_ref, lng_ref, lnb_ref, o_ref, ext_ref, *, n_tiles):
    ch = CONV_CHANNELS
    glu = lambda p: p[:, :ch] * jax.nn.sigmoid(p[:, ch:])
    has_prev, has_next = _segment_edges(n_tiles)
    ext_ref[0:HALO, :] = jnp.where(has_prev, glu(prev_ref[0, ROW_TILE - HALO:, :]), 0.0)
    ext_ref[HALO:HALO + ROW_TILE, :] = glu(cur_ref[0])
    ext_ref[HALO + ROW_TILE:, :] = jnp.where(has_next, glu(next_ref[0, :HALO, :]), 0.0)
    pad = (CONV_TAPS - 1) // 2
    acc = jnp.zeros((ROW_TILE, ch), F32) + db_ref[...]
    for j in range(CONV_TAPS):
        acc = acc + ext_ref[pl.ds(HALO - pad + j, ROW_TILE), :] * dw_ref[j:j + 1, :]
    zc = acc - jnp.mean(acc, axis=1, keepdims=True)
    z = zc * lax.rsqrt(jnp.mean(zc * zc, axis=1, keepdims=True) + EPS) * lng_ref[...] + lnb_ref[...]
    o_ref[0] = z * jax.nn.sigmoid(z)


def conformer_conv(p, dw, db, ln_g, ln_b):
    bsz, length, width = p.shape
    n_tiles = length // ROW_TILE
    ch = CONV_CHANNELS
    row = lambda v: v.reshape(1, ch)
    return pl.pallas_call(
        functools.partial(_conformer_kernel, n_tiles=n_tiles),
        grid=(bsz, n_tiles),
        in_specs=_halo_specs(width, n_tiles) + [_const_spec(dw)] + [_const_spec(row(db))] * 3,
        out_specs=_tok_spec(ch),
        out_shape=jax.ShapeDtypeStruct((bsz, length, ch), F32),
        scratch_shapes=[pltpu.VMEM((ROW_TILE + 2 * HALO, ch), F32)],
        compiler_params=pltpu.CompilerParams(dimension_semantics=("arbitrary", "arbitrary"),
                                             vmem_limit_bytes=VMEM_LIMIT),
        name="conformer_conv",
    )(p, p, p, dw, row(db), row(ln_g), row(ln_b))


def _s5_out_kernel(u_ref, yf_ref, yb_ref, d_ref, w_ref, b_ref, o_ref):
    z = _gelu_tanh(d_ref[...] * u_ref[0] + yf_ref[0] + yb_ref[0])
    o_ref[0] = z * jax.nn.sigmoid(_mm(z, w_ref[...]) + b_ref[...])


def s5_out(u, y_fwd, y_bwd, d_skip, glu_w, glu_b):
    bsz, length, w = u.shape
    return pl.pallas_call(
        _s5_out_kernel,
        grid=(bsz, length // ROW_TILE),
        in_specs=[_tok_spec(w)] * 3 + [_const_spec(d_skip), _const_spec(glu_w), _const_spec(glu_b)],
        out_specs=_tok_spec(w),
        out_shape=jax.ShapeDtypeStruct((bsz, length, w), F32),
        compiler_params=pltpu.CompilerParams(dimension_semantics=("arbitrary", "arbitrary")),
        name="s5_out",
    )(u, y_fwd, y_bwd, d_skip, glu_w, glu_b)


def _head_sums(z, ones_bd):
    hi = z.astype(BF16)
    lo = (z - hi.astype(F32)).astype(BF16)
    return _mm(hi, ones_bd) + _mm(lo, ones_bd)


def _rwkv_prep_kernel(prev_ref, cur_ref, next_ref, shift_ref, w2_ref, a2_ref, g2_ref, vec_ref, bd_ref,
                      r_ref, v_ref, kk_ref, gb_ref, k_ref, b_ref, lw_ref, ext_ref, *, n_tiles):
    w = RWKV_WIDTH
    has_prev, has_next = _segment_edges(n_tiles)
    ext_ref[0:8, :] = jnp.where(has_prev, prev_ref[0, ROW_TILE - 8:, :], 0.0)
    ext_ref[8:8 + ROW_TILE, :] = cur_ref[0]
    ext_ref[8 + ROW_TILE:, :] = jnp.where(has_next, next_ref[0, :8, :], 0.0)
    p = (ext_ref[pl.ds(7, ROW_TILE), :] * shift_ref[0:1, :] + ext_ref[pl.ds(8, ROW_TILE), :] * shift_ref[1:2, :]
         + ext_ref[pl.ds(9, ROW_TILE), :] * shift_ref[2:3, :])
    r, k, v = p[:, :w], p[:, w:2 * w], p[:, 2 * w:3 * w]
    lowrank_w = jnp.tanh(p[:, 3 * w:3 * w + LANES])
    lowrank_a = p[:, 3 * w + LANES:3 * w + 2 * LANES]
    gl = p[:, 3 * w + 2 * LANES:]
    ones_bd = bd_ref[...]
    kk = k * vec_ref[0:1, :]
    kk = kk * lax.rsqrt(_head_sums(kk * kk, ones_bd) + EPS)
    r_ref[0], v_ref[0], kk_ref[0] = r, v, kk
    gb_ref[0, 0] = _mm(jax.nn.sigmoid(gl), g2_ref[...])
    k_sum = jnp.zeros_like(k)
    for d in range(2):
        w_log = -jax.nn.softplus(-(vec_ref[4 + d:5 + d, :] + _mm(lowrank_w, w2_ref[d]))) - 0.5
        a = jax.nn.sigmoid(vec_ref[6 + d:7 + d, :] + _mm(lowrank_a, a2_ref[d]))
        k_d = k * (1.0 + (a - 1.0) * vec_ref[1:2, :])
        k_ref[0, d], b_ref[0, d], lw_ref[0, d] = k_d, a * kk, -jnp.exp(w_log)
        k_sum = k_sum + k_d
    gb_ref[0, 1] = _head_sums(r * k_sum * vec_ref[2:3, :], ones_bd) * v


def rwkv_prep(p, shift_w, w2, a2, g2, vecs, ones_bd):
    bsz, length, width = p.shape
    n_tiles = length // ROW_TILE
    w = RWKV_WIDTH
    shared = jax.ShapeDtypeStruct((bsz, length, w), F32)
    per_dir = jax.ShapeDtypeStruct((bsz, 2, length, w), F32)
    dir_spec = pl.BlockSpec((1, 2, ROW_TILE, w), lambda b, i: (b, 0, i, 0))
    return pl.pallas_call(
        functools.partial(_rwkv_prep_kernel, n_tiles=n_tiles),
        grid=(bsz, n_tiles),
        in_specs=_halo_specs(width, n_tiles) + [_const_spec(a) for a in (shift_w, w2, a2, g2, vecs, ones_bd)],
        out_specs=[_tok_spec(w)] * 3 + [dir_spec] * 4,
        out_shape=[shared] * 3 + [per_dir] * 4,
        scratch_shapes=[pltpu.VMEM((ROW_TILE + 16, width), F32)],
        compiler_params=pltpu.CompilerParams(dimension_semantics=("arbitrary", "arbitrary"),
                                             vmem_limit_bytes=VMEM_LIMIT),
        name="rwkv_prep",
    )(p, p, p, shift_w, w2, a2, g2, vecs, ones_bd)


def _rwkv_out_kernel(y_ref, gb_ref, gn_ref, bd_ref, o_ref):
    y = y_ref[0, 0] + y_ref[0, 1]
    ones_bd = bd_ref[...]
    inv = 1.0 / RWKV_HEAD_DIM
    yc = y - _head_sums(y, ones_bd) * inv
    yn = yc * lax.rsqrt(_head_sums(yc * yc, ones_bd) * inv + EPS)
    o_ref[0] = (yn * gn_ref[0:1, :] + gn_ref[1:2, :] + gb_ref[0, 1]) * gb_ref[0, 0]


def rwkv_out(y, gate_bonus, gn, ones_bd):
    bsz, _, length, w = y.shape
    dir_spec = pl.BlockSpec((1, 2, ROW_TILE, w), lambda b, i: (b, 0, i, 0))
    return pl.pallas_call(
        _rwkv_out_kernel,
        grid=(bsz, length // ROW_TILE),
        in_specs=[dir_spec, dir_spec, _const_spec(gn), _const_spec(ones_bd)],
        out_specs=_tok_spec(w),
        out_shape=jax.ShapeDtypeStruct((bsz, length, w), F32),
        compiler_params=pltpu.CompilerParams(dimension_semantics=("arbitrary", "arbitrary")),
        name="rwkv_out",
    )(y, gate_bonus, gn, ones_bd)


SC_CORES = 2
SC_SUBCORES = 16
SC_WORKERS = SC_CORES * SC_SUBCORES
GATHER_ROWS = 64
PEER_SLOTS = 128
PEER_TOKENS_PER_STEP = 16


PEER_HEADS = 8
PEER_KEYS = 128
PEER_TOPK = 16
PEER_HALF = 128
PEER_SELECT_TOKENS = 128


def _top_rows(s, payload=None):
    n_rows = s.shape[0]
    iota = lax.broadcasted_iota(jnp.int32, s.shape, 0).astype(F32)
    vals, picks = [], []
    for _ in range(PEER_TOPK):
        m = jnp.max(s, axis=0, keepdims=True)
        pos = jnp.min(jnp.where(s == m, iota, float(n_rows)), axis=0, keepdims=True)
        hit = iota == pos
        vals.append(m)
        picks.append(pos if payload is None else jnp.max(jnp.where(hit, payload, -1.0), axis=0, keepdims=True))
        s = jnp.where(hit, -jnp.inf, s)
    return jnp.concatenate(vals, axis=0), jnp.concatenate(picks, axis=0)


def _peer_select_kernel(h_ref, wq_ref, keys_ref, ids_ref, gate_ref):
    q = _mm(h_ref[...], wq_ref[...]).astype(BF16)
    for h in range(PEER_HEADS):
        halves = []
        for p in range(2):
            lst = 2 * h + p
            s = _mm_nt(keys_ref[lst], q[:, lst * PEER_HALF:(lst + 1) * PEER_HALF])
            halves.append(_top_rows(s))
        (v1, p1), (v2, p2) = halves
        sub8 = lax.broadcasted_iota(jnp.int32, (8, v1.shape[1]), 0)
        cand_rows, id_rows = [v1[0:1] + v2], [p1[0:1] * PEER_KEYS + p2]
        for i in range(1, 8):
            bound = PEER_TOPK // (i + 1)
            slab = v1[i:i + 1] + v2[0:8]
            cand_rows.append(slab if bound >= 8 else jnp.where(sub8 < bound, slab, -jnp.inf))
            id_rows.append(p1[i:i + 1] * PEER_KEYS + p2[0:8])
        cand_rows.append(v1[8:] + v2[0:1])
        id_rows.append(p1[8:] * PEER_KEYS + p2[0:1])
        best, ids = _top_rows(jnp.concatenate(cand_rows, axis=0), jnp.concatenate(id_rows, axis=0))
        e = jnp.exp(best - best[0:1])
        ids_ref[h * PEER_TOPK:(h + 1) * PEER_TOPK, :] = ids.astype(jnp.int32)
        gate_ref[h * PEER_TOPK:(h + 1) * PEER_TOPK, :] = e / jnp.sum(e, axis=0, keepdims=True)


def peer_select(h, wq, keys):
    n, d = h.shape
    tn = PEER_SELECT_TOKENS
    slots = PEER_HEADS * PEER_TOPK
    full = lambda a: pl.BlockSpec(a.shape, lambda i: (0,) * a.ndim)
    out = pl.BlockSpec((slots, tn), lambda i: (0, i))
    return pl.pallas_call(
        _peer_select_kernel,
        grid=(n // tn,),
        in_specs=[pl.BlockSpec((tn, d), lambda i: (i, 0)), full(wq), full(keys)],
        out_specs=[out, out],
        out_shape=[jax.ShapeDtypeStruct((slots, n), jnp.int32), jax.ShapeDtypeStruct((slots, n), F32)],
        compiler_params=pltpu.CompilerParams(dimension_semantics=("arbitrary",),
                                             vmem_limit_bytes=48 * 1024 * 1024),
        name="peer_select",
    )(h, wq, keys)


def pack_bf16_pairs(table):
    half = table.shape[1] // 2
    bits = lax.bitcast_convert_type(table.astype(BF16), jnp.uint16).astype(jnp.uint32)
    return bits[:, :half] | (bits[:, half:] << 16)


def sc_gather_rows(table, idx):
    n_rows, width = idx.shape[0], table.shape[1]
    per_worker = n_rows // SC_WORKERS
    n_pairs = per_worker // (2 * GATHER_ROWS)
    assert per_worker * SC_WORKERS == n_rows and n_pairs * 2 * GATHER_ROWS == per_worker
    mesh = plsc.VectorSubcoreMesh(core_axis_name="c", subcore_axis_name="s",
                                  num_cores=SC_CORES, num_subcores=SC_SUBCORES)

    @functools.partial(
        pl.kernel, mesh=mesh,
        out_type=jax.ShapeDtypeStruct((n_rows, width), table.dtype),
        scratch_types=[pltpu.VMEM((2, GATHER_ROWS), jnp.int32),
                       pltpu.VMEM((2, GATHER_ROWS, width), table.dtype),
                       pltpu.SemaphoreType.DMA((2,)),
                       pltpu.SemaphoreType.DMA((2,))],
        name="peer_sc_gather",
    )
    def gather(table_hbm, idx_hbm, out_hbm, idx_v, rows_v, gather_sem, write_sem):
        worker = lax.axis_index("s") * SC_CORES + lax.axis_index("c")
        base = worker * per_worker

        def rows_of(chunk):
            return pl.ds(pl.multiple_of(base + chunk * GATHER_ROWS, GATHER_ROWS), GATHER_ROWS)

        def gather_copy(slot):
            return pltpu.make_async_copy(table_hbm.at[idx_v.at[slot]], rows_v.at[slot], gather_sem.at[slot])

        def write_copy(chunk, slot):
            return pltpu.make_async_copy(rows_v.at[slot], out_hbm.at[rows_of(chunk)], write_sem.at[slot])

        def start_gather(chunk, slot):
            pltpu.sync_copy(idx_hbm.at[rows_of(chunk)], idx_v.at[slot])
            gather_copy(slot).start()

        start_gather(0, 0)

        @pl.loop(0, n_pairs)
        def _(g):
            even, odd = 2 * g, 2 * g + 1

            @pl.when(g > 0)
            def _():
                write_copy(odd - 2, 1).wait()
            start_gather(odd, 1)
            gather_copy(0).wait()
            write_copy(even, 0).start()
            gather_copy(1).wait()
            write_copy(odd, 1).start()
            write_copy(even, 0).wait()

            @pl.when(g + 1 < n_pairs)
            def _():
                start_gather(even + 2, 0)

        write_copy(2 * n_pairs - 1, 1).wait()

    return gather(table, idx)


def _unpack_pairs(words):
    lo = pltpu.bitcast(words << 16, F32)
    hi = pltpu.bitcast(words & jnp.uint32(0xFFFF0000), F32)
    return lo, hi


def _gelu_tanh(x):
    return 0.5 * x * (1.0 + jnp.tanh(0.7978845608028654 * (x + 0.044715 * (x * x * x))))


def _peer_expert_kernel(z_ref, gate_ref, ug_ref, vg_ref, o_ref):
    half = ug_ref.shape[2]
    gate_t = gate_ref[...].T
    for n in range(PEER_TOKENS_PER_STEP):
        z_lo, z_hi = z_ref[n:n + 1, :half], z_ref[n:n + 1, half:]
        u_lo, u_hi = _unpack_pairs(ug_ref[n])
        prod = u_lo * z_lo + u_hi * z_hi
        act = jnp.sum(prod, axis=1, keepdims=True)
        w = _gelu_tanh(act) * gate_t[:, n:n + 1]
        v_lo, v_hi = _unpack_pairs(vg_ref[n])
        o_ref[n:n + 1, :half] = jnp.sum(w * v_lo, axis=0, keepdims=True)
        o_ref[n:n + 1, half:] = jnp.sum(w * v_hi, axis=0, keepdims=True)


def peer_experts(z, gate, ug, vg):
    n, d = z.shape
    tn = PEER_TOKENS_PER_STEP
    slots, half = ug.shape[1], ug.shape[2]
    return pl.pallas_call(
        _peer_expert_kernel,
        grid=(n // tn,),
        in_specs=[pl.BlockSpec((tn, d), lambda i: (i, 0)),
                  pl.BlockSpec((tn, slots), lambda i: (i, 0)),
                  pl.BlockSpec((tn, slots, half), lambda i: (i, 0, 0)),
                  pl.BlockSpec((tn, slots, half), lambda i: (i, 0, 0))],
        out_specs=pl.BlockSpec((tn, d), lambda i: (i, 0)),
        out_shape=jax.ShapeDtypeStruct((n, d), F32),
        compiler_params=pltpu.CompilerParams(dimension_semantics=("arbitrary",),
                                             vmem_limit_bytes=48 * 1024 * 1024),
        name="peer_experts",
    )(z, gate, ug, vg)


S5_WIDTH = 512
S5_GROUP = 16
S5_GROUPS = S5_WIDTH // S5_GROUP
S5_STATE = 64
S5_MAX_RE = -1e-4
A_END = 3 * RWKV_WIDTH + 2 * RWKV_DECAY_RANK + 2 * RWKV_ICLR_RANK + RWKV_GATE_RANK
B_END = A_END + 2 * CONV_CHANNELS
C_END = B_END + S5_WIDTH
IN_WIDTH = C_END + 5 * RET_WIDTH


def _s5_discretise(lam_re, lam_im, log_dt, b_re, b_im):
    lam_re = jnp.minimum(lam_re.astype(F32), S5_MAX_RE)
    lam_im = lam_im.astype(F32)
    dt = jnp.exp(log_dt.astype(F32))[:, None]
    mag = jnp.exp(lam_re * dt)
    ang = lam_im * dt
    ab_re, ab_im = mag * jnp.cos(ang), mag * jnp.sin(ang)
    den = lam_re * lam_re + lam_im * lam_im
    nr, ni = ab_re - 1.0, ab_im
    f_re = (nr * lam_re + ni * lam_im) / den
    f_im = (ni * lam_re - nr * lam_im) / den
    b_re, b_im = b_re.astype(F32), b_im.astype(F32)
    bb_re = f_re[..., None] * b_re - f_im[..., None] * b_im
    bb_im = f_re[..., None] * b_im + f_im[..., None] * b_re
    return ab_re, ab_im, bb_re, bb_im


PEER_EXPERTS = PEER_KEYS * PEER_KEYS
SC_LANES = 16
SC_TOKENS = 4
DENSE_TOKENS = 768
DENSE_EXPERTS = 2048


def _sc_mesh():
    return plsc.VectorSubcoreMesh(core_axis_name="c", subcore_axis_name="s",
                                  num_cores=SC_CORES, num_subcores=SC_SUBCORES)


def sc_pick_experts(act, ids):
    n = ids.shape[0] // PEER_SLOTS
    e = PEER_EXPERTS
    n_steps = n // (SC_WORKERS * SC_TOKENS)
    assert n_steps * SC_TOKENS * SC_WORKERS == n
    blk = SC_TOKENS * PEER_SLOTS

    @functools.partial(
        pl.kernel, mesh=_sc_mesh(), out_type=jax.ShapeDtypeStruct((n * PEER_SLOTS,), F32),
        scratch_types=[pltpu.VMEM((SC_TOKENS * e,), F32), pltpu.VMEM((blk,), jnp.int32), pltpu.VMEM((blk,), F32)],
        compiler_params=pltpu.CompilerParams(needs_layout_passes=False), name="peer_sc_pick")
    def pick(act_hbm, ids_hbm, out_hbm, rows_v, idx_v, out_v):
        worker = lax.axis_index("s") * SC_CORES + lax.axis_index("c")

        @pl.loop(0, n_steps)
        def _(i):
            tok = (worker * n_steps + i) * SC_TOKENS
            slot0 = pl.multiple_of(tok * PEER_SLOTS, blk)
            pltpu.sync_copy(ids_hbm.at[pl.ds(slot0, blk)], idx_v)
            pltpu.sync_copy(act_hbm.at[pl.ds(pl.multiple_of(tok * e, SC_TOKENS * e), SC_TOKENS * e)], rows_v)
            for t in range(SC_TOKENS):
                for k in range(PEER_SLOTS // SC_LANES):
                    sl = pl.ds(t * PEER_SLOTS + k * SC_LANES, SC_LANES)
                    out_v[sl] = plsc.load_gather(rows_v, [idx_v[sl] + t * e])
            pltpu.sync_copy(out_v, out_hbm.at[pl.ds(slot0, blk)])

    return pick(act, ids)


def sc_spread_weights(ids, w):
    n = ids.shape[0] // PEER_SLOTS
    e = PEER_EXPERTS
    n_steps = n // (SC_WORKERS * SC_TOKENS)
    assert n_steps * SC_TOKENS * SC_WORKERS == n
    blk = SC_TOKENS * PEER_SLOTS

    @functools.partial(
        pl.kernel, mesh=_sc_mesh(), out_type=jax.ShapeDtypeStruct((n * e,), F32),
        scratch_types=[pltpu.VMEM((SC_TOKENS * e,), F32), pltpu.VMEM((blk,), jnp.int32), pltpu.VMEM((blk,), F32)],
        compiler_params=pltpu.CompilerParams(needs_layout_passes=False), name="peer_sc_spread")
    def spread(ids_hbm, w_hbm, out_hbm, dense_v, idx_v, w_v):
        worker = lax.axis_index("s") * SC_CORES + lax.axis_index("c")
        zeros = jnp.zeros((SC_LANES,), F32)

        @pl.loop(0, SC_TOKENS * e // SC_LANES)
        def _(i):
            dense_v[pl.ds(pl.multiple_of(i * SC_LANES, SC_LANES), SC_LANES)] = zeros

        @pl.loop(0, n_steps)
        def _(i):
            tok = (worker * n_steps + i) * SC_TOKENS
            slot0 = pl.multiple_of(tok * PEER_SLOTS, blk)
            pltpu.sync_copy(ids_hbm.at[pl.ds(slot0, blk)], idx_v)
            pltpu.sync_copy(w_hbm.at[pl.ds(slot0, blk)], w_v)
            for t in range(SC_TOKENS):
                for k in range(PEER_SLOTS // SC_LANES):
                    sl = pl.ds(t * PEER_SLOTS + k * SC_LANES, SC_LANES)
                    plsc.addupdate_scatter(dense_v, [idx_v[sl] + t * e], w_v[sl])
            pltpu.sync_copy(dense_v, out_hbm.at[pl.ds(pl.multiple_of(tok * e, SC_TOKENS * e), SC_TOKENS * e)])
            for t in range(SC_TOKENS):
                for k in range(PEER_SLOTS // SC_LANES):
                    sl = pl.ds(t * PEER_SLOTS + k * SC_LANES, SC_LANES)
                    plsc.store_scatter(dense_v, [idx_v[sl] + t * e], zeros)

    return spread(ids, w)


def _dense_act_kernel(h_ref, u_ref, o_ref):
    o_ref[...] = _mm_nt(h_ref[...], u_ref[...])


def peer_dense_act(h, u):
    n, d = h.shape
    e = u.shape[0]
    tn, te = DENSE_TOKENS, DENSE_EXPERTS
    return pl.pallas_call(
        _dense_act_kernel,
        grid=(e // te, n // tn),
        in_specs=[pl.BlockSpec((tn, d), lambda j, i: (i, 0)), pl.BlockSpec((te, d), lambda j, i: (j, 0))],
        out_specs=pl.BlockSpec((tn, te), lambda j, i: (i, j)),
        out_shape=jax.ShapeDtypeStruct((n, e), F32),
        compiler_params=pltpu.CompilerParams(dimension_semantics=("arbitrary", "arbitrary"),
                                             vmem_limit_bytes=VMEM_LIMIT),
        name="peer_dense_act",
    )(h, u)


def _dense_out_kernel(w_ref, v_ref, o_ref):
    @pl.when(pl.program_id(1) == 0)
    def _():
        o_ref[...] = jnp.zeros_like(o_ref)
    o_ref[...] += _mm(w_ref[...], v_ref[...])


def peer_dense_out(wd, v):
    n, e = wd.shape
    d = v.shape[1]
    tn, te = DENSE_TOKENS, DENSE_EXPERTS
    return pl.pallas_call(
        _dense_out_kernel,
        grid=(n // tn, e // te),
        in_specs=[pl.BlockSpec((tn, te), lambda i, j: (i, j)), pl.BlockSpec((te, d), lambda i, j: (j, 0))],
        out_specs=pl.BlockSpec((tn, d), lambda i, j: (i, 0)),
        out_shape=jax.ShapeDtypeStruct((n, d), F32),
        compiler_params=pltpu.CompilerParams(dimension_semantics=("arbitrary", "arbitrary"),
                                             vmem_limit_bytes=VMEM_LIMIT),
        name="peer_dense_out",
    )(wd, v)


def _peer_weight_kernel(a_ref, g_ref, o_ref):
    o_ref[...] = g_ref[...] * _gelu_tanh(a_ref[...])


def peer_weights(act_sel, gate):
    n, s = act_sel.shape
    spec = pl.BlockSpec((DENSE_TOKENS, s), lambda i: (i, 0))
    return pl.pallas_call(
        _peer_weight_kernel, grid=(n // DENSE_TOKENS,), in_specs=[spec, spec], out_specs=spec,
        out_shape=jax.ShapeDtypeStruct((n, s), F32),
        compiler_params=pltpu.CompilerParams(dimension_semantics=("arbitrary",)),
        name="peer_weights",
    )(act_sel, gate)


def _peer_ffn_dense(h, w_q, sub_keys, u_bf16, v_bf16):
    n, d = h.shape
    keys = sub_keys.reshape(2 * PEER_HEADS, PEER_KEYS, PEER_HALF).astype(BF16)
    ids_t, gate_t = peer_select(h, w_q.astype(BF16), keys)
    ids = ids_t.T.reshape(-1)
    act = peer_dense_act(h, u_bf16)
    act_sel = sc_pick_experts(act.reshape(-1), ids).reshape(n, PEER_SLOTS)
    w = peer_weights(act_sel, gate_t.T)
    wd = sc_spread_weights(ids, w.reshape(-1)).reshape(n, PEER_EXPERTS)
    return peer_dense_out(wd, v_bf16)


def _peer_ffn(h, w_q, sub_keys, u_packed, v_packed):
    n, d = h.shape
    keys = sub_keys.reshape(2 * PEER_HEADS, PEER_KEYS, PEER_HALF).astype(BF16)
    wq = w_q.astype(BF16)
    step = 2 * SC_WORKERS * GATHER_ROWS
    n_blk = next(k for k in (8, 6, 4, 3, 2, 1)
                 if n % (k * PEER_SELECT_TOKENS) == 0 and (n // k * PEER_SLOTS) % step == 0)
    nb = n // n_blk
    outs = []
    for i in range(n_blk):
        h_b = h[i * nb:(i + 1) * nb]
        ids_t, gate_t = peer_select(h_b, wq, keys)
        flat_ids = ids_t.T.reshape(-1)
        ug = sc_gather_rows(u_packed, flat_ids).reshape(nb, PEER_SLOTS, d // 2)
        vg = sc_gather_rows(v_packed, flat_ids).reshape(nb, PEER_SLOTS, d // 2)
        outs.append(peer_experts(h_b, gate_t.T, ug, vg))
    return jnp.concatenate(outs, axis=0)


def kernel(x, c, ctx, c_ctx, ada_w, ada_b, norm1_g, norm2_g, w_in, rwkv_shift, rwkv_w0, rwkv_w2,
           rwkv_a0, rwkv_a2, rwkv_g2, rwkv_kk, rwkv_ka, rwkv_rk, rwkv_gn_g, rwkv_gn_b,
           conv_dw, conv_db, conv_ln_g, conv_ln_b, s5_lam_re, s5_lam_im, s5_log_dt,
           s5_b_re, s5_b_im, s5_c_re, s5_c_im, s5_d, s5_glu_w, s5_glu_b, ret_gn_g, ret_gn_b,
           w_branch, w_merge, b_merge, w_out, peer_wq, peer_keys, peer_u, peer_v, final_g):
    depth = ada_w.shape[0]
    bsz, n_lat, d = x.shape
    n_ctx = ctx.shape[1]
    assert n_ctx == ROW_TILE and n_lat % ROW_TILE == 0
    xa = jnp.concatenate([ctx, x], axis=1)
    cos2, sin2 = rope_tables(n_lat, n_ctx)
    ones_bd = jnp.kron(jnp.eye(RWKV_HEADS, dtype=F32), jnp.ones((RWKV_HEAD_DIM, RWKV_HEAD_DIM), F32)).astype(BF16)
    cond = jnp.zeros((8, d), F32).at[:bsz].set(c).at[bsz].set(c_ctx)
    row = lambda v: v.reshape(1, -1)
    w = RWKV_WIDTH
    for l in range(depth):
        mod = ada_modulation(cond, ada_w[l], row(ada_b[l])).reshape(8, 6, d)
        mods = jnp.stack([jnp.broadcast_to(mod[bsz], (bsz, 6, d)), mod[:bsz]], axis=1)
        g1 = row(norm1_g[l])
        w_in_l = w_in[l].astype(BF16)
        p_a, p_b, p_c, p_d = (in_proj(xa, g1, mods, w_in_l[:, lo:hi])
                              for lo, hi in ((0, A_END), (A_END, B_END), (B_END, C_END), (C_END, IN_WIDTH)))
        half = jnp.zeros((RWKV_DECAY_RANK, w), F32)
        w2 = jnp.stack([jnp.concatenate([rwkv_w2[l, 0], half]), jnp.concatenate([half, rwkv_w2[l, 1]])]).astype(BF16)
        a2 = jnp.stack([jnp.concatenate([rwkv_a2[l, 0], half]), jnp.concatenate([half, rwkv_a2[l, 1]])]).astype(BF16)
        vecs = jnp.stack([rwkv_kk[l], rwkv_ka[l], rwkv_rk[l].reshape(w), jnp.zeros((w,), F32),
                          rwkv_w0[l, 0], rwkv_w0[l, 1], rwkv_a0[l, 0], rwkv_a0[l, 1]])
        r, v, kk, gate_bonus, k_dir, b_dir, lw_dir = rwkv_prep(
            p_a, rwkv_shift[l], w2, a2, rwkv_g2[l].astype(BF16), vecs, ones_bd)
        y = rwkv_scan(r, v, kk, k_dir, b_dir, lw_dir, n_ctx)
        ya = rwkv_out(y, gate_bonus, jnp.stack([rwkv_gn_g[l], rwkv_gn_b[l]]), ones_bd)
        yb = conformer_conv(p_b, conv_dw[l], conv_db[l], conv_ln_g[l], conv_ln_b[l])
        eye = jnp.eye(S5_GROUPS, dtype=F32)
        n_state = S5_GROUPS * S5_STATE
        y_dirs = []
        for dr in range(2):
            ab_re, ab_im, bb_re, bb_im = _s5_discretise(s5_lam_re[l, dr], s5_lam_im[l, dr], s5_log_dt[l, dr],
                                                        s5_b_re[l, dr], s5_b_im[l, dr])
            blk_in = lambda bb: jnp.einsum("gph,gk->ghkp", bb, eye).reshape(S5_WIDTH, n_state).astype(BF16)
            blk_out = lambda cc: jnp.einsum("ghp,gk->kpgh", cc.astype(F32), eye).reshape(n_state, S5_WIDTH)
            c_cat = jnp.concatenate([blk_out(s5_c_re[l, dr]), -blk_out(s5_c_im[l, dr])], axis=0).astype(BF16)
            pw_re, pw_im = [ab_re.reshape(n_state)], [ab_im.reshape(n_state)]
            for _ in range(7):
                pw_re, pw_im = (pw_re + [pw_re[-1] * pw_re[0] - pw_im[-1] * pw_im[0]],
                                pw_im + [pw_re[-1] * pw_im[0] + pw_im[-1] * pw_re[0]])
            order = slice(None, None, -1) if dr == 1 else slice(None)
            y_dirs.append(s5_scan(p_c, blk_in(bb_re), blk_in(bb_im), c_cat, jnp.stack(pw_re[order]),
                                  jnp.stack(pw_im[order]), n_ctx, reverse=(dr == 1)))
        yc = s5_out(p_c, y_dirs[0], y_dirs[1], row(s5_d[l]), s5_glu_w[l].astype(BF16), row(s5_glu_b[l]))
        yd = retention_out(retention_scan(p_d, cos2, sin2, n_ctx), p_d, ret_gn_g[l], ret_gn_b[l])
        xa = merge_residual(xa, g1, mods, (ya, yb, yc, yd), w_merge[l].astype(BF16), b_merge[l],
                            w_branch[l].astype(BF16), w_out[l].astype(BF16))
        h = norm_modulate2(xa, row(norm2_g[l]), mods)
        f = _peer_ffn_dense(h.reshape(-1, d), peer_wq[l], peer_keys[l],
                            peer_u[l].astype(BF16), peer_v[l].astype(BF16)).reshape(bsz, -1, d)
        xa = residual2(xa, f, mods, row(final_g), final_norm=(l == depth - 1))
    return xa[:, n_ctx:]
```

```python
import functools
import math

import jax
import jax.numpy as jnp
from jax import lax
from jax.experimental import pallas as pl
from jax.experimental.pallas import tpu as pltpu
from jax.experimental.pallas import tpu_sc as plsc

F32 = jnp.float32
BF16 = jnp.bfloat16

D_MODEL = 1024
GRID_W = 64
EPS = 1e-6

RWKV_HEADS = 8
RWKV_HEAD_DIM = 64
RWKV_WIDTH = RWKV_HEADS * RWKV_HEAD_DIM
RWKV_DECAY_RANK = 64
RWKV_ICLR_RANK = 64
RWKV_GATE_RANK = 128
RWKV_CHUNK = 64
LANES = 128


def _bdot(a, b, dims):
    return lax.dot_general(a.astype(BF16), b.astype(BF16), (dims, ((), ())), preferred_element_type=F32)


def _mm(a, b):
    return _bdot(a, b, ((1,), (0,)))


def _mm_nt(a, b):
    return _bdot(a, b, ((1,), (1,)))


def _mm_tn(a, b):
    return _bdot(a, b, ((0,), (0,)))


def _rwkv_chunk_kernel(r_ref, v_ref, kk_ref, k_ref, b_ref, lw_ref, y_ref, s_ref):
    c = RWKV_CHUNK
    d = pl.program_id(1)
    j = pl.program_id(2)

    @pl.when(j == 0)
    def _():
        s_ref[...] = jnp.zeros_like(s_ref)

    sign = jnp.where(d == 0, 1, -1)
    row = lax.broadcasted_iota(jnp.int32, (c, c), 0)
    col = lax.broadcasted_iota(jnp.int32, (c, c), 1)
    cum_mat = jnp.where((row - col) * sign >= 0, 1.0, 0.0).astype(BF16)

    lw = lw_ref[0, 0]
    lw_hi = lw.astype(BF16)
    rem = lw - lw_hi.astype(F32)
    lw_mid = rem.astype(BF16)
    lw_lo = (rem - lw_mid.astype(F32)).astype(BF16)
    lc_incl = _mm(cum_mat, lw_hi) + _mm(cum_mat, lw_mid) + _mm(cum_mat, lw_lo)
    lc_excl = lc_incl - lw
    lc_ref = 0.5 * jnp.sum(lw, axis=0, keepdims=True)

    row2 = lax.broadcasted_iota(jnp.int32, (2 * c, 2 * c), 0)
    col2 = lax.broadcasted_iota(jnp.int32, (2 * c, 2 * c), 1)
    same_head = (row2 // c) == (col2 // c)
    tdiff = jnp.where(same_head, ((row2 % c) - (col2 % c)) * sign, -1)
    strict2 = tdiff > 0
    incl2 = tdiff >= 0
    eye2 = jnp.where(row2 == col2, 1.0, 0.0)
    lane_head = lax.broadcasted_iota(jnp.int32, (c, LANES), 1) // RWKV_HEAD_DIM

    def stack2(z):
        return jnp.concatenate([jnp.where(lane_head == 0, z, 0.0), jnp.where(lane_head == 1, z, 0.0)], axis=0)

    pairs = range(RWKV_WIDTH // LANES)
    lane = lambda p: slice(p * LANES, (p + 1) * LANES)
    bf = lambda z: z.astype(BF16)
    x, q0_stack, v2, kb_end, decay_c, s0 = [], [], [], [], [], []
    for p in pairs:
        sl = lane(p)
        r, v, kk = r_ref[0, :, sl], v_ref[0, :, sl], kk_ref[0, :, sl]
        k, b = k_ref[0, 0, :, sl], b_ref[0, 0, :, sl]
        ref = lc_ref[:, sl]
        e_in = jnp.exp(lc_incl[:, sl] - ref)
        e_ex = jnp.exp(lc_excl[:, sl] - ref)
        e_inv = jnp.exp(ref - lc_incl[:, sl])
        e_ref = jnp.exp(ref)
        e_end = e_inv * e_ref
        q_stack = jnp.concatenate([stack2(kk * e_ex), stack2(r * e_in)], axis=0)
        k_stack = jnp.concatenate([stack2(k * e_inv), stack2(b * e_inv)], axis=0)
        x.append(_mm_nt(q_stack, k_stack))
        q0_stack.append(bf(jnp.concatenate([stack2(kk * (e_ex * e_ref)), stack2(r * (e_in * e_ref))], axis=0)))
        v2.append(bf(stack2(v)))
        kb_end.append(bf(jnp.concatenate([stack2(k * e_end), stack2(-(b * e_end))], axis=0)))
        decay_c.append(e_ref * e_ref)
        s0.append(s_ref[p])
    m_b = [jnp.where(strict2, x[p][:2 * c, 2 * c:], 0.0) for p in pairs]
    p_b = [bf(jnp.where(incl2, x[p][2 * c:, 2 * c:], 0.0)) for p in pairs]
    mk_pk = [bf(jnp.concatenate([jnp.where(strict2, x[p][:2 * c, :2 * c], 0.0),
                                 jnp.where(incl2, x[p][2 * c:, :2 * c], 0.0)], axis=0)) for p in pairs]
    from_state = [_mm_nt(q0_stack[p], s0[p]) for p in pairs]
    from_chunk = [_mm(mk_pk[p], v2[p]) for p in pairs]
    t_inv = [eye2 - m_b[p] for p in pairs]
    m_pow = [bf(m_b[p]) for p in pairs]
    for _ in range(int(math.log2(c)) - 1):
        m_pow = [bf(_mm(m_pow[p], m_pow[p])) for p in pairs]
        t_inv = [t_inv[p] + _mm(t_inv[p], m_pow[p]) for p in pairs]
    u2 = [_mm(t_inv[p], from_state[p][:2 * c] + from_chunk[p][:2 * c]) for p in pairs]
    y2 = [from_state[p][2 * c:] + from_chunk[p][2 * c:] - _mm(p_b[p], u2[p]) for p in pairs]
    s_new = [s0[p] * decay_c[p] + _mm_tn(jnp.concatenate([v2[p], bf(u2[p])], axis=0), kb_end[p]) for p in pairs]
    for p in pairs:
        y_ref[0, 0, :, lane(p)] = y2[p][:c] + y2[p][c:]
        s_ref[p] = s_new[p]


def rwkv_scan(r, v, kk, k_dir, b_dir, lw_dir, n_ctx):
    bsz, length, width = r.shape
    c = RWKV_CHUNK
    nc, nc_ctx = length // c, n_ctx // c

    def chunk_of(d, j):
        back = jnp.where(j < nc_ctx, nc_ctx - 1 - j, nc + nc_ctx - 1 - j)
        return jnp.where(d == 0, j, back)

    shared = pl.BlockSpec((1, c, width), lambda b, d, j: (b, chunk_of(d, j), 0))
    per_dir = pl.BlockSpec((1, 1, c, width), lambda b, d, j: (b, d, chunk_of(d, j), 0))
    return pl.pallas_call(
        _rwkv_chunk_kernel,
        grid=(bsz, 2, nc),
        in_specs=[shared, shared, shared, per_dir, per_dir, per_dir],
        out_specs=per_dir,
        out_shape=jax.ShapeDtypeStruct((bsz, 2, length, width), F32),
        scratch_shapes=[pltpu.VMEM((width // LANES, LANES, LANES), F32)],
        compiler_params=pltpu.CompilerParams(dimension_semantics=("arbitrary", "arbitrary", "arbitrary")),
        name="rwkv_scan",
    )(r, v, kk, k_dir, b_dir, lw_dir)


S5_CHUNK = 128


def _s5_chunk_kernel(u_ref, bre_ref, bim_ref, c_ref, are_ref, aim_ref, y_ref, carry_ref, *, reverse):
    tc = S5_CHUNK
    j = pl.program_id(1)

    @pl.when(j == 0)
    def _():
        carry_ref[...] = jnp.zeros_like(carry_ref)

    u = u_ref[0]
    xr = _mm(u, bre_ref[...])
    xi = _mm(u, bim_ref[...])
    pw_r, pw_i = are_ref[...], aim_ref[...]
    a_pow = lambda n: (pw_r[8 - n:9 - n], pw_i[8 - n:9 - n]) if reverse else (pw_r[n - 1:n], pw_i[n - 1:n])
    sub = lax.broadcasted_iota(jnp.int32, xr.shape, 0) % 8
    for sh in (1, 2, 4):
        ar, ai = a_pow(sh)
        if reverse:
            sr, si = pltpu.roll(xr, tc - sh, 0), pltpu.roll(xi, tc - sh, 0)
            keep = sub < 8 - sh
        else:
            sr, si = pltpu.roll(xr, sh, 0), pltpu.roll(xi, sh, 0)
            keep = sub >= sh
        sr, si = jnp.where(keep, sr, 0.0), jnp.where(keep, si, 0.0)
        xr, xi = xr + (ar * sr - ai * si), xi + (ar * si + ai * sr)
    a8r, a8i = a_pow(8)
    cr, ci = carry_ref[0:1, :], carry_ref[1:2, :]
    n_groups = tc // 8
    enter_r, enter_i = [None] * n_groups, [None] * n_groups
    for g in (range(n_groups - 1, -1, -1) if reverse else range(n_groups)):
        enter_r[g], enter_i[g] = cr, ci
        close = 8 * g if reverse else 8 * g + 7
        cr, ci = (xr[close:close + 1, :] + (a8r * cr - a8i * ci), xi[close:close + 1, :] + (a8r * ci + a8i * cr))
    carry_ref[0:1, :] = cr
    carry_ref[1:2, :] = ci
    er = jnp.concatenate([jnp.broadcast_to(z, (8, z.shape[1])) for z in enter_r], axis=0)
    ei = jnp.concatenate([jnp.broadcast_to(z, (8, z.shape[1])) for z in enter_i], axis=0)
    tr, ti = jnp.tile(pw_r, (n_groups, 1)), jnp.tile(pw_i, (n_groups, 1))
    xr, xi = xr + (tr * er - ti * ei), xi + (tr * ei + ti * er)
    y_ref[0] = _mm(jnp.concatenate([xr, xi], axis=1), c_ref[...])


def s5_scan(u, b_re, b_im, c_cat, a_re, a_im, n_ctx, reverse):
    bsz, length, width = u.shape
    n_state = b_re.shape[1]
    tc = S5_CHUNK
    nc, nc_ctx = length // tc, n_ctx // tc

    def chunk_of(j):
        if not reverse:
            return j
        return jnp.where(j < nc_ctx, nc_ctx - 1 - j, nc + nc_ctx - 1 - j)

    tok = pl.BlockSpec((1, tc, width), lambda b, j: (b, chunk_of(j), 0))
    full = lambda shape: pl.BlockSpec(shape, lambda b, j: (0,) * len(shape))
    return pl.pallas_call(
        functools.partial(_s5_chunk_kernel, reverse=reverse),
        grid=(bsz, nc),
        in_specs=[tok, full(b_re.shape), full(b_im.shape), full(c_cat.shape), full(a_re.shape), full(a_im.shape)],
        out_specs=tok,
        out_shape=jax.ShapeDtypeStruct((bsz, length, width), F32),
        scratch_shapes=[pltpu.VMEM((8, n_state), F32)],
        compiler_params=pltpu.CompilerParams(dimension_semantics=("arbitrary", "arbitrary"),
                                             vmem_limit_bytes=48 * 1024 * 1024),
        name="s5_scan_rev" if reverse else "s5_scan_fwd",
    )(u, b_re, b_im, c_cat, a_re, a_im)


RET_HEADS = 4
RET_HEAD_DIM = 128
RET_WIDTH = RET_HEADS * RET_HEAD_DIM
RET_CHUNK = 128
ROPE_BASE = 10000.0
ROW_TILE = 256


def _ret_chunk_kernel(q_ref, k_ref, v_ref, cos_ref, sin_ref, dm_ref, dq_ref, dk_ref, o_ref, s_ref, *, chunk_decay):
    j = pl.program_id(2)

    @pl.when(j == 0)
    def _():
        s_ref[...] = jnp.zeros_like(s_ref)

    cos2, sin2 = cos_ref[...], sin_ref[...]
    rope = lambda z: z * cos2 + pltpu.roll(z, RET_HEAD_DIM // 2, 1) * sin2
    for h in range(RET_HEADS):
        sl = slice(h * RET_HEAD_DIM, (h + 1) * RET_HEAD_DIM)
        q = rope(q_ref[0, :, sl])
        k = rope(k_ref[0, :, sl] * (RET_HEAD_DIM ** -0.5))
        v = v_ref[0, :, sl]
        s0 = s_ref[h]
        scores = _mm_nt(q, k) * dm_ref[0, h]
        o_ref[0, 0, :, sl] = _mm(scores, v) + _mm(q * dq_ref[0, h], s0)
        s_ref[h] = chunk_decay[h] * s0 + _mm_tn(k * dk_ref[0, h], v)


def _ret_decay_tables():
    c = RET_CHUNK
    lg = jnp.log1p(-jnp.exp2(-5.0 - jnp.arange(RET_HEADS, dtype=F32)))[:, None, None]
    n = jnp.arange(c, dtype=F32)[:, None]
    m = jnp.arange(c, dtype=F32)[None, :]
    fwd = jnp.where(n >= m, jnp.exp(jnp.where(n >= m, n - m, 0.0) * lg), 0.0)
    bwd = jnp.where(m > n, jnp.exp(jnp.where(m > n, m - n, 0.0) * lg), 0.0)
    ones = jnp.ones((1, c), F32)
    dq = jnp.stack([jnp.exp((n + 1.0) * lg) * ones, jnp.exp((c - n) * lg) * ones])
    dk = jnp.stack([jnp.exp((c - 1.0 - n) * lg) * ones, jnp.exp(n * lg) * ones])
    return jnp.stack([fwd, bwd]), dq, dk


def retention_scan(p_ret, cos2, sin2, n_ctx):
    bsz, length, _ = p_ret.shape
    c, w = RET_CHUNK, RET_WIDTH
    nc, nc_ctx = length // c, n_ctx // c
    dm, dq, dk = _ret_decay_tables()
    chunk_decay = tuple(math.exp(c * math.log1p(-2.0 ** (-5 - h))) for h in range(RET_HEADS))

    def chunk_of(d, j):
        back = jnp.where(j < nc_ctx, nc_ctx - 1 - j, nc + nc_ctx - 1 - j)
        return jnp.where(d == 0, j, back)

    col = lambda which: pl.BlockSpec((1, c, w), lambda b, d, j: (b, chunk_of(d, j), which(d)))
    rope_spec = pl.BlockSpec((c, RET_HEAD_DIM), lambda b, d, j: (chunk_of(d, j), 0))
    table = pl.BlockSpec((1, RET_HEADS, c, c), lambda b, d, j: (d, 0, 0, 0))
    return pl.pallas_call(
        functools.partial(_ret_chunk_kernel, chunk_decay=chunk_decay),
        grid=(bsz, 2, nc),
        in_specs=[col(lambda d: 0), col(lambda d: 1 + d), col(lambda d: 3), rope_spec, rope_spec, table, table, table],
        out_specs=pl.BlockSpec((1, 1, c, w), lambda b, d, j: (b, d, chunk_of(d, j), 0)),
        out_shape=jax.ShapeDtypeStruct((bsz, 2, length, w), F32),
        scratch_shapes=[pltpu.VMEM((RET_HEADS, RET_HEAD_DIM, RET_HEAD_DIM), F32)],
        compiler_params=pltpu.CompilerParams(dimension_semantics=("arbitrary", "arbitrary", "arbitrary")),
        name="retention_scan",
    )(p_ret, p_ret, p_ret, cos2, sin2, dm, dq, dk)


def _ret_out_kernel(o_ref, g_ref, gn_g_ref, gn_b_ref, y_ref):
    o = o_ref[0, 0] + o_ref[0, 1]
    g = g_ref[0]
    for h in range(RET_HEADS):
        sl = slice(h * RET_HEAD_DIM, (h + 1) * RET_HEAD_DIM)
        z = o[:, sl]
        zc = z - jnp.mean(z, axis=1, keepdims=True)
        zn = zc * lax.rsqrt(jnp.mean(zc * zc, axis=1, keepdims=True) + EPS)
        gate = g[:, sl]
        y_ref[0, :, sl] = (zn * gn_g_ref[:, sl] + gn_b_ref[:, sl]) * (gate * jax.nn.sigmoid(gate))


def retention_out(o, p_ret, gn_g, gn_b):
    bsz, _, length, w = o.shape
    tm = ROW_TILE
    vec = pl.BlockSpec((1, w), lambda b, i: (0, 0))
    return pl.pallas_call(
        _ret_out_kernel,
        grid=(bsz, length // tm),
        in_specs=[pl.BlockSpec((1, 2, tm, w), lambda b, i: (b, 0, i, 0)),
                  pl.BlockSpec((1, tm, w), lambda b, i: (b, i, 4)), vec, vec],
        out_specs=pl.BlockSpec((1, tm, w), lambda b, i: (b, i, 0)),
        out_shape=jax.ShapeDtypeStruct((bsz, length, w), F32),
        compiler_params=pltpu.CompilerParams(dimension_semantics=("arbitrary", "arbitrary")),
        name="retention_out",
    )(o, p_ret, gn_g.reshape(1, w), gn_b.reshape(1, w))


def rope_tables(n_tokens, n_ctx):
    rows = n_tokens // GRID_W
    row = jnp.repeat(jnp.arange(rows, dtype=F32), GRID_W)
    col = jnp.tile(jnp.arange(GRID_W, dtype=F32), rows)
    n_freq = RET_HEAD_DIM // 4
    inv = ROPE_BASE ** (-jnp.arange(n_freq, dtype=F32) / n_freq)
    ang = jnp.concatenate([row[:, None] * inv, col[:, None] * inv], axis=-1)
    cos, sin = jnp.cos(ang), jnp.sin(ang)
    cos2 = jnp.concatenate([jnp.ones((n_ctx, RET_HEAD_DIM), F32), jnp.concatenate([cos, cos], axis=-1)], axis=0)
    sin2 = jnp.concatenate([jnp.zeros((n_ctx, RET_HEAD_DIM), F32), jnp.concatenate([-sin, sin], axis=-1)], axis=0)
    return cos2, sin2


VMEM_LIMIT = 56 * 1024 * 1024


def _const_spec(a):
    return pl.BlockSpec(a.shape, lambda b, i: (0,) * a.ndim, pipeline_mode=pl.Buffered(1))


def _mod_spec(d):
    return pl.BlockSpec((1, 1, 6, d), lambda b, i: (b, jnp.minimum(i, 1), 0, 0))


def _tok_spec(width, col=0):
    return pl.BlockSpec((1, ROW_TILE, width), lambda b, i: (b, i, col))


def _norm_mod(x, g, mod, shift_row, scale_row):
    y = x * lax.rsqrt(jnp.mean(x * x, axis=1, keepdims=True) + EPS) * g
    return y * (1.0 + mod[scale_row:scale_row + 1]) + mod[shift_row:shift_row + 1]


def _in_proj_kernel(x_ref, g_ref, mod_ref, w_ref, o_ref):
    n = _norm_mod(x_ref[0], g_ref[...], mod_ref[0, 0], 0, 1)
    o_ref[0] = _mm(n, w_ref[...])


def in_proj(x, norm_g, mods, w):
    bsz, length, d = x.shape
    n_out = w.shape[1]
    return pl.pallas_call(
        _in_proj_kernel,
        grid=(bsz, length // ROW_TILE),
        in_specs=[_tok_spec(d), _const_spec(norm_g), _mod_spec(d), _const_spec(w)],
        out_specs=_tok_spec(n_out),
        out_shape=jax.ShapeDtypeStruct((bsz, length, n_out), F32),
        compiler_params=pltpu.CompilerParams(dimension_semantics=("arbitrary", "arbitrary"),
                                             vmem_limit_bytes=VMEM_LIMIT),
        name="in_proj",
    )(x, norm_g, mods, w)


ADA_COLS = 1024


def _ada_kernel(c_ref, w_ref, b_ref, o_ref):
    cv = c_ref[...]
    o_ref[...] = _mm(cv * jax.nn.sigmoid(cv), w_ref[...]) + b_ref[...]


def ada_modulation(cond, w, b):
    rows, d = cond.shape
    n_out = w.shape[1]
    return pl.pallas_call(
        _ada_kernel,
        grid=(n_out // ADA_COLS,),
        in_specs=[pl.BlockSpec((rows, d), lambda j: (0, 0)), pl.BlockSpec((d, ADA_COLS), lambda j: (0, j)),
                  pl.BlockSpec((1, ADA_COLS), lambda j: (0, j))],
        out_specs=pl.BlockSpec((rows, ADA_COLS), lambda j: (0, j)),
        out_shape=jax.ShapeDtypeStruct((rows, n_out), F32),
        compiler_params=pltpu.CompilerParams(dimension_semantics=("arbitrary",)),
        name="ada_modulation",
    )(cond, w, b)


def _norm_mod_kernel(x_ref, g_ref, mod_ref, o_ref):
    o_ref[0] = _norm_mod(x_ref[0], g_ref[...], mod_ref[0, 0], 3, 4)


def norm_modulate2(x, norm_g, mods):
    bsz, length, d = x.shape
    return pl.pallas_call(
        _norm_mod_kernel,
        grid=(bsz, length // ROW_TILE),
        in_specs=[_tok_spec(d), _const_spec(norm_g), _mod_spec(d)],
        out_specs=_tok_spec(d),
        out_shape=jax.ShapeDtypeStruct((bsz, length, d), F32),
        compiler_params=pltpu.CompilerParams(dimension_semantics=("arbitrary", "arbitrary")),
        name="norm_modulate2",
    )(x, norm_g, mods)


def _residual_kernel(x_ref, f_ref, mod_ref, g_ref, o_ref, *, final_norm):
    y = x_ref[0] + mod_ref[0, 0][5:6] * f_ref[0]
    if final_norm:
        y = y * lax.rsqrt(jnp.mean(y * y, axis=1, keepdims=True) + EPS) * g_ref[...]
    o_ref[0] = y


def residual2(x, f, mods, final_g, final_norm):
    bsz, length, d = x.shape
    return pl.pallas_call(
        functools.partial(_residual_kernel, final_norm=final_norm),
        grid=(bsz, length // ROW_TILE),
        in_specs=[_tok_spec(d), _tok_spec(d), _mod_spec(d), _const_spec(final_g)],
        out_specs=_tok_spec(d),
        out_shape=jax.ShapeDtypeStruct((bsz, length, d), F32),
        compiler_params=pltpu.CompilerParams(dimension_semantics=("arbitrary", "arbitrary")),
        name="residual2",
    )(x, f, mods, final_g)


def _merge_kernel(x_ref, g_ref, mod_ref, ya_ref, yb_ref, yc_ref, yd_ref, wg_ref, bg_ref, wbr_ref, wout_ref, o_ref):
    x = x_ref[0]
    mod = mod_ref[0, 0]
    n = _norm_mod(x, g_ref[...], mod, 0, 1).astype(BF16)
    m = jnp.zeros(x.shape, F32)
    for i, y_ref in enumerate((ya_ref, yb_ref, yc_ref, yd_ref)):
        gate = jax.nn.sigmoid(_mm(n, wg_ref[i]) + bg_ref[i:i + 1])
        m = m + gate * _mm(y_ref[0], wbr_ref[i])
    o_ref[0] = x + mod[2:3] * _mm(m, wout_ref[...])


def merge_residual(x, norm_g, mods, ys, w_merge, b_merge, w_branch, w_out):
    bsz, length, d = x.shape
    bw = ys[0].shape[-1]
    return pl.pallas_call(
        _merge_kernel,
        grid=(bsz, length // ROW_TILE),
        in_specs=[_tok_spec(d), _const_spec(norm_g), _mod_spec(d)] + [_tok_spec(bw)] * 4
        + [_const_spec(w_merge), _const_spec(b_merge), _const_spec(w_branch), _const_spec(w_out)],
        out_specs=_tok_spec(d),
        out_shape=jax.ShapeDtypeStruct((bsz, length, d), F32),
        compiler_params=pltpu.CompilerParams(dimension_semantics=("arbitrary", "arbitrary"),
                                             vmem_limit_bytes=VMEM_LIMIT),
        name="merge_residual",
    )(x, norm_g, mods, *ys, w_merge, b_merge, w_branch, w_out)


def _halo_specs(width, n_tiles):
    prev = pl.BlockSpec((1, ROW_TILE, width), lambda b, i: (b, jnp.maximum(i - 1, 0), 0))
    nxt = pl.BlockSpec((1, ROW_TILE, width), lambda b, i: (b, jnp.minimum(i + 1, n_tiles - 1), 0))
    return [prev, _tok_spec(width), nxt]


def _segment_edges(n_tiles):
    i = pl.program_id(1)
    return i >= 2, jnp.logical_and(i >= 1, i < n_tiles - 1)


CONV_CHANNELS = 512
CONV_TAPS = 31
HALO = 16


def _conformer_kernel(prev_ref, cur_ref, next_ref, dw_ref, db_ref, lng_ref, lnb_ref, o_ref, ext_ref, *, n_tiles):
    ch = CONV_CHANNELS
    glu = lambda p: p[:, :ch] * jax.nn.sigmoid(p[:, ch:])
    has_prev, has_next = _segment_edges(n_tiles)
    ext_ref[0:HALO, :] = jnp.where(has_prev, glu(prev_ref[0, ROW_TILE - HALO:, :]), 0.0)
    ext_ref[HALO:HALO + ROW_TILE, :] = glu(cur_ref[0])
    ext_ref[HALO + ROW_TILE:, :] = jnp.where(has_next, glu(next_ref[0, :HALO, :]), 0.0)
    pad = (CONV_TAPS - 1) // 2
    acc = jnp.zeros((ROW_TILE, ch), F32) + db_ref[...]
    for j in range(CONV_TAPS):
        acc = acc + ext_ref[pl.ds(HALO - pad + j, ROW_TILE), :] * dw_ref[j:j + 1, :]
    zc = acc - jnp.mean(acc, axis=1, keepdims=True)
    z = zc * lax.rsqrt(jnp.mean(zc * zc, axis=1, keepdims=True) + EPS) * lng_ref[...] + lnb_ref[...]
    o_ref[0] = z * jax.nn.sigmoid(z)


def conformer_conv(p, dw, db, ln_g, ln_b):
    bsz, length, width = p.shape
    n_tiles = length // ROW_TILE
    ch = CONV_CHANNELS
    row = lambda v: v.reshape(1, ch)
    return pl.pallas_call(
        functools.partial(_conformer_kernel, n_tiles=n_tiles),
        grid=(bsz, n_tiles),
        in_specs=_halo_specs(width, n_tiles) + [_const_spec(dw)] + [_const_spec(row(db))] * 3,
        out_specs=_tok_spec(ch),
        out_shape=jax.ShapeDtypeStruct((bsz, length, ch), F32),
        scratch_shapes=[pltpu.VMEM((ROW_TILE + 2 * HALO, ch), F32)],
        compiler_params=pltpu.CompilerParams(dimension_semantics=("arbitrary", "arbitrary"),
                                             vmem_limit_bytes=VMEM_LIMIT),
        name="conformer_conv",
    )(p, p, p, dw, row(db), row(ln_g), row(ln_b))


def _s5_out_kernel(u_ref, yf_ref, yb_ref, d_ref, w_ref, b_ref, o_ref):
    z = _gelu_tanh(d_ref[...] * u_ref[0] + yf_ref[0] + yb_ref[0])
    o_ref[0] = z * jax.nn.sigmoid(_mm(z, w_ref[...]) + b_ref[...])


def s5_out(u, y_fwd, y_bwd, d_skip, glu_w, glu_b):
    bsz, length, w = u.shape
    return pl.pallas_call(
        _s5_out_kernel,
        grid=(bsz, length // ROW_TILE),
        in_specs=[_tok_spec(w)] * 3 + [_const_spec(d_skip), _const_spec(glu_w), _const_spec(glu_b)],
        out_specs=_tok_spec(w),
        out_shape=jax.ShapeDtypeStruct((bsz, length, w), F32),
        compiler_params=pltpu.CompilerParams(dimension_semantics=("arbitrary", "arbitrary")),
        name="s5_out",
    )(u, y_fwd, y_bwd, d_skip, glu_w, glu_b)


def _head_sums(z, ones_bd):
    hi = z.astype(BF16)
    lo = (z - hi.astype(F32)).astype(BF16)
    return _mm(hi, ones_bd) + _mm(lo, ones_bd)


def _rwkv_prep_kernel(prev_ref, cur_ref, next_ref, shift_ref, w2_ref, a2_ref, g2_ref, vec_ref, bd_ref,
                      r_ref, v_ref, kk_ref, gb_ref, k_ref, b_ref, lw_ref, ext_ref, *, n_tiles):
    w = RWKV_WIDTH
    has_prev, has_next = _segment_edges(n_tiles)
    ext_ref[0:8, :] = jnp.where(has_prev, prev_ref[0, ROW_TILE - 8:, :], 0.0)
    ext_ref[8:8 + ROW_TILE, :] = cur_ref[0]
    ext_ref[8 + ROW_TILE:, :] = jnp.where(has_next, next_ref[0, :8, :], 0.0)
    p = (ext_ref[pl.ds(7, ROW_TILE), :] * shift_ref[0:1, :] + ext_ref[pl.ds(8, ROW_TILE), :] * shift_ref[1:2, :]
         + ext_ref[pl.ds(9, ROW_TILE), :] * shift_ref[2:3, :])
    r, k, v = p[:, :w], p[:, w:2 * w], p[:, 2 * w:3 * w]
    lowrank_w = jnp.tanh(p[:, 3 * w:3 * w + LANES])
    lowrank_a = p[:, 3 * w + LANES:3 * w + 2 * LANES]
    gl = p[:, 3 * w + 2 * LANES:]
    ones_bd = bd_ref[...]
    kk = k * vec_ref[0:1, :]
    kk = kk * lax.rsqrt(_head_sums(kk * kk, ones_bd) + EPS)
    r_ref[0], v_ref[0], kk_ref[0] = r, v, kk
    gb_ref[0, 0] = _mm(jax.nn.sigmoid(gl), g2_ref[...])
    k_sum = jnp.zeros_like(k)
    for d in range(2):
        w_log = -jax.nn.softplus(-(vec_ref[4 + d:5 + d, :] + _mm(lowrank_w, w2_ref[d]))) - 0.5
        a = jax.nn.sigmoid(vec_ref[6 + d:7 + d, :] + _mm(lowrank_a, a2_ref[d]))
        k_d = k * (1.0 + (a - 1.0) * vec_ref[1:2, :])
        k_ref[0, d], b_ref[0, d], lw_ref[0, d] = k_d, a * kk, -jnp.exp(w_log)
        k_sum = k_sum + k_d
    gb_ref[0, 1] = _head_sums(r * k_sum * vec_ref[2:3, :], ones_bd) * v


def rwkv_prep(p, shift_w, w2, a2, g2, vecs, ones_bd):
    bsz, length, width = p.shape
    n_tiles = length // ROW_TILE
    w = RWKV_WIDTH
    shared = jax.ShapeDtypeStruct((bsz, length, w), F32)
    per_dir = jax.ShapeDtypeStruct((bsz, 2, length, w), F32)
    dir_spec = pl.BlockSpec((1, 2, ROW_TILE, w), lambda b, i: (b, 0, i, 0))
    return pl.pallas_call(
        functools.partial(_rwkv_prep_kernel, n_tiles=n_tiles),
        grid=(bsz, n_tiles),
        in_specs=_halo_specs(width, n_tiles) + [_const_spec(a) for a in (shift_w, w2, a2, g2, vecs, ones_bd)],
        out_specs=[_tok_spec(w)] * 3 + [dir_spec] * 4,
        out_shape=[shared] * 3 + [per_dir] * 4,
        scratch_shapes=[pltpu.VMEM((ROW_TILE + 16, width), F32)],
        compiler_params=pltpu.CompilerParams(dimension_semantics=("arbitrary", "arbitrary"),
                                             vmem_limit_bytes=VMEM_LIMIT),
        name="rwkv_prep",
    )(p, p, p, shift_w, w2, a2, g2, vecs, ones_bd)


def _rwkv_out_kernel(y_ref, gb_ref, gn_ref, bd_ref, o_ref):
    y = y_ref[0, 0] + y_ref[0, 1]
    ones_bd = bd_ref[...]
    inv = 1.0 / RWKV_HEAD_DIM
    yc = y - _head_sums(y, ones_bd) * inv
    yn = yc * lax.rsqrt(_head_sums(yc * yc, ones_bd) * inv + EPS)
    o_ref[0] = (yn * gn_ref[0:1, :] + gn_ref[1:2, :] + gb_ref[0, 1]) * gb_ref[0, 0]


def rwkv_out(y, gate_bonus, gn, ones_bd):
    bsz, _, length, w = y.shape
    dir_spec = pl.BlockSpec((1, 2, ROW_TILE, w), lambda b, i: (b, 0, i, 0))
    return pl.pallas_call(
        _rwkv_out_kernel,
        grid=(bsz, length // ROW_TILE),
        in_specs=[dir_spec, dir_spec, _const_spec(gn), _const_spec(ones_bd)],
        out_specs=_tok_spec(w),
        out_shape=jax.ShapeDtypeStruct((bsz, length, w), F32),
        compiler_params=pltpu.CompilerParams(dimension_semantics=("arbitrary", "arbitrary")),
        name="rwkv_out",
    )(y, gate_bonus, gn, ones_bd)


SC_CORES = 2
SC_SUBCORES = 16
SC_WORKERS = SC_CORES * SC_SUBCORES
GATHER_ROWS = 64
PEER_SLOTS = 128
PEER_TOKENS_PER_STEP = 16


PEER_HEADS = 8
PEER_KEYS = 128
PEER_TOPK = 16
PEER_HALF = 128
PEER_SELECT_TOKENS = 128


def _top_rows(s, payload=None):
    n_rows = s.shape[0]
    iota = lax.broadcasted_iota(jnp.int32, s.shape, 0).astype(F32)
    vals, picks = [], []
    for _ in range(PEER_TOPK):
        m = jnp.max(s, axis=0, keepdims=True)
        pos = jnp.min(jnp.where(s == m, iota, float(n_rows)), axis=0, keepdims=True)
        hit = iota == pos
        vals.append(m)
        picks.append(pos if payload is None else jnp.max(jnp.where(hit, payload, -1.0), axis=0, keepdims=True))
        s = jnp.where(hit, -jnp.inf, s)
    return jnp.concatenate(vals, axis=0), jnp.concatenate(picks, axis=0)


def _peer_select_kernel(h_ref, wq_ref, keys_ref, ids_ref, gate_ref):
    q = _mm(h_ref[...], wq_ref[...]).astype(BF16)
    for h in range(PEER_HEADS):
        halves = []
        for p in range(2):
            lst = 2 * h + p
            s = _mm_nt(keys_ref[lst], q[:, lst * PEER_HALF:(lst + 1) * PEER_HALF])
            halves.append(_top_rows(s))
        (v1, p1), (v2, p2) = halves
        sub8 = lax.broadcasted_iota(jnp.int32, (8, v1.shape[1]), 0)
        cand_rows, id_rows = [v1[0:1] + v2], [p1[0:1] * PEER_KEYS + p2]
        for i in range(1, 8):
            bound = PEER_TOPK // (i + 1)
            slab = v1[i:i + 1] + v2[0:8]
            cand_rows.append(slab if bound >= 8 else jnp.where(sub8 < bound, slab, -jnp.inf))
            id_rows.append(p1[i:i + 1] * PEER_KEYS + p2[0:8])
        cand_rows.append(v1[8:] + v2[0:1])
        id_rows.append(p1[8:] * PEER_KEYS + p2[0:1])
        best, ids = _top_rows(jnp.concatenate(cand_rows, axis=0), jnp.concatenate(id_rows, axis=0))
        e = jnp.exp(best - best[0:1])
        ids_ref[h * PEER_TOPK:(h + 1) * PEER_TOPK, :] = ids.astype(jnp.int32)
        gate_ref[h * PEER_TOPK:(h + 1) * PEER_TOPK, :] = e / jnp.sum(e, axis=0, keepdims=True)


def peer_select(h, wq, keys):
    n, d = h.shape
    tn = PEER_SELECT_TOKENS
    slots = PEER_HEADS * PEER_TOPK
    full = lambda a: pl.BlockSpec(a.shape, lambda i: (0,) * a.ndim)
    out = pl.BlockSpec((slots, tn), lambda i: (0, i))
    return pl.pallas_call(
        _peer_select_kernel,
        grid=(n // tn,),
        in_specs=[pl.BlockSpec((tn, d), lambda i: (i, 0)), full(wq), full(keys)],
        out_specs=[out, out],
        out_shape=[jax.ShapeDtypeStruct((slots, n), jnp.int32), jax.ShapeDtypeStruct((slots, n), F32)],
        compiler_params=pltpu.CompilerParams(dimension_semantics=("arbitrary",),
                                             vmem_limit_bytes=48 * 1024 * 1024),
        name="peer_select",
    )(h, wq, keys)


def pack_bf16_pairs(table):
    half = table.shape[1] // 2
    bits = lax.bitcast_convert_type(table.astype(BF16), jnp.uint16).astype(jnp.uint32)
    return bits[:, :half] | (bits[:, half:] << 16)


def sc_gather_rows(table, idx):
    n_rows, width = idx.shape[0], table.shape[1]
    per_worker = n_rows // SC_WORKERS
    n_pairs = per_worker // (2 * GATHER_ROWS)
    assert per_worker * SC_WORKERS == n_rows and n_pairs * 2 * GATHER_ROWS == per_worker
    mesh = plsc.VectorSubcoreMesh(core_axis_name="c", subcore_axis_name="s",
                                  num_cores=SC_CORES, num_subcores=SC_SUBCORES)

    @functools.partial(
        pl.kernel, mesh=mesh,
        out_type=jax.ShapeDtypeStruct((n_rows, width), table.dtype),
        scratch_types=[pltpu.VMEM((2, GATHER_ROWS), jnp.int32),
                       pltpu.VMEM((2, GATHER_ROWS, width), table.dtype),
                       pltpu.SemaphoreType.DMA((2,)),
                       pltpu.SemaphoreType.DMA((2,))],
        name="peer_sc_gather",
    )
    def gather(table_hbm, idx_hbm, out_hbm, idx_v, rows_v, gather_sem, write_sem):
        worker = lax.axis_index("s") * SC_CORES + lax.axis_index("c")
        base = worker * per_worker

        def rows_of(chunk):
            return pl.ds(pl.multiple_of(base + chunk * GATHER_ROWS, GATHER_ROWS), GATHER_ROWS)

        def gather_copy(slot):
            return pltpu.make_async_copy(table_hbm.at[idx_v.at[slot]], rows_v.at[slot], gather_sem.at[slot])

        def write_copy(chunk, slot):
            return pltpu.make_async_copy(rows_v.at[slot], out_hbm.at[rows_of(chunk)], write_sem.at[slot])

        def start_gather(chunk, slot):
            pltpu.sync_copy(idx_hbm.at[rows_of(chunk)], idx_v.at[slot])
            gather_copy(slot).start()

        start_gather(0, 0)

        @pl.loop(0, n_pairs)
        def _(g):
            even, odd = 2 * g, 2 * g + 1

            @pl.when(g > 0)
            def _():
                write_copy(odd - 2, 1).wait()
            start_gather(odd, 1)
            gather_copy(0).wait()
            write_copy(even, 0).start()
            gather_copy(1).wait()
            write_copy(odd, 1).start()
            write_copy(even, 0).wait()

            @pl.when(g + 1 < n_pairs)
            def _():
                start_gather(even + 2, 0)

        write_copy(2 * n_pairs - 1, 1).wait()

    return gather(table, idx)


def _unpack_pairs(words):
    lo = pltpu.bitcast(words << 16, F32)
    hi = pltpu.bitcast(words & jnp.uint32(0xFFFF0000), F32)
    return lo, hi


def _gelu_tanh(x):
    return 0.5 * x * (1.0 + jnp.tanh(0.7978845608028654 * (x + 0.044715 * (x * x * x))))


def _peer_expert_kernel(z_ref, gate_ref, ug_ref, vg_ref, o_ref):
    half = ug_ref.shape[2]
    gate_t = gate_ref[...].T
    for n in range(PEER_TOKENS_PER_STEP):
        z_lo, z_hi = z_ref[n:n + 1, :half], z_ref[n:n + 1, half:]
        u_lo, u_hi = _unpack_pairs(ug_ref[n])
        prod = u_lo * z_lo + u_hi * z_hi
        act = jnp.sum(prod, axis=1, keepdims=True)
        w = _gelu_tanh(act) * gate_t[:, n:n + 1]
        v_lo, v_hi = _unpack_pairs(vg_ref[n])
        o_ref[n:n + 1, :half] = jnp.sum(w * v_lo, axis=0, keepdims=True)
        o_ref[n:n + 1, half:] = jnp.sum(w * v_hi, axis=0, keepdims=True)


def peer_experts(z, gate, ug, vg):
    n, d = z.shape
    tn = PEER_TOKENS_PER_STEP
    slots, half = ug.shape[1], ug.shape[2]
    return pl.pallas_call(
        _peer_expert_kernel,
        grid=(n // tn,),
        in_specs=[pl.BlockSpec((tn, d), lambda i: (i, 0)),
                  pl.BlockSpec((tn, slots), lambda i: (i, 0)),
                  pl.BlockSpec((tn, slots, half), lambda i: (i, 0, 0)),
                  pl.BlockSpec((tn, slots, half), lambda i: (i, 0, 0))],
        out_specs=pl.BlockSpec((tn, d), lambda i: (i, 0)),
        out_shape=jax.ShapeDtypeStruct((n, d), F32),
        compiler_params=pltpu.CompilerParams(dimension_semantics=("arbitrary",),
                                             vmem_limit_bytes=48 * 1024 * 1024),
        name="peer_experts",
    )(z, gate, ug, vg)


S5_WIDTH = 512
S5_GROUP = 16
S5_GROUPS = S5_WIDTH // S5_GROUP
S5_STATE = 64
S5_MAX_RE = -1e-4
A_END = 3 * RWKV_WIDTH + 2 * RWKV_DECAY_RANK + 2 * RWKV_ICLR_RANK + RWKV_GATE_RANK
B_END = A_END + 2 * CONV_CHANNELS
C_END = B_END + S5_WIDTH
IN_WIDTH = C_END + 5 * RET_WIDTH


def _s5_discretise(lam_re, lam_im, log_dt, b_re, b_im):
    lam_re = jnp.minimum(lam_re.astype(F32), S5_MAX_RE)
    lam_im = lam_im.astype(F32)
    dt = jnp.exp(log_dt.astype(F32))[:, None]
    mag = jnp.exp(lam_re * dt)
    ang = lam_im * dt
    ab_re, ab_im = mag * jnp.cos(ang), mag * jnp.sin(ang)
    den = lam_re * lam_re + lam_im * lam_im
    nr, ni = ab_re - 1.0, ab_im
    f_re = (nr * lam_re + ni * lam_im) / den
    f_im = (ni * lam_re - nr * lam_im) / den
    b_re, b_im = b_re.astype(F32), b_im.astype(F32)
    bb_re = f_re[..., None] * b_re - f_im[..., None] * b_im
    bb_im = f_re[..., None] * b_im + f_im[..., None] * b_re
    return ab_re, ab_im, bb_re, bb_im


PEER_EXPERTS = PEER_KEYS * PEER_KEYS
SC_LANES = 16
SC_TOKENS = 4
DENSE_TOKENS = 640
DENSE_EXPERTS = 2048


def _sc_mesh():
    return plsc.VectorSubcoreMesh(core_axis_name="c", subcore_axis_name="s",
                                  num_cores=SC_CORES, num_subcores=SC_SUBCORES)


def sc_pick_experts(act, ids):
    n = ids.shape[0] // PEER_SLOTS
    e = PEER_EXPERTS
    n_steps = n // (SC_WORKERS * SC_TOKENS)
    assert n_steps * SC_TOKENS * SC_WORKERS == n
    blk = SC_TOKENS * PEER_SLOTS

    @functools.partial(
        pl.kernel, mesh=_sc_mesh(), out_type=jax.ShapeDtypeStruct((n * PEER_SLOTS,), F32),
        scratch_types=[pltpu.VMEM((SC_TOKENS * e,), F32), pltpu.VMEM((blk,), jnp.int32), pltpu.VMEM((blk,), F32)],
        compiler_params=pltpu.CompilerParams(needs_layout_passes=False), name="peer_sc_pick")
    def pick(act_hbm, ids_hbm, out_hbm, rows_v, idx_v, out_v):
        worker = lax.axis_index("s") * SC_CORES + lax.axis_index("c")

        @pl.loop(0, n_steps)
        def _(i):
            tok = (worker * n_steps + i) * SC_TOKENS
            slot0 = pl.multiple_of(tok * PEER_SLOTS, blk)
            pltpu.sync_copy(ids_hbm.at[pl.ds(slot0, blk)], idx_v)
            pltpu.sync_copy(act_hbm.at[pl.ds(pl.multiple_of(tok * e, SC_TOKENS * e), SC_TOKENS * e)], rows_v)
            for t in range(SC_TOKENS):
                for k in range(PEER_SLOTS // SC_LANES):
                    sl = pl.ds(t * PEER_SLOTS + k * SC_LANES, SC_LANES)
                    out_v[sl] = plsc.load_gather(rows_v, [idx_v[sl] + t * e])
            pltpu.sync_copy(out_v, out_hbm.at[pl.ds(slot0, blk)])

    return pick(act, ids)


def sc_spread_weights(ids, w):
    n = ids.shape[0] // PEER_SLOTS
    e = PEER_EXPERTS
    n_steps = n // (SC_WORKERS * SC_TOKENS)
    assert n_steps * SC_TOKENS * SC_WORKERS == n
    blk = SC_TOKENS * PEER_SLOTS

    @functools.partial(
        pl.kernel, mesh=_sc_mesh(), out_type=jax.ShapeDtypeStruct((n * e,), F32),
        scratch_types=[pltpu.VMEM((SC_TOKENS * e,), F32), pltpu.VMEM((blk,), jnp.int32), pltpu.VMEM((blk,), F32)],
        compiler_params=pltpu.CompilerParams(needs_layout_passes=False), name="peer_sc_spread")
    def spread(ids_hbm, w_hbm, out_hbm, dense_v, idx_v, w_v):
        worker = lax.axis_index("s") * SC_CORES + lax.axis_index("c")
        zeros = jnp.zeros((SC_LANES,), F32)

        @pl.loop(0, SC_TOKENS * e // SC_LANES)
        def _(i):
            dense_v[pl.ds(pl.multiple_of(i * SC_LANES, SC_LANES), SC_LANES)] = zeros

        @pl.loop(0, n_steps)
        def _(i):
            tok = (worker * n_steps + i) * SC_TOKENS
            slot0 = pl.multiple_of(tok * PEER_SLOTS, blk)
            pltpu.sync_copy(ids_hbm.at[pl.ds(slot0, blk)], idx_v)
            pltpu.sync_copy(w_hbm.at[pl.ds(slot0, blk)], w_v)
            for t in range(SC_TOKENS):
                for k in range(PEER_SLOTS // SC_LANES):
                    sl = pl.ds(t * PEER_SLOTS + k * SC_LANES, SC_LANES)
                    plsc.addupdate_scatter(dense_v, [idx_v[sl] + t * e], w_v[sl])
            pltpu.sync_copy(dense_v, out_hbm.at[pl.ds(pl.multiple_of(tok * e, SC_TOKENS * e), SC_TOKENS * e)])
            for t in range(SC_TOKENS):
                for k in range(PEER_SLOTS // SC_LANES):
                    sl = pl.ds(t * PEER_SLOTS + k * SC_LANES, SC_LANES)
                    plsc.store_scatter(dense_v, [idx_v[sl] + t * e], zeros)

    return spread(ids, w)


def _dense_act_kernel(h_ref, u_ref, o_ref):
    o_ref[...] = _mm_nt(h_ref[...], u_ref[...])


def peer_dense_act(h, u):
    n, d = h.shape
    e = u.shape[0]
    tn, te = DENSE_TOKENS, DENSE_EXPERTS
    assert n % tn == 0 and e % te == 0
    return pl.pallas_call(
        _dense_act_kernel,
        grid=(e // te, n // tn),
        in_specs=[pl.BlockSpec((tn, d), lambda j, i: (i, 0)), pl.BlockSpec((te, d), lambda j, i: (j, 0))],
        out_specs=pl.BlockSpec((tn, te), lambda j, i: (i, j)),
        out_shape=jax.ShapeDtypeStruct((n, e), F32),
        compiler_params=pltpu.CompilerParams(dimension_semantics=("arbitrary", "arbitrary"),
                                             vmem_limit_bytes=VMEM_LIMIT),
        name="peer_dense_act",
    )(h, u)


def _dense_out_kernel(w_ref, v_ref, o_ref):
    @pl.when(pl.program_id(1) == 0)
    def _():
        o_ref[...] = jnp.zeros_like(o_ref)
    o_ref[...] += _mm(w_ref[...], v_ref[...])


def peer_dense_out(wd, v):
    n, e = wd.shape
    d = v.shape[1]
    tn, te = DENSE_TOKENS, DENSE_EXPERTS
    assert n % tn == 0 and e % te == 0
    return pl.pallas_call(
        _dense_out_kernel,
        grid=(n // tn, e // te),
        in_specs=[pl.BlockSpec((tn, te), lambda i, j: (i, j)), pl.BlockSpec((te, d), lambda i, j: (j, 0))],
        out_specs=pl.BlockSpec((tn, d), lambda i, j: (i, 0)),
        out_shape=jax.ShapeDtypeStruct((n, d), F32),
        compiler_params=pltpu.CompilerParams(dimension_semantics=("arbitrary", "arbitrary"),
                                             vmem_limit_bytes=VMEM_LIMIT),
        name="peer_dense_out",
    )(wd, v)


def _peer_weight_kernel(a_ref, g_ref, o_ref):
    o_ref[...] = g_ref[...] * _gelu_tanh(a_ref[...])


def peer_weights(act_sel, gate):
    n, s = act_sel.shape
    assert n % DENSE_TOKENS == 0
    spec = pl.BlockSpec((DENSE_TOKENS, s), lambda i: (i, 0))
    return pl.pallas_call(
        _peer_weight_kernel, grid=(n // DENSE_TOKENS,), in_specs=[spec, spec], out_specs=spec,
        out_shape=jax.ShapeDtypeStruct((n, s), F32),
        compiler_params=pltpu.CompilerParams(dimension_semantics=("arbitrary",)),
        name="peer_weights",
    )(act_sel, gate)


def _peer_ffn_dense(h, w_q, sub_keys, u_bf16, v_bf16):
    n, d = h.shape
    keys = sub_keys.reshape(2 * PEER_HEADS, PEER_KEYS, PEER_HALF).astype(BF16)
    wq = w_q.astype(BF16)
    unit = math.lcm(DENSE_TOKENS, PEER_SELECT_TOKENS, SC_WORKERS * SC_TOKENS)
    n_blk = next(k for k in (2, 1) if n % (k * unit) == 0)
    nb = n // n_blk
    outs = []
    for i in range(n_blk):
        h_b = h[i * nb:(i + 1) * nb]
        ids_t, gate_t = peer_select(h_b, wq, keys)
        ids = ids_t.T.reshape(-1)
        act = peer_dense_act(h_b, u_bf16)
        act_sel = sc_pick_experts(act.reshape(-1), ids).reshape(nb, PEER_SLOTS)
        w = peer_weights(act_sel, gate_t.T)
        wd = sc_spread_weights(ids, w.reshape(-1)).reshape(nb, PEER_EXPERTS)
        outs.append(peer_dense_out(wd, v_bf16))
    return jnp.concatenate(outs, axis=0)


def _peer_ffn(h, w_q, sub_keys, u_packed, v_packed):
    n, d = h.shape
    keys = sub_keys.reshape(2 * PEER_HEADS, PEER_KEYS, PEER_HALF).astype(BF16)
    wq = w_q.astype(BF16)
    step = 2 * SC_WORKERS * GATHER_ROWS
    n_blk = next(k for k in (8, 6, 4, 3, 2, 1)
                 if n % (k * PEER_SELECT_TOKENS) == 0 and (n // k * PEER_SLOTS) % step == 0)
    nb = n // n_blk
    outs = []
    for i in range(n_blk):
        h_b = h[i * nb:(i + 1) * nb]
        ids_t, gate_t = peer_select(h_b, wq, keys)
        flat_ids = ids_t.T.reshape(-1)
        ug = sc_gather_rows(u_packed, flat_ids).reshape(nb, PEER_SLOTS, d // 2)
        vg = sc_gather_rows(v_packed, flat_ids).reshape(nb, PEER_SLOTS, d // 2)
        outs.append(peer_experts(h_b, gate_t.T, ug, vg))
    return jnp.concatenate(outs, axis=0)


def kernel(x, c, ctx, c_ctx, ada_w, ada_b, norm1_g, norm2_g, w_in, rwkv_shift, rwkv_w0, rwkv_w2,
           rwkv_a0, rwkv_a2, rwkv_g2, rwkv_kk, rwkv_ka, rwkv_rk, rwkv_gn_g, rwkv_gn_b,
           conv_dw, conv_db, conv_ln_g, conv_ln_b, s5_lam_re, s5_lam_im, s5_log_dt,
           s5_b_re, s5_b_im, s5_c_re, s5_c_im, s5_d, s5_glu_w, s5_glu_b, ret_gn_g, ret_gn_b,
           w_branch, w_merge, b_merge, w_out, peer_wq, peer_keys, peer_u, peer_v, final_g):
    depth = ada_w.shape[0]
    bsz, n_lat, d = x.shape
    n_ctx = ctx.shape[1]
    assert n_ctx == ROW_TILE and n_lat % ROW_TILE == 0
    xa = jnp.concatenate([ctx, x], axis=1)
    cos2, sin2 = rope_tables(n_lat, n_ctx)
    ones_bd = jnp.kron(jnp.eye(RWKV_HEADS, dtype=F32), jnp.ones((RWKV_HEAD_DIM, RWKV_HEAD_DIM), F32)).astype(BF16)
    cond = jnp.zeros((8, d), F32).at[:bsz].set(c).at[bsz].set(c_ctx)
    row = lambda v: v.reshape(1, -1)
    w = RWKV_WIDTH
    for l in range(depth):
        mod = ada_modulation(cond, ada_w[l], row(ada_b[l])).reshape(8, 6, d)
        mods = jnp.stack([jnp.broadcast_to(mod[bsz], (bsz, 6, d)), mod[:bsz]], axis=1)
        g1 = row(norm1_g[l])
        w_in_l = w_in[l].astype(BF16)
        p_a, p_b, p_c, p_d = (in_proj(xa, g1, mods, w_in_l[:, lo:hi])
                              for lo, hi in ((0, A_END), (A_END, B_END), (B_END, C_END), (C_END, IN_WIDTH)))
        half = jnp.zeros((RWKV_DECAY_RANK, w), F32)
        w2 = jnp.stack([jnp.concatenate([rwkv_w2[l, 0], half]), jnp.concatenate([half, rwkv_w2[l, 1]])]).astype(BF16)
        a2 = jnp.stack([jnp.concatenate([rwkv_a2[l, 0], half]), jnp.concatenate([half, rwkv_a2[l, 1]])]).astype(BF16)
        vecs = jnp.stack([rwkv_kk[l], rwkv_ka[l], rwkv_rk[l].reshape(w), jnp.zeros((w,), F32),
                          rwkv_w0[l, 0], rwkv_w0[l, 1], rwkv_a0[l, 0], rwkv_a0[l, 1]])
        r, v, kk, gate_bonus, k_dir, b_dir, lw_dir = rwkv_prep(
            p_a, rwkv_shift[l], w2, a2, rwkv_g2[l].astype(BF16), vecs, ones_bd)
        y = rwkv_scan(r, v, kk, k_dir, b_dir, lw_dir, n_ctx)
        ya = rwkv_out(y, gate_bonus, jnp.stack([rwkv_gn_g[l], rwkv_gn_b[l]]), ones_bd)
        yb = conformer_conv(p_b, conv_dw[l], conv_db[l], conv_ln_g[l], conv_ln_b[l])
        eye = jnp.eye(S5_GROUPS, dtype=F32)
        n_state = S5_GROUPS * S5_STATE
        y_dirs = []
        for dr in range(2):
            ab_re, ab_im, bb_re, bb_im = _s5_discretise(s5_lam_re[l, dr], s5_lam_im[l, dr], s5_log_dt[l, dr],
                                                        s5_b_re[l, dr], s5_b_im[l, dr])
            blk_in = lambda bb: jnp.einsum("gph,gk->ghkp", bb, eye).reshape(S5_WIDTH, n_state).astype(BF16)
            blk_out = lambda cc: jnp.einsum("ghp,gk->kpgh", cc.astype(F32), eye).reshape(n_state, S5_WIDTH)
            c_cat = jnp.concatenate([blk_out(s5_c_re[l, dr]), -blk_out(s5_c_im[l, dr])], axis=0).astype(BF16)
            pw_re, pw_im = [ab_re.reshape(n_state)], [ab_im.reshape(n_state)]
            for _ in range(7):
                pw_re, pw_im = (pw_re + [pw_re[-1] * pw_re[0] - pw_im[-1] * pw_im[0]],
                                pw_im + [pw_re[-1] * pw_im[0] + pw_im[-1] * pw_re[0]])
            order = slice(None, None, -1) if dr == 1 else slice(None)
            y_dirs.append(s5_scan(p_c, blk_in(bb_re), blk_in(bb_im), c_cat, jnp.stack(pw_re[order]),
                                  jnp.stack(pw_im[order]), n_ctx, reverse=(dr == 1)))
        yc = s5_out(p_c, y_dirs[0], y_dirs[1], row(s5_d[l]), s5_glu_w[l].astype(BF16), row(s5_glu_b[l]))
        yd = retention_out(retention_scan(p_d, cos2, sin2, n_ctx), p_d, ret_gn_g[l], ret_gn_b[l])
        xa = merge_residual(xa, g1, mods, (ya, yb, yc, yd), w_merge[l].astype(BF16), b_merge[l],
                            w_branch[l].astype(BF16), w_out[l].astype(BF16))
        h = norm_modulate2(xa, row(norm2_g[l]), mods)
        f = _peer_ffn_dense(h.reshape(-1, d), peer_wq[l], peer_keys[l],
                            peer_u[l].astype(BF16), peer_v[l].astype(BF16)).reshape(bsz, -1, d)
        xa = residual2(xa, f, mods, row(final_g), final_norm=(l == depth - 1))
    return xa[:, n_ctx:]
```
